```python
import math
import jax, jax.numpy as jnp
from jax import lax
import numpy as np

D_MODEL = 1024
BATCH = 8
SEQ = 2048
DEPTH = 1
DEC_BATCH = 128
DEC_SEQ = 1
PAST_LEN = 16384
PAGE_SIZE = 128

GLA_HEADS = 4
GLA_DK = D_MODEL // 2
GLA_DV = D_MODEL
GLA_HK = GLA_DK // GLA_HEADS
GLA_HV = GLA_DV // GLA_HEADS
GLA_LOWRANK = 16
GLA_TAU = 16.0
GLA_CHUNK = 64
GM_GROUPS = 4
GM_WIDTH = D_MODEL
GM_GC = GM_WIDTH // GM_GROUPS
GM_CHUNK = 128
D_FF = 2816
N_MOD = 9
EPS = 1e-6
IN_SPLITS = (GLA_DK, GLA_DK, GLA_DV, GLA_DV, GLA_LOWRANK, GM_WIDTH, GM_WIDTH, 2 * D_MODEL)
D_IN = 2 * GLA_DK + 2 * GLA_DV + GLA_LOWRANK + 2 * GM_WIDTH + 2 * D_MODEL

kernel_name = "hybrid_gla_chunk_gmlp_macaron_decoder_step"


def rmsnorm(x, w):
    xf = x.astype(jnp.float32)
    y = xf * lax.rsqrt(jnp.mean(xf * xf, axis=-1, keepdims=True) + EPS)
    return (y * w.astype(jnp.float32)).astype(x.dtype)


def layernorm(x, w, b):
    xf = x.astype(jnp.float32)
    mu = jnp.mean(xf, axis=-1, keepdims=True)
    d = xf - mu
    y = d * lax.rsqrt(jnp.mean(d * d, axis=-1, keepdims=True) + EPS)
    return (y * w.astype(jnp.float32) + b.astype(jnp.float32)).astype(x.dtype)


def modulate(h, shift, scale):
    return h * (1.0 + scale[:, None, :]) + shift[:, None, :]


def swiglu(h, w13, w2):
    a, b = jnp.split(h @ w13, 2, axis=-1)
    return (jax.nn.silu(a) * b) @ w2


def gla_chunked(q, k, v, g, s0):
    B, T, H, K = q.shape
    V = v.shape[-1]
    C = math.gcd(T, GLA_CHUNK)
    N = T // C

    def to_chunks(a):
        return jnp.moveaxis(a.reshape(B, N, C, H, a.shape[-1]), 1, 0)

    qc, kc, vc, gc = to_chunks(q), to_chunks(k), to_chunks(v), to_chunks(g)
    causal = jnp.tril(jnp.ones((C, C), dtype=bool))[None, :, :, None, None]

    def step(S, inp):
        qi, ki, vi, gi = inp
        b = jnp.cumsum(gi.astype(jnp.float32), axis=1)
        rel = b[:, :, None] - b[:, None, :]
        decay = jnp.exp(jnp.where(causal, rel, -jnp.inf))
        scores = jnp.einsum('bthk,bshk,btshk->bhts', qi, ki, decay)
        o = (jnp.einsum('bhts,bshv->bthv', scores, vi)
             + jnp.einsum('bthk,bhkv->bthv', qi * jnp.exp(b), S))
        b_last = b[:, -1]
        k_dec = ki * jnp.exp(b_last[:, None] - b)
        S_new = jnp.exp(b_last)[..., None] * S + jnp.einsum('bshk,bshv->bhkv', k_dec, vi)
        return S_new.astype(S.dtype), o.astype(vi.dtype)

    S, o = lax.scan(step, s0, (qc, kc, vc, gc))
    o = jnp.moveaxis(o, 0, 1).reshape(B, T, H, V)
    return o, S


def chunk_spatial_gate(u, v, w_s, b_s):
    B, T, W = v.shape
    L = min(T, GM_CHUNK)
    N = T // L
    mask = jnp.tril(jnp.ones((L, L), dtype=bool))
    ws = jnp.where(mask[None], w_s[:, :L, :L], 0.0)
    vc = v.reshape(B, N, L, GM_GROUPS, GM_GC)
    mixed = jnp.einsum('gts,bnsgc->bntgc', ws, vc) + b_s[:, :L].T[None, None, :, :, None]
    return u * mixed.reshape(B, T, W)


def token_mix(h, s0, w_in, w_a2, b_a, gla_norm_w, w_pa, gm_ln_w, gm_ln_b, gm_ws, gm_bs, w_pb, w_o):
    B, T, _ = h.shape
    p = h @ w_in
    q, k, v, r, a_lr, u, gv, gates = jnp.split(p, [int(i) for i in np.cumsum(IN_SPLITS)[:-1]], axis=-1)
    log_a = jax.nn.log_sigmoid(a_lr @ w_a2 + b_a) / GLA_TAU
    qh = q.reshape(B, T, GLA_HEADS, GLA_HK) * (GLA_HK ** -0.5)
    kh = k.reshape(B, T, GLA_HEADS, GLA_HK)
    vh = v.reshape(B, T, GLA_HEADS, GLA_HV)
    gh = log_a.reshape(B, T, GLA_HEADS, GLA_HK)
    o, S = gla_chunked(qh, kh, vh, gh, s0)
    o = rmsnorm(o, gla_norm_w).reshape(B, T, GLA_DV) * jax.nn.silu(r)
    y_a = o @ w_pa
    u = jax.nn.gelu(u, approximate=False)
    gv_n = layernorm(jax.nn.gelu(gv, approximate=False), gm_ln_w, gm_ln_b)
    y_b = chunk_spatial_gate(u, gv_n, gm_ws, gm_bs) @ w_pb
    ga, gb = jnp.split(gates, 2, axis=-1)
    merged = jax.nn.sigmoid(ga) * y_a + jax.nn.sigmoid(gb) * y_b
    return merged @ w_o, S, gv_n


def hybrid_layer(x, c, s0, w_ada, b_ada, norm1_w, ffn1_w13, ffn1_w2, norm2_w, w_in, w_a2, b_a,
                 gla_norm_w, w_pa, gm_ln_w, gm_ln_b, gm_ws, gm_bs, w_pb, w_o, norm3_w,
                 ffn2_w13, ffn2_w2):
    mod = jax.nn.silu(c) @ w_ada + b_ada
    sh1, sc1, g1, sh2, sc2, g2, sh3, sc3, g3 = jnp.split(mod, N_MOD, axis=-1)
    h = modulate(rmsnorm(x, norm1_w), sh1, sc1)
    x = x + 0.5 * g1[:, None, :] * swiglu(h, ffn1_w13, ffn1_w2)
    h = modulate(rmsnorm(x, norm2_w), sh2, sc2)
    m, S, gv_n = token_mix(h, s0, w_in, w_a2, b_a, gla_norm_w, w_pa, gm_ln_w, gm_ln_b,
                           gm_ws, gm_bs, w_pb, w_o)
    x = x + g2[:, None, :] * m
    h = modulate(rmsnorm(x, norm3_w), sh3, sc3)
    x = x + 0.5 * g3[:, None, :] * swiglu(h, ffn2_w13, ffn2_w2)
    return x, S, gv_n


def setup_inputs(seed: int = 0) -> dict:
    key = jax.random.key(seed)
    ks = jax.random.split(key, 32)
    f32 = jnp.float32

    def nrm(k, shape, scale):
        return jax.random.normal(k, shape, f32) * scale

    D = D_MODEL
    return {
        "x_prompt": nrm(ks[0], (BATCH, SEQ, D), 1.0),
        "x_sample": nrm(ks[1], (DEC_BATCH, DEC_SEQ, D), 1.0),
        "state_gla": nrm(ks[2], (DEPTH, DEC_BATCH, GLA_HEADS, GLA_HK, GLA_HV), 0.5),
        "c_prompt": nrm(ks[3], (BATCH, D), 1.0),
        "c_sample": nrm(ks[4], (DEC_BATCH, D), 1.0),
        "w_ada": nrm(ks[5], (DEPTH, D, N_MOD * D), 0.3 * D ** -0.5),
        "b_ada": nrm(ks[6], (DEPTH, N_MOD * D), 0.02),
        "norm1_w": 1.0 + nrm(ks[7], (DEPTH, D), 0.05),
        "ffn1_w13": nrm(ks[8], (DEPTH, D, 2 * D_FF), D ** -0.5),
        "ffn1_w2": nrm(ks[9], (DEPTH, D_FF, D), D_FF ** -0.5),
        "norm2_w": 1.0 + nrm(ks[10], (DEPTH, D), 0.05),
        "w_in": nrm(ks[11], (DEPTH, D, D_IN), D ** -0.5),
        "w_a2": nrm(ks[12], (DEPTH, GLA_LOWRANK, GLA_DK), GLA_LOWRANK ** -0.5),
        "b_a": nrm(ks[13], (DEPTH, GLA_DK), 0.1),
        "gla_norm_w": 1.0 + nrm(ks[14], (DEPTH, GLA_HEADS, GLA_HV), 0.05),
        "w_pa": nrm(ks[15], (DEPTH, GLA_DV, D), GLA_DV ** -0.5),
        "gm_ln_w": 1.0 + nrm(ks[16], (DEPTH, GM_WIDTH), 0.05),
        "gm_ln_b": nrm(ks[17], (DEPTH, GM_WIDTH), 0.02),
        "gm_ws": nrm(ks[18], (DEPTH, GM_GROUPS, GM_CHUNK, GM_CHUNK), GM_CHUNK ** -0.5),
        "gm_bs": 1.0 + nrm(ks[19], (DEPTH, GM_GROUPS, GM_CHUNK), 0.1),
        "w_pb": nrm(ks[20], (DEPTH, GM_WIDTH, D), GM_WIDTH ** -0.5),
        "w_o": nrm(ks[21], (DEPTH, D, D), D ** -0.5),
        "norm3_w": 1.0 + nrm(ks[22], (DEPTH, D), 0.05),
        "ffn2_w13": nrm(ks[23], (DEPTH, D, 2 * D_FF), D ** -0.5),
        "ffn2_w2": nrm(ks[24], (DEPTH, D_FF, D), D_FF ** -0.5),
        "normf_w": 1.0 + nrm(ks[25], (D,), 0.05),
    }


def reference(x_prompt, x_sample, state_gla, c_prompt, c_sample, w_ada, b_ada, norm1_w,
              ffn1_w13, ffn1_w2, norm2_w, w_in, w_a2, b_a, gla_norm_w, w_pa, gm_ln_w, gm_ln_b,
              gm_ws, gm_bs, w_pb, w_o, norm3_w, ffn2_w13, ffn2_w2, normf_w):
    xp, xs = x_prompt, x_sample
    sp_list, ss_list, vs_list = [], [], []
    for l in range(DEPTH):
        params = (w_ada[l], b_ada[l], norm1_w[l], ffn1_w13[l], ffn1_w2[l], norm2_w[l], w_in[l],
                  w_a2[l], b_a[l], gla_norm_w[l], w_pa[l], gm_ln_w[l], gm_ln_b[l], gm_ws[l],
                  gm_bs[l], w_pb[l], w_o[l], norm3_w[l], ffn2_w13[l], ffn2_w2[l])
        s0p = jnp.zeros((xp.shape[0], GLA_HEADS, GLA_HK, GLA_HV), dtype=state_gla.dtype)
        xp, sp, _ = hybrid_layer(xp, c_prompt, s0p, *params)
        xs, ss, vs = hybrid_layer(xs, c_sample, state_gla[l], *params)
        sp_list.append(sp)
        ss_list.append(ss)
        vs_list.append(vs)
    y_prompt = rmsnorm(xp, normf_w)
    y_sample = rmsnorm(xs, normf_w)
    state_gla_prompt = jnp.stack(sp_list)
    state_gla_sample = jnp.stack(ss_list)
    gm_v_sample = jnp.stack(vs_list)
    return (y_prompt, y_sample, state_gla_prompt, state_gla_sample, gm_v_sample)
```

```python
import functools

import numpy as np
import jax
import jax.numpy as jnp
from jax import lax
from jax.experimental import pallas as pl
from jax.experimental.pallas import tpu as pltpu

F32 = jnp.float32
BF16 = jnp.bfloat16

D = 1024
HEADS = 4
HK = 128
HV = 256
DK = HEADS * HK
DV = HEADS * HV
LOWRANK = 16
TAU = 16.0
GROUPS = 4
GC = D // GROUPS
GM_CHUNK = 128
F = 2816
N_MOD = 9
EPS = 1e-6

CH = 128
N_LEVELS = 7
LANE = 128

C_Q, C_K, C_V, C_R, C_U, C_GV, C_GA, C_GB, C_ALR = 0, 512, 1024, 2048, 3072, 4096, 5120, 6144, 7168
D_INR = C_ALR + LANE

TM_P = 256
VMEM_LIMIT = 58 * 1024 * 1024


def _gla_constants():
  t = np.arange(CH)
  j = np.arange(CH)
  a_rows = []
  masks = []
  for lvl in range(N_LEVELS):
    m = 1 << lvl
    base = (t // (2 * m)) * (2 * m)
    bd = base + m - 1
    upper = t >= base + m
    sel_q = upper[:, None] & (j[None, :] > bd[:, None]) & (j[None, :] <= t[:, None])
    sel_k = (~upper)[:, None] & (j[None, :] > t[:, None]) & (j[None, :] <= bd[:, None])
    a_rows.append((sel_q | sel_k).astype(np.float32))
    same_pair = (t[:, None] // (2 * m)) == (t[None, :] // (2 * m))
    masks.append((upper[:, None] & (~upper)[None, :] & same_pair).astype(np.float32))
  a_rows.append((j[None, :] <= t[:, None]).astype(np.float32))
  a_rows.append((j[None, :] > t[:, None]).astype(np.float32))
  masks.append(np.eye(CH, dtype=np.float32))
  return np.concatenate(a_rows, axis=0), np.stack(masks)


def _dot(a, b):
  return jnp.dot(a, b, preferred_element_type=F32)


def _dot_nt(a, b):
  return lax.dot_general(a, b, (((1,), (1,)), ((), ())), preferred_element_type=F32)


def _rms(x, w):
  ms = jnp.mean(x * x, axis=-1, keepdims=True)
  return x * lax.rsqrt(ms + EPS) * w


def _gelu(x):
  return 0.5 * x * (1.0 + lax.erf(x * (2.0 ** -0.5)))


def _log_sigmoid(z):
  return jnp.minimum(z, 0.0) - jnp.log1p(jnp.exp(-jnp.abs(z)))


def _const_spec(shape):
  nd = len(shape)
  return pl.BlockSpec(shape, lambda *_: (0,) * nd, pipeline_mode=pl.Buffered(1))


def _ada_kernel(c_ref, w_ref, b_ref, o_ref):
  c = c_ref[...]
  s = (c * jax.nn.sigmoid(c)).astype(BF16)
  o_ref[...] = _dot(s, w_ref[...].astype(BF16)) + b_ref[...]


def _ada_call(c_all, w_ada, b_ada):
  rows = c_all.shape[0]
  tn = 1024
  return pl.pallas_call(
      _ada_kernel,
      grid=(N_MOD * D // tn,),
      in_specs=[
          pl.BlockSpec((rows, D), lambda i: (0, 0)),
          pl.BlockSpec((D, tn), lambda i: (0, i)),
          pl.BlockSpec((1, tn), lambda i: (0, i)),
      ],
      out_specs=pl.BlockSpec((rows, tn), lambda i: (0, i)),
      out_shape=jax.ShapeDtypeStruct((rows, N_MOD * D), F32),
      compiler_params=pltpu.CompilerParams(dimension_semantics=("arbitrary",)),
      name="ada",
  )(c_all, w_ada, b_ada)


def _ffn_kernel(x_ref, sh_ref, sc_ref, g_ref, nw_ref, w13_ref, w2_ref, *rest, final_norm):
  o_ref = rest[-1]
  x = x_ref[...]
  h = (_rms(x, nw_ref[...]) * (1.0 + sc_ref[...]) + sh_ref[...]).astype(BF16)
  a = _dot(h, w13_ref[:, :F])
  b = _dot(h, w13_ref[:, F:])
  p = (a * jax.nn.sigmoid(a) * b).astype(BF16)
  out = x + 0.5 * g_ref[...] * _dot(p, w2_ref[...])
  if final_norm:
    out = _rms(out, rest[0][...])
  o_ref[...] = out


def _mod_spec(per_row, rows_per_batch, tm, seg):
  if per_row:
    return pl.BlockSpec((tm, D), lambda i: (i, seg))
  tiles = rows_per_batch // tm
  return pl.BlockSpec((None, 1, D), lambda i: (i // tiles, 0, seg))


def _ffn_call(x, mod, segs, nw, w13, w2, nf, *, per_row, rows_per_batch, tm, name):
  rows = x.shape[0]
  in_specs = [pl.BlockSpec((tm, D), lambda i: (i, 0))]
  in_specs += [_mod_spec(per_row, rows_per_batch, tm, s) for s in segs]
  in_specs += [_const_spec((1, D)), _const_spec((D, 2 * F)), _const_spec((F, D))]
  args = [x, mod, mod, mod, nw, w13, w2]
  if nf is not None:
    in_specs.append(_const_spec((1, D)))
    args.append(nf)
  return pl.pallas_call(
      functools.partial(_ffn_kernel, final_norm=nf is not None),
      grid=(rows // tm,),
      in_specs=in_specs,
      out_specs=pl.BlockSpec((tm, D), lambda i: (i, 0)),
      out_shape=jax.ShapeDtypeStruct((rows, D), F32),
      compiler_params=pltpu.CompilerParams(
          dimension_semantics=("arbitrary",), vmem_limit_bytes=VMEM_LIMIT),
      name=name,
  )(*args)


def _project(x, sh, sc, nw_ref, w_in_ref, w_a2_ref, b_a_ref, p_ref):
  h = (_rms(x, nw_ref[...]) * (1.0 + sc) + sh).astype(BF16)
  p_ref[...] = _dot(h, w_in_ref[...])
  a_lr = p_ref[:, C_ALR:C_ALR + LANE].astype(BF16)
  z = _dot(a_lr, w_a2_ref[...]) + b_a_ref[...]
  return _log_sigmoid(z) * (1.0 / TAU)


def _head_norm_gate(o, p_ref, gnw_ref):
  parts = []
  for hh in range(HEADS):
    sl = slice(hh * HV, (hh + 1) * HV)
    parts.append(_rms(o[:, sl], gnw_ref[:, sl]))
  r = p_ref[:, C_R:C_R + DV]
  return jnp.concatenate(parts, axis=-1) * (r * jax.nn.sigmoid(r))


def _gv_norm(p_ref, lnw_ref, lnb_ref):
  gv = _gelu(p_ref[:, C_GV:C_GV + D])
  mu = jnp.mean(gv, axis=-1, keepdims=True)
  d = gv - mu
  var = jnp.mean(d * d, axis=-1, keepdims=True)
  return d * lax.rsqrt(var + EPS) * lnw_ref[...] + lnb_ref[...]


def _merge_out(x, g2, ya, yb, p_ref, w_o_ref):
  ga = p_ref[:, C_GA:C_GA + D]
  gb = p_ref[:, C_GB:C_GB + D]
  merged = (jax.nn.sigmoid(ga) * ya + jax.nn.sigmoid(gb) * yb).astype(BF16)
  return x + g2 * _dot(merged, w_o_ref[...])


def _mix_prompt_kernel(x_ref, sh_ref, sc_ref, g2_ref, nw_ref, w_in_ref, w_a2_ref, b_a_ref,
                       amat_ref, mask_ref, gnw_ref, w_pa_ref, lnw_ref, lnb_ref, ws_ref, bs_ref,
                       w_pb_ref, w_o_ref, o_ref, s_out_ref, st_ref, p_ref, oa_ref, mx_ref):
  j = pl.program_id(1)
  tm = x_ref.shape[0]

  @pl.when(j == 0)
  def _():
    st_ref[...] = jnp.zeros_like(st_ref)

  x = x_ref[...]
  g = _project(x, sh_ref[...], sc_ref[...], nw_ref, w_in_ref, w_a2_ref, b_a_ref, p_ref)

  amat = amat_ref[...]
  for c in range(tm // CH):
    rows = slice(c * CH, (c + 1) * CH)
    gc = g[rows, :]
    g_hi = gc.astype(BF16)
    g_lo = (gc - g_hi.astype(F32)).astype(BF16)
    e_all = _dot(amat, g_hi) + _dot(amat, g_lo)
    for hh in range(HEADS):
      ck = slice(hh * HK, (hh + 1) * HK)
      q = p_ref[rows, C_Q + hh * HK:C_Q + (hh + 1) * HK] * (HK ** -0.5)
      k = p_ref[rows, C_K + hh * HK:C_K + (hh + 1) * HK]
      v32 = p_ref[rows, C_V + hh * HV:C_V + (hh + 1) * HV]
      v = v32.astype(BF16)
      scores = mask_ref[N_LEVELS] * _dot_nt(q.astype(BF16), k.astype(BF16))
      for lvl in range(N_LEVELS):
        e = jnp.exp(e_all[lvl * CH:(lvl + 1) * CH, ck])
        scores = scores + mask_ref[lvl] * _dot_nt((q * e).astype(BF16), (k * e).astype(BF16))
      b_incl = e_all[N_LEVELS * CH:(N_LEVELS + 1) * CH, ck]
      b_rev = e_all[(N_LEVELS + 1) * CH:(N_LEVELS + 2) * CH, ck]
      qe = (q * jnp.exp(b_incl)).astype(BF16)
      kd = (k * jnp.exp(b_rev)).astype(BF16)
      st = st_ref[hh]
      oa_ref[rows, hh * HV:(hh + 1) * HV] = _dot(scores.astype(BF16), v) + _dot_nt(qe, st.astype(BF16))
      decay = jnp.exp(b_incl[CH - 1:CH, :])
      st_ref[hh] = st * decay + _dot(v32.T.astype(BF16), kd)

  ya = _dot(_head_norm_gate(oa_ref[...], p_ref, gnw_ref).astype(BF16), w_pa_ref[...])

  gvn = _gv_norm(p_ref, lnw_ref, lnb_ref).astype(BF16)
  ti = lax.broadcasted_iota(jnp.int32, (GM_CHUNK, GM_CHUNK), 0)
  si = lax.broadcasted_iota(jnp.int32, (GM_CHUNK, GM_CHUNK), 1)
  for gg in range(GROUPS):
    ws = jnp.where(si <= ti, ws_ref[gg], 0.0).astype(BF16)
    cols = slice(gg * GC, (gg + 1) * GC)
    for c in range(tm // GM_CHUNK):
      rows = slice(c * GM_CHUNK, (c + 1) * GM_CHUNK)
      mx_ref[rows, cols] = _dot(ws, gvn[rows, cols]) + bs_ref[:, cols]
  u = _gelu(p_ref[:, C_U:C_U + D])
  yb = _dot((u * mx_ref[...]).astype(BF16), w_pb_ref[...])

  o_ref[...] = _merge_out(x, g2_ref[...], ya, yb, p_ref, w_o_ref)

  @pl.when(j == pl.num_programs(1) - 1)
  def _():
    for hh in range(HEADS):
      s_out_ref[hh] = st_ref[hh].T


def _mix_prompt_call(x, mod3, nw, w_in, w_a2, b_a, amat, masks, gnw, w_pa, lnw, lnb, ws, bs_full,
                     w_pb, w_o, *, batch, seq):
  tm = TM_P
  tiles = seq // tm
  row_spec = pl.BlockSpec((tm, D), lambda b, j: (b * tiles + j, 0))

  def mod_spec(seg):
    return pl.BlockSpec((None, 1, D), lambda b, j: (b, 0, seg))

  in_specs = [
      row_spec, mod_spec(3), mod_spec(4), mod_spec(5),
      _const_spec((1, D)), _const_spec((D, D_INR)), _const_spec((LANE, DK)), _const_spec((1, DK)),
      _const_spec(amat.shape), _const_spec(masks.shape), _const_spec((1, DV)),
      _const_spec((DV, D)), _const_spec((1, D)), _const_spec((1, D)),
      _const_spec((GROUPS, GM_CHUNK, GM_CHUNK)), _const_spec((GM_CHUNK, D)),
      _const_spec((D, D)), _const_spec((D, D)),
  ]
  out_specs = [
      row_spec,
      pl.BlockSpec((None, HEADS, HK, HV), lambda b, j: (b, 0, 0, 0)),
  ]
  return pl.pallas_call(
      _mix_prompt_kernel,
      grid=(batch, tiles),
      in_specs=in_specs,
      out_specs=out_specs,
      out_shape=[jax.ShapeDtypeStruct((batch * seq, D), F32),
                 jax.ShapeDtypeStruct((batch, HEADS, HK, HV), F32)],
      scratch_shapes=[
          pltpu.VMEM((HEADS, HV, HK), F32),
          pltpu.VMEM((tm, D_INR), F32),
          pltpu.VMEM((tm, DV), F32),
          pltpu.VMEM((tm, D), F32),
      ],
      compiler_params=pltpu.CompilerParams(
          dimension_semantics=("arbitrary", "arbitrary"), vmem_limit_bytes=VMEM_LIMIT),
      name="mix_prompt",
  )(x, mod3, mod3, mod3, nw, w_in, w_a2, b_a, amat, masks, gnw, w_pa, lnw, lnb, ws, bs_full,
    w_pb, w_o)


NB = 8


def _mix_sample_kernel(x_ref, sh_ref, sc_ref, g2_ref, nw_ref, w_in_ref, w_a2_ref, b_a_ref,
                       gnw_ref, w_pa_ref, lnw_ref, lnb_ref, ws0_ref, bs0_ref, w_pb_ref, w_o_ref,
                       s_ref, o_ref, s_out_ref, gvn_ref,
                       p_ref, oa_ref, bc_ref, kt_ref, v_ref):
  i = pl.program_id(0)
  rows = x_ref.shape[0]

  @pl.when(i == 0)
  def _():
    g = _project(x_ref[...], sh_ref[...], sc_ref[...], nw_ref, w_in_ref, w_a2_ref, b_a_ref, p_ref)
    q = p_ref[:, C_Q:C_Q + DK] * (HK ** -0.5)
    k = p_ref[:, C_K:C_K + DK]
    eb = jnp.exp(g)
    eb_hi = eb.astype(BF16)
    eb_lo = (eb - eb_hi.astype(F32)).astype(BF16)
    qe = q * eb
    for hh in range(HEADS):
      ck = slice(hh * HK, (hh + 1) * HK)
      bc_ref[hh, 0:HK, :] = eb_hi[:, ck].astype(F32).T.astype(BF16)
      bc_ref[hh, HK:2 * HK, :] = eb_lo[:, ck].astype(F32).T.astype(BF16)
      bc_ref[hh, 2 * HK:3 * HK, :] = qe[:, ck].T.astype(BF16)
      kt_ref[hh] = k[:, ck].T.astype(BF16)
      vh = p_ref[:, C_V + hh * HV:C_V + (hh + 1) * HV]
      v_ref[hh] = vh.astype(BF16)
      qk = jnp.sum(q[:, ck] * k[:, ck], axis=-1, keepdims=True)
      oa_ref[:, hh * HV:(hh + 1) * HV] = qk * vh

  row_id = lax.broadcasted_iota(jnp.int32, (rows, HV), 0)
  for nl in range(NB):
    n = i * NB + nl
    hot = row_id == n
    one_hot = jnp.where(hot, 1.0, 0.0).astype(BF16)
    for hh in range(HEADS):
      bc = _dot(bc_ref[hh], one_hot)
      decay = bc[0:HK] + bc[HK:2 * HK]
      qe_b = bc[2 * HK:3 * HK]
      kv = _dot(kt_ref[hh], jnp.where(hot, v_ref[hh], jnp.zeros((), BF16)))
      s = s_ref[nl, hh]
      o_inter = jnp.sum(qe_b * s, axis=0, keepdims=True)
      cur = oa_ref[pl.ds(n, 1), hh * HV:(hh + 1) * HV]
      oa_ref[pl.ds(n, 1), hh * HV:(hh + 1) * HV] = cur + o_inter
      s_out_ref[nl, hh] = decay * s + kv

  @pl.when(i == pl.num_programs(0) - 1)
  def _():
    ya = _dot(_head_norm_gate(oa_ref[...], p_ref, gnw_ref).astype(BF16), w_pa_ref[...])
    gvn = _gv_norm(p_ref, lnw_ref, lnb_ref)
    gvn_ref[...] = gvn
    mixed = ws0_ref[...] * gvn + bs0_ref[...]
    u = _gelu(p_ref[:, C_U:C_U + D])
    yb = _dot((u * mixed).astype(BF16), w_pb_ref[...])
    o_ref[...] = _merge_out(x_ref[...], g2_ref[...], ya, yb, p_ref, w_o_ref)


def _mix_sample_call(x, mod, nw, w_in, w_a2, b_a, gnw, w_pa, lnw, lnb, ws0, bs0, w_pb, w_o, state):
  rows = x.shape[0]

  def mod_spec(seg):
    return pl.BlockSpec((rows, D), lambda i: (0, seg), pipeline_mode=pl.Buffered(1))

  in_specs = [
      _const_spec((rows, D)), mod_spec(3), mod_spec(4), mod_spec(5),
      _const_spec((1, D)), _const_spec((D, D_INR)), _const_spec((LANE, DK)), _const_spec((1, DK)),
      _const_spec((1, DV)), _const_spec((DV, D)), _const_spec((1, D)), _const_spec((1, D)),
      _const_spec((1, D)), _const_spec((1, D)), _const_spec((D, D)), _const_spec((D, D)),
      pl.BlockSpec((NB, HEADS, HK, HV), lambda i: (i, 0, 0, 0)),
  ]
  out_specs = [
      pl.BlockSpec((rows, D), lambda i: (0, 0)),
      pl.BlockSpec((NB, HEADS, HK, HV), lambda i: (i, 0, 0, 0)),
      pl.BlockSpec((rows, D), lambda i: (0, 0)),
  ]
  return pl.pallas_call(
      _mix_sample_kernel,
      grid=(rows // NB,),
      in_specs=in_specs,
      out_specs=out_specs,
      out_shape=[jax.ShapeDtypeStruct((rows, D), F32),
                 jax.ShapeDtypeStruct(state.shape, F32),
                 jax.ShapeDtypeStruct((rows, D), F32)],
      scratch_shapes=[
          pltpu.VMEM((rows, D_INR), F32),
          pltpu.VMEM((rows, DV), F32),
          pltpu.VMEM((HEADS, 3 * HK, rows), BF16),
          pltpu.VMEM((HEADS, HK, rows), BF16),
          pltpu.VMEM((HEADS, rows, HV), BF16),
      ],
      compiler_params=pltpu.CompilerParams(
          dimension_semantics=("arbitrary",), vmem_limit_bytes=VMEM_LIMIT),
      name="mix_sample",
  )(x, mod, mod, mod, nw, w_in, w_a2, b_a, gnw, w_pa, lnw, lnb, ws0, bs0, w_pb, w_o, state)


def _reorder_w_in(w_in):
  lo = 2 * DK + 2 * DV
  alr = jnp.pad(w_in[:, lo:lo + LOWRANK], ((0, 0), (0, LANE - LOWRANK)))
  return jnp.concatenate([w_in[:, :lo], w_in[:, lo + LOWRANK:], alr], axis=1).astype(BF16)


def kernel(x_prompt, x_sample, state_gla, c_prompt, c_sample, w_ada, b_ada, norm1_w, ffn1_w13, ffn1_w2, norm2_w, w_in, w_a2, b_a, gla_norm_w, w_pa, gm_ln_w, gm_ln_b, gm_ws, gm_bs, w_pb, w_o, norm3_w, ffn2_w13, ffn2_w2, normf_w):
  batch, seq, _ = x_prompt.shape
  dec = x_sample.shape[0]
  depth = w_ada.shape[0]
  assert depth == 1 and x_sample.shape[1] == 1 and seq % TM_P == 0 and dec % NB == 0

  amat_np, masks_np = _gla_constants()
  amat = jnp.asarray(amat_np, BF16)
  masks = jnp.asarray(masks_np, F32)

  xp = x_prompt.reshape(batch * seq, D)
  xs = x_sample.reshape(dec, D)
  l = 0
  mod_all = _ada_call(jnp.concatenate([c_prompt, c_sample], axis=0), w_ada[l],
                      b_ada[l].reshape(1, N_MOD * D))
  mod_p = mod_all[:batch].reshape(batch, 1, N_MOD * D)
  mod_s = mod_all[batch:]

  row = lambda a: a.reshape(1, -1)
  w13_1, w2_1 = ffn1_w13[l].astype(BF16), ffn1_w2[l].astype(BF16)
  w13_2, w2_2 = ffn2_w13[l].astype(BF16), ffn2_w2[l].astype(BF16)
  w_in_r = _reorder_w_in(w_in[l])
  w_a2_p = jnp.pad(w_a2[l], ((0, LANE - LOWRANK), (0, 0))).astype(BF16)
  w_pa_b, w_pb_b, w_o_b = w_pa[l].astype(BF16), w_pb[l].astype(BF16), w_o[l].astype(BF16)
  gnw = gla_norm_w[l].reshape(1, DV)
  bs_full = jnp.repeat(gm_bs[l].T, GC, axis=1)
  ws0 = jnp.repeat(gm_ws[l][:, 0, 0], GC).reshape(1, D)
  bs0 = bs_full[0:1]
  normf = row(normf_w)

  xp = _ffn_call(xp, mod_p, (0, 1, 2), row(norm1_w[l]), w13_1, w2_1, None,
                 per_row=False, rows_per_batch=seq, tm=TM_P, name="ffn1_prompt")
  xp, s_prompt = _mix_prompt_call(xp, mod_p, row(norm2_w[l]), w_in_r, w_a2_p, row(b_a[l]), amat, masks,
                                  gnw, w_pa_b, row(gm_ln_w[l]), row(gm_ln_b[l]), gm_ws[l], bs_full,
                                  w_pb_b, w_o_b, batch=batch, seq=seq)
  yp = _ffn_call(xp, mod_p, (6, 7, 8), row(norm3_w[l]), w13_2, w2_2, normf,
                 per_row=False, rows_per_batch=seq, tm=TM_P, name="ffn2_prompt")

  xs = _ffn_call(xs, mod_s, (0, 1, 2), row(norm1_w[l]), w13_1, w2_1, None,
                 per_row=True, rows_per_batch=dec, tm=dec, name="ffn1_sample")
  xs, s_sample, gvn = _mix_sample_call(xs, mod_s, row(norm2_w[l]), w_in_r, w_a2_p, row(b_a[l]), gnw,
                                       w_pa_b, row(gm_ln_w[l]), row(gm_ln_b[l]), ws0, bs0,
                                       w_pb_b, w_o_b, state_gla[l])
  ys = _ffn_call(xs, mod_s, (6, 7, 8), row(norm3_w[l]), w13_2, w2_2, normf,
                 per_row=True, rows_per_batch=dec, tm=dec, name="ffn2_sample")

  return (yp.reshape(batch, seq, D), ys.reshape(dec, 1, D), s_prompt[None], s_sample[None],
          gvn.reshape(1, dec, 1, D))
```

```python
import functools

import numpy as np
import jax
import jax.numpy as jnp
from jax import lax
from jax.experimental import pallas as pl
from jax.experimental.pallas import tpu as pltpu

F32 = jnp.float32
BF16 = jnp.bfloat16

D = 1024
HEADS = 4
HK = 128
HV = 256
DK = HEADS * HK
DV = HEADS * HV
LOWRANK = 16
TAU = 16.0
GROUPS = 4
GC = D // GROUPS
GM_CHUNK = 128
F = 2816
N_MOD = 9
EPS = 1e-6

CH = 128
N_LEVELS = 7
LANE = 128

C_Q, C_K, C_ALR, C_U, C_GV, C_V, C_R, C_GA, C_GB = 0, 512, 1024, 1152, 2176, 3200, 4224, 5248, 6272
D_INR = C_GB + D

TM_P = 256
VMEM_LIMIT = 58 * 1024 * 1024


def _gla_constants():
  t = np.arange(CH)
  masks = []
  for lvl in range(N_LEVELS):
    m = 1 << lvl
    upper = t >= (t // (2 * m)) * (2 * m) + m
    same_pair = (t[:, None] // (2 * m)) == (t[None, :] // (2 * m))
    masks.append((upper[:, None] & (~upper)[None, :] & same_pair).astype(np.float32))
  masks.append(np.eye(CH, dtype=np.float32))
  ltri = (t[None, :] <= t[:, None]).astype(np.float32)
  return ltri, np.stack(masks)


def _level_exponents(g, b):
  t = lax.broadcasted_iota(jnp.int32, g.shape, 0)
  g_prev = pltpu.roll(g, 1, 0)
  g_next = pltpu.roll(g, CH - 1, 0)
  r4 = t & 3
  levels = [
      jnp.where((t & 1) == 1, g, 0.0),
      jnp.where(r4 == 0, g_next, jnp.where(r4 == 1, 0.0, jnp.where(r4 == 2, g, g + g_prev))),
  ]
  groups = CH // 8
  row = lambda i: jnp.broadcast_to(b[i:i + 1, :], (8, b.shape[1]))
  b3 = [row(8 * j + 3) for j in range(groups)]
  b7 = [row(8 * j + 7) for j in range(groups)]
  levels.append(-jnp.abs(b - jnp.concatenate(b3, axis=0)))
  for lvl in range(3, N_LEVELS):
    m = 1 << lvl
    ref = [b7[((8 * j) // (2 * m)) * (2 * m) // 8 + m // 8 - 1] for j in range(groups)]
    levels.append(-jnp.abs(b - jnp.concatenate(ref, axis=0)))
  to_end = -jnp.abs(b - jnp.concatenate([b7[groups - 1]] * groups, axis=0))
  return levels, to_end


def _dot(a, b):
  return jnp.dot(a, b, preferred_element_type=F32)


def _dot_nt(a, b):
  return lax.dot_general(a, b, (((1,), (1,)), ((), ())), preferred_element_type=F32)


def _rms(x, w):
  ms = jnp.mean(x * x, axis=-1, keepdims=True)
  return x * lax.rsqrt(ms + EPS) * w


def _gelu(x):
  return 0.5 * x * (1.0 + lax.erf(x * (2.0 ** -0.5)))


def _log_sigmoid(z):
  return jnp.minimum(z, 0.0) - jnp.log1p(jnp.exp(-jnp.abs(z)))


def _const_spec(shape):
  nd = len(shape)
  return pl.BlockSpec(shape, lambda *_: (0,) * nd, pipeline_mode=pl.Buffered(1))


def _ada_kernel(c_ref, w_ref, b_ref, o_ref):
  c = c_ref[...]
  s = (c * jax.nn.sigmoid(c)).astype(BF16)
  o_ref[...] = _dot(s, w_ref[...].astype(BF16)) + b_ref[...]


def _ada_call(c_all, w_ada, b_ada):
  rows = c_all.shape[0]
  tn = 1024
  return pl.pallas_call(
      _ada_kernel,
      grid=(N_MOD * D // tn,),
      in_specs=[
          pl.BlockSpec((rows, D), lambda i: (0, 0)),
          pl.BlockSpec((D, tn), lambda i: (0, i)),
          pl.BlockSpec((1, tn), lambda i: (0, i)),
      ],
      out_specs=pl.BlockSpec((rows, tn), lambda i: (0, i)),
      out_shape=jax.ShapeDtypeStruct((rows, N_MOD * D), F32),
      compiler_params=pltpu.CompilerParams(dimension_semantics=("arbitrary",)),
      name="ada",
  )(c_all, w_ada, b_ada)


def _ffn_kernel(x_ref, sh_ref, sc_ref, g_ref, nw_ref, w13_ref, w2_ref, *rest, final_norm):
  o_ref = rest[-1]
  x = x_ref[...]
  h = (_rms(x, nw_ref[...]) * (1.0 + sc_ref[...]) + sh_ref[...]).astype(BF16)
  a = _dot(h, w13_ref[:, :F])
  b = _dot(h, w13_ref[:, F:])
  p = (a * jax.nn.sigmoid(a) * b).astype(BF16)
  out = x + 0.5 * g_ref[...] * _dot(p, w2_ref[...])
  if final_norm:
    out = _rms(out, rest[0][...])
  o_ref[...] = out


def _mod_spec(per_row, rows_per_batch, tm, seg):
  if per_row:
    return pl.BlockSpec((tm, D), lambda i: (i, seg))
  tiles = rows_per_batch // tm
  return pl.BlockSpec((None, 1, D), lambda i: (i // tiles, 0, seg))


def _ffn_call(x, mod, segs, nw, w13, w2, nf, *, per_row, rows_per_batch, tm, name):
  rows = x.shape[0]
  in_specs = [pl.BlockSpec((tm, D), lambda i: (i, 0))]
  in_specs += [_mod_spec(per_row, rows_per_batch, tm, s) for s in segs]
  in_specs += [_const_spec((1, D)), _const_spec((D, 2 * F)), _const_spec((F, D))]
  args = [x, mod, mod, mod, nw, w13, w2]
  if nf is not None:
    in_specs.append(_const_spec((1, D)))
    args.append(nf)
  return pl.pallas_call(
      functools.partial(_ffn_kernel, final_norm=nf is not None),
      grid=(rows // tm,),
      in_specs=in_specs,
      out_specs=pl.BlockSpec((tm, D), lambda i: (i, 0)),
      out_shape=jax.ShapeDtypeStruct((rows, D), F32),
      compiler_params=pltpu.CompilerParams(
          dimension_semantics=("arbitrary",), vmem_limit_bytes=VMEM_LIMIT),
      name=name,
  )(*args)


def _project(x, sh, sc, nw_ref, w_in_ref, w_a2_ref, b_a_ref, p_ref):
  h = (_rms(x, nw_ref[...]) * (1.0 + sc) + sh).astype(BF16)
  p_ref[...] = _dot(h, w_in_ref[...])
  a_lr = p_ref[:, C_ALR:C_ALR + LANE].astype(BF16)
  z = _dot(a_lr, w_a2_ref[...]) + b_a_ref[...]
  return _log_sigmoid(z) * (1.0 / TAU)


def _head_norm_gate(o, p_ref, gnw_ref):
  parts = []
  for hh in range(HEADS):
    sl = slice(hh * HV, (hh + 1) * HV)
    parts.append(_rms(o[:, sl], gnw_ref[:, sl]))
  r = p_ref[:, C_R:C_R + DV]
  return jnp.concatenate(parts, axis=-1) * (r * jax.nn.sigmoid(r))


def _gv_norm(p_ref, lnw_ref, lnb_ref):
  gv = _gelu(p_ref[:, C_GV:C_GV + D])
  mu = jnp.mean(gv, axis=-1, keepdims=True)
  d = gv - mu
  var = jnp.mean(d * d, axis=-1, keepdims=True)
  return d * lax.rsqrt(var + EPS) * lnw_ref[...] + lnb_ref[...]


def _merge_out(x, g2, ya, yb, p_ref, w_o_ref):
  ga = p_ref[:, C_GA:C_GA + D]
  gb = p_ref[:, C_GB:C_GB + D]
  merged = (jax.nn.sigmoid(ga) * ya + jax.nn.sigmoid(gb) * yb).astype(BF16)
  return x + g2 * _dot(merged, w_o_ref[...])


def _mix_prompt_kernel(x_ref, sh_ref, sc_ref, g2_ref, nw_ref, w_in_ref, w_a2_ref, b_a_ref,
                       ltri_ref, mask_ref, gnw_ref, w_pa_ref, lnw_ref, lnb_ref, ws_ref, bs_ref,
                       w_pb_ref, w_o_ref, o_ref, s_out_ref, st_ref, p_ref, oa_ref, mx_ref, qk_ref):
  j = pl.program_id(1)
  tm = x_ref.shape[0]

  @pl.when(j == 0)
  def _():
    st_ref[...] = jnp.zeros_like(st_ref)

  x = x_ref[...]
  h = (_rms(x, nw_ref[...]) * (1.0 + sc_ref[...]) + sh_ref[...]).astype(BF16)

  def project(lo, hi):
    p_ref[:, lo:hi] = _dot(h, w_in_ref[:, lo:hi])

  project(C_Q, C_U)
  project(C_U, C_V)
  a_lr = p_ref[:, C_ALR:C_ALR + LANE].astype(BF16)
  z = _dot(a_lr, w_a2_ref[...]) + b_a_ref[...]
  project(C_V, C_R)
  g = _log_sigmoid(z) * (1.0 / TAU)

  ltri = ltri_ref[...]
  cums = []
  for c in range(tm // CH):
    gc = g[c * CH:(c + 1) * CH, :]
    g_hi = gc.astype(BF16)
    g_lo = (gc - g_hi.astype(F32)).astype(BF16)
    cums.append(_dot(ltri, g_hi) + _dot(ltri, g_lo))
  project(C_R, D_INR)

  decays = []
  for c in range(tm // CH):
    rows = slice(c * CH, (c + 1) * CH)
    b = cums[c]
    e_lvl, e_end = _level_exponents(g[rows, :], b)
    q = p_ref[rows, C_Q:C_Q + DK] * (HK ** -0.5)
    k = p_ref[rows, C_K:C_K + DK]
    for lvl in range(N_LEVELS):
      e = jnp.exp(e_lvl[lvl])
      qk_ref[2 * lvl, rows, :] = (q * e).astype(BF16)
      qk_ref[2 * lvl + 1, rows, :] = (k * e).astype(BF16)
    qk_ref[2 * N_LEVELS, rows, :] = q.astype(BF16)
    qk_ref[2 * N_LEVELS + 1, rows, :] = k.astype(BF16)
    qk_ref[2 * N_LEVELS + 2, rows, :] = (q * jnp.exp(b)).astype(BF16)
    qk_ref[2 * N_LEVELS + 3, rows, :] = (k * jnp.exp(e_end)).astype(BF16)
    decays.append(jnp.exp(b[CH - 1:CH, :]))

  gvn = _gv_norm(p_ref, lnw_ref, lnb_ref).astype(BF16)
  u = _gelu(p_ref[:, C_U:C_U + D])

  def level_scores(c, hh):
    rows = slice(c * CH, (c + 1) * CH)
    ck = slice(hh * HK, (hh + 1) * HK)
    scores = None
    for lvl in range(N_LEVELS + 1):
      part = mask_ref[lvl] * _dot_nt(qk_ref[2 * lvl, rows, ck], qk_ref[2 * lvl + 1, rows, ck])
      scores = part if scores is None else scores + part
    return scores.astype(BF16)

  def finish(c, hh, scores):
    rows = slice(c * CH, (c + 1) * CH)
    ck = slice(hh * HK, (hh + 1) * HK)
    v32 = p_ref[rows, C_V + hh * HV:C_V + (hh + 1) * HV]
    st = st_ref[hh]
    oa_ref[rows, hh * HV:(hh + 1) * HV] = (
        _dot(scores, v32.astype(BF16))
        + _dot_nt(qk_ref[2 * N_LEVELS + 2, rows, ck], st.astype(BF16)))
    st_ref[hh] = st * decays[c][:, ck] + _dot(v32.T.astype(BF16), qk_ref[2 * N_LEVELS + 3, rows, ck])

  pending = None
  for c in range(tm // CH):
    for hh in range(HEADS):
      scores = level_scores(c, hh)
      if pending is not None:
        finish(*pending)
      pending = (c, hh, scores)

  ti = lax.broadcasted_iota(jnp.int32, (GM_CHUNK, GM_CHUNK), 0)
  si = lax.broadcasted_iota(jnp.int32, (GM_CHUNK, GM_CHUNK), 1)
  for gg in range(GROUPS):
    ws = jnp.where(si <= ti, ws_ref[gg], 0.0).astype(BF16)
    cols = slice(gg * GC, (gg + 1) * GC)
    for c in range(tm // GM_CHUNK):
      rows = slice(c * GM_CHUNK, (c + 1) * GM_CHUNK)
      mx_ref[rows, cols] = _dot(ws, gvn[rows, cols]) + bs_ref[:, cols]
  finish(*pending)
  yb = _dot((u * mx_ref[...]).astype(BF16), w_pb_ref[...])
  ya = _dot(_head_norm_gate(oa_ref[...], p_ref, gnw_ref).astype(BF16), w_pa_ref[...])

  o_ref[...] = _merge_out(x, g2_ref[...], ya, yb, p_ref, w_o_ref)

  @pl.when(j == pl.num_programs(1) - 1)
  def _():
    for hh in range(HEADS):
      s_out_ref[hh] = st_ref[hh].T


def _mix_prompt_call(x, mod3, nw, w_in, w_a2, b_a, ltri, masks, gnw, w_pa, lnw, lnb, ws, bs_full,
                     w_pb, w_o, *, batch, seq):
  tm = TM_P
  tiles = seq // tm
  row_spec = pl.BlockSpec((tm, D), lambda b, j: (b * tiles + j, 0))

  def mod_spec(seg):
    return pl.BlockSpec((None, 1, D), lambda b, j: (b, 0, seg))

  in_specs = [
      row_spec, mod_spec(3), mod_spec(4), mod_spec(5),
      _const_spec((1, D)), _const_spec((D, D_INR)), _const_spec((LANE, DK)), _const_spec((1, DK)),
      _const_spec(ltri.shape), _const_spec(masks.shape), _const_spec((1, DV)),
      _const_spec((DV, D)), _const_spec((1, D)), _const_spec((1, D)),
      _const_spec((GROUPS, GM_CHUNK, GM_CHUNK)), _const_spec((GM_CHUNK, D)),
      _const_spec((D, D)), _const_spec((D, D)),
  ]
  out_specs = [
      row_spec,
      pl.BlockSpec((None, HEADS, HK, HV), lambda b, j: (b, 0, 0, 0)),
  ]
  return pl.pallas_call(
      _mix_prompt_kernel,
      grid=(batch, tiles),
      in_specs=in_specs,
      out_specs=out_specs,
      out_shape=[jax.ShapeDtypeStruct((batch * seq, D), F32),
                 jax.ShapeDtypeStruct((batch, HEADS, HK, HV), F32)],
      scratch_shapes=[
          pltpu.VMEM((HEADS, HV, HK), F32),
          pltpu.VMEM((tm, D_INR), F32),
          pltpu.VMEM((tm, DV), F32),
          pltpu.VMEM((tm, D), F32),
          pltpu.VMEM((2 * N_LEVELS + 4, tm, DK), BF16),
      ],
      compiler_params=pltpu.CompilerParams(
          dimension_semantics=("arbitrary", "arbitrary"), vmem_limit_bytes=VMEM_LIMIT),
      name="mix_prompt",
  )(x, mod3, mod3, mod3, nw, w_in, w_a2, b_a, ltri, masks, gnw, w_pa, lnw, lnb, ws, bs_full,
    w_pb, w_o)


NB = 8


def _mix_sample_kernel(x_ref, sh_ref, sc_ref, g2_ref, nw_ref, w_in_ref, w_a2_ref, b_a_ref,
                       gnw_ref, w_pa_ref, lnw_ref, lnb_ref, ws0_ref, bs0_ref, w_pb_ref, w_o_ref,
                       s_ref, o_ref, s_out_ref, gvn_ref,
                       p_ref, oa_ref, bc_ref, kt_ref, v_ref):
  i = pl.program_id(0)
  rows = x_ref.shape[0]

  @pl.when(i == 0)
  def _():
    g = _project(x_ref[...], sh_ref[...], sc_ref[...], nw_ref, w_in_ref, w_a2_ref, b_a_ref, p_ref)
    q = p_ref[:, C_Q:C_Q + DK] * (HK ** -0.5)
    k = p_ref[:, C_K:C_K + DK]
    eb = jnp.exp(g)
    eb_hi = eb.astype(BF16)
    eb_lo = (eb - eb_hi.astype(F32)).astype(BF16)
    qe = q * eb
    for hh in range(HEADS):
      ck = slice(hh * HK, (hh + 1) * HK)
      bc_ref[hh, 0:HK, :] = eb_hi[:, ck].astype(F32).T.astype(BF16)
      bc_ref[hh, HK:2 * HK, :] = eb_lo[:, ck].astype(F32).T.astype(BF16)
      bc_ref[hh, 2 * HK:3 * HK, :] = qe[:, ck].T.astype(BF16)
      kt_ref[hh] = k[:, ck].T.astype(BF16)
      vh = p_ref[:, C_V + hh * HV:C_V + (hh + 1) * HV]
      v_ref[hh] = vh.astype(BF16)
      qk = jnp.sum(q[:, ck] * k[:, ck], axis=-1, keepdims=True)
      oa_ref[:, hh * HV:(hh + 1) * HV] = qk * vh

  row_id = lax.broadcasted_iota(jnp.int32, (rows, HV), 0)
  for nl in range(NB):
    n = i * NB + nl
    hot = row_id == n
    one_hot = jnp.where(hot, 1.0, 0.0).astype(BF16)
    for hh in range(HEADS):
      bc = _dot(bc_ref[hh], one_hot)
      decay = bc[0:HK] + bc[HK:2 * HK]
      qe_b = bc[2 * HK:3 * HK]
      kv = _dot(kt_ref[hh], jnp.where(hot, v_ref[hh], jnp.zeros((), BF16)))
      s = s_ref[nl, hh]
      o_inter = jnp.sum(qe_b * s, axis=0, keepdims=True)
      cur = oa_ref[pl.ds(n, 1), hh * HV:(hh + 1) * HV]
      oa_ref[pl.ds(n, 1), hh * HV:(hh + 1) * HV] = cur + o_inter
      s_out_ref[nl, hh] = decay * s + kv

  @pl.when(i == pl.num_programs(0) - 1)
  def _():
    ya = _dot(_head_norm_gate(oa_ref[...], p_ref, gnw_ref).astype(BF16), w_pa_ref[...])
    gvn = _gv_norm(p_ref, lnw_ref, lnb_ref)
    gvn_ref[...] = gvn
    mixed = ws0_ref[...] * gvn + bs0_ref[...]
    u = _gelu(p_ref[:, C_U:C_U + D])
    yb = _dot((u * mixed).astype(BF16), w_pb_ref[...])
    o_ref[...] = _merge_out(x_ref[...], g2_ref[...], ya, yb, p_ref, w_o_ref)


def _mix_sample_call(x, mod, nw, w_in, w_a2, b_a, gnw, w_pa, lnw, lnb, ws0, bs0, w_pb, w_o, state):
  rows = x.shape[0]

  def mod_spec(seg):
    return pl.BlockSpec((rows, D), lambda i: (0, seg), pipeline_mode=pl.Buffered(1))

  in_specs = [
      _const_spec((rows, D)), mod_spec(3), mod_spec(4), mod_spec(5),
      _const_spec((1, D)), _const_spec((D, D_INR)), _const_spec((LANE, DK)), _const_spec((1, DK)),
      _const_spec((1, DV)), _const_spec((DV, D)), _const_spec((1, D)), _const_spec((1, D)),
      _const_spec((1, D)), _const_spec((1, D)), _const_spec((D, D)), _const_spec((D, D)),
      pl.BlockSpec((NB, HEADS, HK, HV), lambda i: (i, 0, 0, 0)),
  ]
  out_specs = [
      pl.BlockSpec((rows, D), lambda i: (0, 0)),
      pl.BlockSpec((NB, HEADS, HK, HV), lambda i: (i, 0, 0, 0)),
      pl.BlockSpec((rows, D), lambda i: (0, 0)),
  ]
  return pl.pallas_call(
      _mix_sample_kernel,
      grid=(rows // NB,),
      in_specs=in_specs,
      out_specs=out_specs,
      out_shape=[jax.ShapeDtypeStruct((rows, D), F32),
                 jax.ShapeDtypeStruct(state.shape, F32),
                 jax.ShapeDtypeStruct((rows, D), F32)],
      scratch_shapes=[
          pltpu.VMEM((rows, D_INR), F32),
          pltpu.VMEM((rows, DV), F32),
          pltpu.VMEM((HEADS, 3 * HK, rows), BF16),
          pltpu.VMEM((HEADS, HK, rows), BF16),
          pltpu.VMEM((HEADS, rows, HV), BF16),
      ],
      compiler_params=pltpu.CompilerParams(
          dimension_semantics=("arbitrary",), vmem_limit_bytes=VMEM_LIMIT),
      name="mix_sample",
  )(x, mod, mod, mod, nw, w_in, w_a2, b_a, gnw, w_pa, lnw, lnb, ws0, bs0, w_pb, w_o, state)


def _reorder_w_in(w_in):
  lo = 2 * DK + 2 * DV
  hi = lo + LOWRANK
  alr = jnp.pad(w_in[:, lo:hi], ((0, 0), (0, LANE - LOWRANK)))
  return jnp.concatenate([w_in[:, :2 * DK], alr, w_in[:, hi:hi + 2 * D], w_in[:, 2 * DK:lo],
                          w_in[:, hi + 2 * D:]], axis=1).astype(BF16)


def kernel(x_prompt, x_sample, state_gla, c_prompt, c_sample, w_ada, b_ada, norm1_w, ffn1_w13, ffn1_w2, norm2_w, w_in, w_a2, b_a, gla_norm_w, w_pa, gm_ln_w, gm_ln_b, gm_ws, gm_bs, w_pb, w_o, norm3_w, ffn2_w13, ffn2_w2, normf_w):
  batch, seq, _ = x_prompt.shape
  dec = x_sample.shape[0]
  depth = w_ada.shape[0]
  assert depth == 1 and x_sample.shape[1] == 1 and seq % TM_P == 0 and dec % NB == 0

  ltri_np, masks_np = _gla_constants()
  ltri = jnp.asarray(ltri_np, BF16)
  masks = jnp.asarray(masks_np, F32)

  xp = x_prompt.reshape(batch * seq, D)
  xs = x_sample.reshape(dec, D)
  l = 0
  mod_all = _ada_call(jnp.concatenate([c_prompt, c_sample], axis=0), w_ada[l],
                      b_ada[l].reshape(1, N_MOD * D))
  mod_p = mod_all[:batch].reshape(batch, 1, N_MOD * D)
  mod_s = mod_all[batch:]

  row = lambda a: a.reshape(1, -1)
  w13_1, w2_1 = ffn1_w13[l].astype(BF16), ffn1_w2[l].astype(BF16)
  w13_2, w2_2 = ffn2_w13[l].astype(BF16), ffn2_w2[l].astype(BF16)
  w_in_r = _reorder_w_in(w_in[l])
  w_a2_p = jnp.pad(w_a2[l], ((0, LANE - LOWRANK), (0, 0))).astype(BF16)
  w_pa_b, w_pb_b, w_o_b = w_pa[l].astype(BF16), w_pb[l].astype(BF16), w_o[l].astype(BF16)
  gnw = gla_norm_w[l].reshape(1, DV)
  bs_full = jnp.repeat(gm_bs[l].T, GC, axis=1)
  ws0 = jnp.repeat(gm_ws[l][:, 0, 0], GC).reshape(1, D)
  bs0 = bs_full[0:1]
  normf = row(normf_w)

  xp = _ffn_call(xp, mod_p, (0, 1, 2), row(norm1_w[l]), w13_1, w2_1, None,
                 per_row=False, rows_per_batch=seq, tm=TM_P, name="ffn1_prompt")
  xp, s_prompt = _mix_prompt_call(xp, mod_p, row(norm2_w[l]), w_in_r, w_a2_p, row(b_a[l]), ltri, masks,
                                  gnw, w_pa_b, row(gm_ln_w[l]), row(gm_ln_b[l]), gm_ws[l], bs_full,
                                  w_pb_b, w_o_b, batch=batch, seq=seq)
  yp = _ffn_call(xp, mod_p, (6, 7, 8), row(norm3_w[l]), w13_2, w2_2, normf,
                 per_row=False, rows_per_batch=seq, tm=TM_P, name="ffn2_prompt")

  xs = _ffn_call(xs, mod_s, (0, 1, 2), row(norm1_w[l]), w13_1, w2_1, None,
                 per_row=True, rows_per_batch=dec, tm=dec, name="ffn1_sample")
  xs, s_sample, gvn = _mix_sample_call(xs, mod_s, row(norm2_w[l]), w_in_r, w_a2_p, row(b_a[l]), gnw,
                                       w_pa_b, row(gm_ln_w[l]), row(gm_ln_b[l]), ws0, bs0,
                                       w_pb_b, w_o_b, state_gla[l])
  ys = _ffn_call(xs, mod_s, (6, 7, 8), row(norm3_w[l]), w13_2, w2_2, normf,
                 per_row=True, rows_per_batch=dec, tm=dec, name="ffn2_sample")

  return (yp.reshape(batch, seq, D), ys.reshape(dec, 1, D), s_prompt[None], s_sample[None],
          gvn.reshape(1, dec, 1, D))
```

```python
import functools

import numpy as np
import jax
import jax.numpy as jnp
from jax import lax
from jax.experimental import pallas as pl
from jax.experimental.pallas import tpu as pltpu

F32 = jnp.float32
BF16 = jnp.bfloat16

D = 1024
HEADS = 4
HK = 128
HV = 256
DK = HEADS * HK
DV = HEADS * HV
LOWRANK = 16
TAU = 16.0
GROUPS = 4
GC = D // GROUPS
GM_CHUNK = 128
F = 2816
N_MOD = 9
EPS = 1e-6

CH = 128
N_LEVELS = 7
LANE = 128

C_Q, C_K, C_ALR, C_U, C_GV, C_V, C_R, C_GA, C_GB = 0, 512, 1024, 1152, 2176, 3200, 4224, 5248, 6272
D_P = C_GB + D

PROJ_QK = (0, 0, 2 * DK, C_Q)
PROJ_ALR = (1, 0, LANE, C_ALR)
PROJ_UGV = (2, 0, 2 * D, C_U)
PROJ_V = (0, 2 * DK, 2 * DK + DV, C_V)
PROJ_R = (0, 2 * DK + DV, 2 * DK + 2 * DV, C_R)
PROJ_GATES = (2, 2 * D, 4 * D, C_GA)
PROJ_ALL = (PROJ_QK, PROJ_ALR, PROJ_UGV, PROJ_V, PROJ_R, PROJ_GATES)

TM_P = 256
TM_FFN = 512
NB = 8
VMEM_LIMIT = 58 * 1024 * 1024


def _gla_constants():
  t = np.arange(CH)
  masks = []
  for lvl in range(N_LEVELS):
    m = 1 << lvl
    upper = t >= (t // (2 * m)) * (2 * m) + m
    same_pair = (t[:, None] // (2 * m)) == (t[None, :] // (2 * m))
    masks.append((upper[:, None] & (~upper)[None, :] & same_pair).astype(np.float32))
  masks.append(np.eye(CH, dtype=np.float32))
  ltri = (t[None, :] <= t[:, None]).astype(np.float32)
  return ltri, np.stack(masks)


def _level_exponents(g, b):
  t = lax.broadcasted_iota(jnp.int32, g.shape, 0)
  g_prev = pltpu.roll(g, 1, 0)
  g_next = pltpu.roll(g, CH - 1, 0)
  r4 = t & 3
  levels = [
      jnp.where((t & 1) == 1, g, 0.0),
      jnp.where(r4 == 0, g_next, jnp.where(r4 == 1, 0.0, jnp.where(r4 == 2, g, g + g_prev))),
  ]
  groups = CH // 8
  row = lambda i: jnp.broadcast_to(b[i:i + 1, :], (8, b.shape[1]))
  b3 = [row(8 * j + 3) for j in range(groups)]
  b7 = [row(8 * j + 7) for j in range(groups)]
  levels.append(-jnp.abs(b - jnp.concatenate(b3, axis=0)))
  for lvl in range(3, N_LEVELS):
    m = 1 << lvl
    ref = [b7[((8 * j) // (2 * m)) * (2 * m) // 8 + m // 8 - 1] for j in range(groups)]
    levels.append(-jnp.abs(b - jnp.concatenate(ref, axis=0)))
  to_end = -jnp.abs(b - jnp.concatenate([b7[groups - 1]] * groups, axis=0))
  return levels, to_end


def _dot(a, b):
  return jnp.dot(a, b, preferred_element_type=F32)


def _dot_nt(a, b):
  return lax.dot_general(a, b, (((1,), (1,)), ((), ())), preferred_element_type=F32)


def _rms(x, w):
  ms = jnp.mean(x * x, axis=-1, keepdims=True)
  return x * lax.rsqrt(ms + EPS) * w


def _norm_mod(x, nw, sc, sh):
  return (_rms(x, nw) * (1.0 + sc) + sh).astype(BF16)


def _gelu(x):
  return 0.5 * x * (1.0 + lax.erf(x * (2.0 ** -0.5)))


def _log_sigmoid(z):
  return jnp.minimum(z, 0.0) - jnp.log1p(jnp.exp(-jnp.abs(z)))


def _const_spec(shape):
  nd = len(shape)
  return pl.BlockSpec(shape, lambda *_: (0,) * nd, pipeline_mode=pl.Buffered(1))


def _ada_kernel(c_ref, w_ref, b_ref, o_ref):
  c = c_ref[...]
  s = (c * jax.nn.sigmoid(c)).astype(BF16)
  o_ref[...] = _dot(s, w_ref[...].astype(BF16)) + b_ref[...]


def _ada_call(c_all, w_ada, b_ada):
  rows = c_all.shape[0]
  tn = 1024
  return pl.pallas_call(
      _ada_kernel,
      grid=(N_MOD * D // tn,),
      in_specs=[
          pl.BlockSpec((rows, D), lambda i: (0, 0)),
          pl.BlockSpec((D, tn), lambda i: (0, i)),
          pl.BlockSpec((1, tn), lambda i: (0, i)),
      ],
      out_specs=pl.BlockSpec((rows, tn), lambda i: (0, i)),
      out_shape=jax.ShapeDtypeStruct((rows, N_MOD * D), F32),
      compiler_params=pltpu.CompilerParams(dimension_semantics=("arbitrary",)),
      name="ada",
  )(c_all, w_ada, b_ada)


def _ffn_kernel(x_ref, sh_ref, sc_ref, g_ref, nw_ref, w13_ref, w2_ref, *rest, final_norm):
  o_ref = rest[-1]
  x = x_ref[...]
  h = _norm_mod(x, nw_ref[...], sc_ref[...], sh_ref[...])
  a = _dot(h, w13_ref[:, :F])
  b = _dot(h, w13_ref[:, F:])
  p = (a * jax.nn.sigmoid(a) * b).astype(BF16)
  out = x + 0.5 * g_ref[...] * _dot(p, w2_ref[...])
  if final_norm:
    out = _rms(out, rest[0][...])
  o_ref[...] = out


def _mod_spec(per_row, rows_per_batch, tm, seg):
  if per_row:
    return pl.BlockSpec((tm, D), lambda i: (i, seg))
  tiles = rows_per_batch // tm
  return pl.BlockSpec((None, 1, D), lambda i: (i // tiles, 0, seg))


def _ffn_call(x, mod, segs, nw, w13, w2, nf, *, per_row, rows_per_batch, tm, name):
  rows = x.shape[0]
  in_specs = [pl.BlockSpec((tm, D), lambda i: (i, 0))]
  in_specs += [_mod_spec(per_row, rows_per_batch, tm, s) for s in segs]
  in_specs += [_const_spec((1, D)), _const_spec((D, 2 * F)), _const_spec((F, D))]
  args = [x, mod, mod, mod, nw, w13, w2]
  if nf is not None:
    in_specs.append(_const_spec((1, D)))
    args.append(nf)
  return pl.pallas_call(
      functools.partial(_ffn_kernel, final_norm=nf is not None),
      grid=(rows // tm,),
      in_specs=in_specs,
      out_specs=pl.BlockSpec((tm, D), lambda i: (i, 0)),
      out_shape=jax.ShapeDtypeStruct((rows, D), F32),
      compiler_params=pltpu.CompilerParams(
          dimension_semantics=("arbitrary",), vmem_limit_bytes=VMEM_LIMIT),
      name=name,
  )(*args)


def _project_piece(h, w_refs, p_ref, piece):
  idx, lo, hi, dst = piece
  p_ref[:, dst:dst + hi - lo] = _dot(h, w_refs[idx][:, lo:hi])


def _log_decay(p_ref, w_a2_ref, b_a_ref):
  a_lr = p_ref[:, C_ALR:C_ALR + LANE].astype(BF16)
  z = _dot(a_lr, w_a2_ref[...]) + b_a_ref[...]
  return _log_sigmoid(z) * (1.0 / TAU)


def _head_norm_gate(o, p_ref, gnw_ref):
  parts = []
  for hh in range(HEADS):
    sl = slice(hh * HV, (hh + 1) * HV)
    parts.append(_rms(o[:, sl], gnw_ref[:, sl]))
  r = p_ref[:, C_R:C_R + DV]
  return jnp.concatenate(parts, axis=-1) * (r * jax.nn.sigmoid(r))


def _gv_norm(p_ref, lnw_ref, lnb_ref):
  gv = _gelu(p_ref[:, C_GV:C_GV + D])
  mu = jnp.mean(gv, axis=-1, keepdims=True)
  d = gv - mu
  var = jnp.mean(d * d, axis=-1, keepdims=True)
  return d * lax.rsqrt(var + EPS) * lnw_ref[...] + lnb_ref[...]


def _merge_out(x, g2, ya, yb, p_ref, w_o_ref):
  ga = p_ref[:, C_GA:C_GA + D]
  gb = p_ref[:, C_GB:C_GB + D]
  merged = (jax.nn.sigmoid(ga) * ya + jax.nn.sigmoid(gb) * yb).astype(BF16)
  return x + g2 * _dot(merged, w_o_ref[...])


def _mix_prompt_kernel(x_ref, xn_ref, sh_ref, sc_ref, shn_ref, scn_ref, g2_ref, nw_ref,
                       wa_ref, walr_ref, wb_ref, w_a2_ref, b_a_ref,
                       ltri_ref, mask_ref, gnw_ref, w_pa_ref, lnw_ref, lnb_ref, ws_ref, bs_ref,
                       w_pb_ref, w_o_ref, o_ref, s_out_ref,
                       st_ref, pa_ref, pb_ref, oa_ref, mx_ref, qk_ref, *, tiles_per_batch):
  s = pl.program_id(0)
  tm = TM_P
  w_refs = (wa_ref, walr_ref, wb_ref)
  nw = nw_ref[...]

  @pl.when(s == 0)
  def _():
    h0 = _norm_mod(x_ref[0:tm, :], nw, sc_ref[...], sh_ref[...])
    for piece in PROJ_ALL:
      _project_piece(h0, w_refs, pa_ref, piece)

  @pl.when((2 * s) % tiles_per_batch == 0)
  def _():
    st_ref[...] = jnp.zeros_like(st_ref)

  ltri = ltri_ref[...]
  ti = lax.broadcasted_iota(jnp.int32, (GM_CHUNK, GM_CHUNK), 0)
  si = lax.broadcasted_iota(jnp.int32, (GM_CHUNK, GM_CHUNK), 1)

  def mix_stages(p_ref, half):
    rows_out = slice(half * tm, (half + 1) * tm)
    oa = oa_ref.at[half]
    mx = mx_ref.at[half]
    env = {}

    def gate():
      env["g"] = _log_decay(p_ref, w_a2_ref, b_a_ref)

    def cumsum():
      cums = []
      for c in range(tm // CH):
        gc = env["g"][c * CH:(c + 1) * CH, :]
        g_hi = gc.astype(BF16)
        g_lo = (gc - g_hi.astype(F32)).astype(BF16)
        cums.append(_dot(ltri, g_hi) + _dot(ltri, g_lo))
      env["cums"] = cums

    def prep():
      decays = []
      for c in range(tm // CH):
        rows = slice(c * CH, (c + 1) * CH)
        b = env["cums"][c]
        e_lvl, e_end = _level_exponents(env["g"][rows, :], b)
        q = p_ref[rows, C_Q:C_Q + DK] * (HK ** -0.5)
        k = p_ref[rows, C_K:C_K + DK]
        for lvl in range(N_LEVELS):
          e = jnp.exp(e_lvl[lvl])
          qk_ref[2 * lvl, rows, :] = (q * e).astype(BF16)
          qk_ref[2 * lvl + 1, rows, :] = (k * e).astype(BF16)
        qk_ref[2 * N_LEVELS, rows, :] = q.astype(BF16)
        qk_ref[2 * N_LEVELS + 1, rows, :] = k.astype(BF16)
        qk_ref[2 * N_LEVELS + 2, rows, :] = (q * jnp.exp(b)).astype(BF16)
        qk_ref[2 * N_LEVELS + 3, rows, :] = (k * jnp.exp(e_end)).astype(BF16)
        decays.append(jnp.exp(b[CH - 1:CH, :]))
      env["decays"] = decays
      env["gvn"] = _gv_norm(p_ref, lnw_ref, lnb_ref).astype(BF16)
      env["u"] = _gelu(p_ref[:, C_U:C_U + D])
      env["pending"] = None

    def level_scores(c, hh):
      rows = slice(c * CH, (c + 1) * CH)
      ck = slice(hh * HK, (hh + 1) * HK)
      scores = None
      for lvl in range(N_LEVELS + 1):
        part = mask_ref[lvl] * _dot_nt(qk_ref[2 * lvl, rows, ck], qk_ref[2 * lvl + 1, rows, ck])
        scores = part if scores is None else scores + part
      return scores.astype(BF16)

    def finish(c, hh, scores):
      rows = slice(c * CH, (c + 1) * CH)
      ck = slice(hh * HK, (hh + 1) * HK)
      v32 = p_ref[rows, C_V + hh * HV:C_V + (hh + 1) * HV]
      st = st_ref[hh]
      oa[rows, hh * HV:(hh + 1) * HV] = (
          _dot(scores, v32.astype(BF16))
          + _dot_nt(qk_ref[2 * N_LEVELS + 2, rows, ck], st.astype(BF16)))
      st_ref[hh] = (st * env["decays"][c][:, ck]
                    + _dot(v32.T.astype(BF16), qk_ref[2 * N_LEVELS + 3, rows, ck]))

    def gla(c):
      def run():
        for hh in range(HEADS):
          scores = level_scores(c, hh)
          if env["pending"] is not None:
            finish(*env["pending"])
          env["pending"] = (c, hh, scores)
      return run

    def gmlp():
      for gg in range(GROUPS):
        ws = jnp.where(si <= ti, ws_ref[gg], 0.0).astype(BF16)
        cols = slice(gg * GC, (gg + 1) * GC)
        for c in range(tm // GM_CHUNK):
          rows = slice(c * GM_CHUNK, (c + 1) * GM_CHUNK)
          mx[rows, cols] = _dot(ws, env["gvn"][rows, cols]) + bs_ref[:, cols]
      finish(*env["pending"])

    def branches():
      env["yb"] = _dot((env["u"] * mx[...]).astype(BF16), w_pb_ref[...])
      env["ya"] = _dot(_head_norm_gate(oa[...], p_ref, gnw_ref).astype(BF16), w_pa_ref[...])

    def out():
      o_ref[rows_out, :] = _merge_out(x_ref[rows_out, :], g2_ref[...], env["ya"], env["yb"],
                                      p_ref, w_o_ref)

    return [gate, cumsum, prep] + [gla(c) for c in range(tm // CH)] + [gmlp, branches, out]

  def run_interleaved(mix, h, p_dst):
    gate, cumsum, prep, gla0, gla1, gmlp, branches, out = mix
    proj = lambda piece: _project_piece(h, w_refs, p_dst, piece)
    gate()
    proj(PROJ_QK)
    cumsum()
    proj(PROJ_ALR)
    proj(PROJ_UGV)
    prep()
    gla0()
    proj(PROJ_V)
    gla1()
    proj(PROJ_R)
    gmlp()
    branches()
    proj(PROJ_GATES)
    out()

  assert tm // CH == 2
  h1 = _norm_mod(x_ref[tm:2 * tm, :], nw, sc_ref[...], sh_ref[...])
  run_interleaved(mix_stages(pa_ref, 0), h1, pb_ref)
  h2 = _norm_mod(xn_ref[...], nw, scn_ref[...], shn_ref[...])
  run_interleaved(mix_stages(pb_ref, 1), h2, pa_ref)

  @pl.when((2 * s + 2) % tiles_per_batch == 0)
  def _():
    for hh in range(HEADS):
      s_out_ref[hh] = st_ref[hh].T


def _mix_prompt_call(x, mod3, nw, w_in3, w_a2, b_a, ltri, masks, gnw, w_pa, lnw, lnb, ws, bs_full,
                     w_pb, w_o, *, batch, seq):
  tm = TM_P
  tiles_per_batch = seq // tm
  n_tiles = batch * tiles_per_batch
  assert tiles_per_batch % 2 == 0
  next_tile = lambda s: jnp.minimum(2 * s + 2, n_tiles - 1)
  pair_spec = pl.BlockSpec((2 * tm, D), lambda s: (s, 0))

  def mod_spec(seg):
    return pl.BlockSpec((None, 1, D), lambda s: ((2 * s) // tiles_per_batch, 0, seg))

  def mod_next_spec(seg):
    return pl.BlockSpec((None, 1, D), lambda s: (next_tile(s) // tiles_per_batch, 0, seg))

  wa, walr, wb = w_in3
  in_specs = [
      pair_spec, pl.BlockSpec((tm, D), lambda s: (next_tile(s), 0)),
      mod_spec(3), mod_spec(4), mod_next_spec(3), mod_next_spec(4), mod_spec(5),
      _const_spec((1, D)), _const_spec(wa.shape), _const_spec(walr.shape), _const_spec(wb.shape),
      _const_spec((LANE, DK)), _const_spec((1, DK)),
      _const_spec(ltri.shape), _const_spec(masks.shape), _const_spec((1, DV)),
      _const_spec((DV, D)), _const_spec((1, D)), _const_spec((1, D)),
      _const_spec((GROUPS, GM_CHUNK, GM_CHUNK)), _const_spec((GM_CHUNK, D)),
      _const_spec((D, D)), _const_spec((D, D)),
  ]
  out_specs = [
      pair_spec,
      pl.BlockSpec((None, None, HEADS, HK, HV), lambda s: (0, (2 * s) // tiles_per_batch, 0, 0, 0)),
  ]
  return pl.pallas_call(
      functools.partial(_mix_prompt_kernel, tiles_per_batch=tiles_per_batch),
      grid=(n_tiles // 2,),
      in_specs=in_specs,
      out_specs=out_specs,
      out_shape=[jax.ShapeDtypeStruct((batch * seq, D), F32),
                 jax.ShapeDtypeStruct((1, batch, HEADS, HK, HV), F32)],
      scratch_shapes=[
          pltpu.VMEM((HEADS, HV, HK), F32),
          pltpu.VMEM((tm, D_P), F32),
          pltpu.VMEM((tm, D_P), F32),
          pltpu.VMEM((2, tm, DV), F32),
          pltpu.VMEM((2, tm, D), F32),
          pltpu.VMEM((2 * N_LEVELS + 4, tm, DK), BF16),
      ],
      compiler_params=pltpu.CompilerParams(
          dimension_semantics=("arbitrary",), vmem_limit_bytes=VMEM_LIMIT),
      name="mix_prompt",
  )(x, x, mod3, mod3, mod3, mod3, mod3, nw, wa, walr, wb, w_a2, b_a, ltri, masks, gnw, w_pa,
    lnw, lnb, ws, bs_full, w_pb, w_o)


def _mix_sample_kernel(x_ref, sh_ref, sc_ref, g2_ref, nw_ref, wa_ref, walr_ref, wb_ref,
                       w_a2_ref, b_a_ref, gnw_ref, w_pa_ref, lnw_ref, lnb_ref, ws0_ref, bs0_ref,
                       w_pb_ref, w_o_ref, s_ref, o_ref, s_out_ref, gvn_ref,
                       p_ref, oa_ref, bc_ref, kt_ref, v_ref):
  i = pl.program_id(0)
  rows = x_ref.shape[0]

  @pl.when(i == 0)
  def _():
    h = _norm_mod(x_ref[...], nw_ref[...], sc_ref[...], sh_ref[...])
    for piece in PROJ_ALL:
      _project_piece(h, (wa_ref, walr_ref, wb_ref), p_ref, piece)
    g = _log_decay(p_ref, w_a2_ref, b_a_ref)
    q = p_ref[:, C_Q:C_Q + DK] * (HK ** -0.5)
    k = p_ref[:, C_K:C_K + DK]
    eb = jnp.exp(g)
    eb_hi = eb.astype(BF16)
    eb_lo = (eb - eb_hi.astype(F32)).astype(BF16)
    qe = q * eb
    for hh in range(HEADS):
      ck = slice(hh * HK, (hh + 1) * HK)
      bc_ref[hh, 0:HK, :] = eb_hi[:, ck].astype(F32).T.astype(BF16)
      bc_ref[hh, HK:2 * HK, :] = eb_lo[:, ck].astype(F32).T.astype(BF16)
      bc_ref[hh, 2 * HK:3 * HK, :] = qe[:, ck].T.astype(BF16)
      kt_ref[hh] = k[:, ck].T.astype(BF16)
      vh = p_ref[:, C_V + hh * HV:C_V + (hh + 1) * HV]
      v_ref[hh] = vh.astype(BF16)
      qk = jnp.sum(q[:, ck] * k[:, ck], axis=-1, keepdims=True)
      oa_ref[:, hh * HV:(hh + 1) * HV] = qk * vh

  row_id = lax.broadcasted_iota(jnp.int32, (rows, HV), 0)
  for nl in range(NB):
    n = i * NB + nl
    hot = row_id == n
    one_hot = jnp.where(hot, 1.0, 0.0).astype(BF16)
    for hh in range(HEADS):
      bc = _dot(bc_ref[hh], one_hot)
      decay = bc[0:HK] + bc[HK:2 * HK]
      qe_b = bc[2 * HK:3 * HK]
      kv = _dot(kt_ref[hh], jnp.where(hot, v_ref[hh], jnp.zeros((), BF16)))
      st = s_ref[nl, hh]
      o_inter = jnp.sum(qe_b * st, axis=0, keepdims=True)
      cur = oa_ref[pl.ds(n, 1), hh * HV:(hh + 1) * HV]
      oa_ref[pl.ds(n, 1), hh * HV:(hh + 1) * HV] = cur + o_inter
      s_out_ref[nl, hh] = decay * st + kv

  @pl.when(i == pl.num_programs(0) - 1)
  def _():
    ya = _dot(_head_norm_gate(oa_ref[...], p_ref, gnw_ref).astype(BF16), w_pa_ref[...])
    gvn = _gv_norm(p_ref, lnw_ref, lnb_ref)
    gvn_ref[...] = gvn
    mixed = ws0_ref[...] * gvn + bs0_ref[...]
    u = _gelu(p_ref[:, C_U:C_U + D])
    yb = _dot((u * mixed).astype(BF16), w_pb_ref[...])
    o_ref[...] = _merge_out(x_ref[...], g2_ref[...], ya, yb, p_ref, w_o_ref)


def _mix_sample_call(x, mod, nw, w_in3, w_a2, b_a, gnw, w_pa, lnw, lnb, ws0, bs0, w_pb, w_o, state):
  rows = x.shape[0]

  def mod_spec(seg):
    return pl.BlockSpec((rows, D), lambda i: (0, seg), pipeline_mode=pl.Buffered(1))

  wa, walr, wb = w_in3
  state_spec = pl.BlockSpec((None, NB, HEADS, HK, HV), lambda i: (0, i, 0, 0, 0))
  in_specs = [
      _const_spec((rows, D)), mod_spec(3), mod_spec(4), mod_spec(5),
      _const_spec((1, D)), _const_spec(wa.shape), _const_spec(walr.shape), _const_spec(wb.shape),
      _const_spec((LANE, DK)), _const_spec((1, DK)),
      _const_spec((1, DV)), _const_spec((DV, D)), _const_spec((1, D)), _const_spec((1, D)),
      _const_spec((1, D)), _const_spec((1, D)), _const_spec((D, D)), _const_spec((D, D)),
      state_spec,
  ]
  out_specs = [
      pl.BlockSpec((rows, D), lambda i: (0, 0)),
      state_spec,
      pl.BlockSpec((rows, D), lambda i: (0, 0)),
  ]
  return pl.pallas_call(
      _mix_sample_kernel,
      grid=(rows // NB,),
      in_specs=in_specs,
      out_specs=out_specs,
      out_shape=[jax.ShapeDtypeStruct((rows, D), F32),
                 jax.ShapeDtypeStruct(state.shape, F32),
                 jax.ShapeDtypeStruct((rows, D), F32)],
      scratch_shapes=[
          pltpu.VMEM((rows, D_P), F32),
          pltpu.VMEM((rows, DV), F32),
          pltpu.VMEM((HEADS, 3 * HK, rows), BF16),
          pltpu.VMEM((HEADS, HK, rows), BF16),
          pltpu.VMEM((HEADS, rows, HV), BF16),
      ],
      compiler_params=pltpu.CompilerParams(
          dimension_semantics=("arbitrary",), vmem_limit_bytes=VMEM_LIMIT),
      name="mix_sample",
  )(x, mod, mod, mod, nw, wa, walr, wb, w_a2, b_a, gnw, w_pa, lnw, lnb, ws0, bs0, w_pb, w_o, state)


def _split_w_in(w_in):
  lo = 2 * DK + 2 * DV
  hi = lo + LOWRANK
  alr = jnp.pad(w_in[:, lo:hi], ((0, 0), (0, LANE - LOWRANK)))
  return w_in[:, :lo].astype(BF16), alr.astype(BF16), w_in[:, hi:].astype(BF16)


def kernel(x_prompt, x_sample, state_gla, c_prompt, c_sample, w_ada, b_ada, norm1_w, ffn1_w13, ffn1_w2, norm2_w, w_in, w_a2, b_a, gla_norm_w, w_pa, gm_ln_w, gm_ln_b, gm_ws, gm_bs, w_pb, w_o, norm3_w, ffn2_w13, ffn2_w2, normf_w):
  batch, seq, _ = x_prompt.shape
  dec = x_sample.shape[0]
  depth = w_ada.shape[0]
  assert depth == 1 and x_sample.shape[1] == 1 and dec % NB == 0
  assert seq % (2 * TM_P) == 0 and seq % TM_FFN == 0

  ltri_np, masks_np = _gla_constants()
  ltri = jnp.asarray(ltri_np, BF16)
  masks = jnp.asarray(masks_np, F32)

  xp = x_prompt.reshape(batch * seq, D)
  xs = x_sample.reshape(dec, D)
  l = 0
  mod_all = _ada_call(jnp.concatenate([c_prompt, c_sample], axis=0), w_ada[l],
                      b_ada[l].reshape(1, N_MOD * D))
  mod_p = mod_all[:batch].reshape(batch, 1, N_MOD * D)
  mod_s = mod_all[batch:]

  row = lambda a: a.reshape(1, -1)
  w13_1, w2_1 = ffn1_w13[l].astype(BF16), ffn1_w2[l].astype(BF16)
  w13_2, w2_2 = ffn2_w13[l].astype(BF16), ffn2_w2[l].astype(BF16)
  w_in3 = _split_w_in(w_in[l])
  w_a2_p = jnp.pad(w_a2[l], ((0, LANE - LOWRANK), (0, 0))).astype(BF16)
  w_pa_b, w_pb_b, w_o_b = w_pa[l].astype(BF16), w_pb[l].astype(BF16), w_o[l].astype(BF16)
  gnw = gla_norm_w[l].reshape(1, DV)
  bs_full = jnp.repeat(gm_bs[l].T, GC, axis=1)
  ws0 = jnp.repeat(gm_ws[l][:, 0, 0], GC).reshape(1, D)
  bs0 = bs_full[0:1]
  normf = row(normf_w)

  xp = _ffn_call(xp, mod_p, (0, 1, 2), row(norm1_w[l]), w13_1, w2_1, None,
                 per_row=False, rows_per_batch=seq, tm=TM_FFN, name="ffn1_prompt")
  xp, s_prompt = _mix_prompt_call(xp, mod_p, row(norm2_w[l]), w_in3, w_a2_p, row(b_a[l]), ltri, masks,
                                  gnw, w_pa_b, row(gm_ln_w[l]), row(gm_ln_b[l]), gm_ws[l], bs_full,
                                  w_pb_b, w_o_b, batch=batch, seq=seq)
  yp = _ffn_call(xp, mod_p, (6, 7, 8), row(norm3_w[l]), w13_2, w2_2, normf,
                 per_row=False, rows_per_batch=seq, tm=TM_FFN, name="ffn2_prompt")

  xs = _ffn_call(xs, mod_s, (0, 1, 2), row(norm1_w[l]), w13_1, w2_1, None,
                 per_row=True, rows_per_batch=dec, tm=dec, name="ffn1_sample")
  xs, s_sample, gvn = _mix_sample_call(xs, mod_s, row(norm2_w[l]), w_in3, w_a2_p, row(b_a[l]), gnw,
                                       w_pa_b, row(gm_ln_w[l]), row(gm_ln_b[l]), ws0, bs0,
                                       w_pb_b, w_o_b, state_gla)
  ys = _ffn_call(xs, mod_s, (6, 7, 8), row(norm3_w[l]), w13_2, w2_2, normf,
                 per_row=True, rows_per_batch=dec, tm=dec, name="ffn2_sample")

  return (yp.reshape(batch, seq, D), ys.reshape(dec, 1, D), s_prompt, s_sample,
          gvn.reshape(1, dec, 1, D))
```

```python
import functools

import numpy as np
import jax
import jax.numpy as jnp
from jax import lax
from jax.experimental import pallas as pl
from jax.experimental.pallas import tpu as pltpu

F32 = jnp.float32
BF16 = jnp.bfloat16

D = 1024
HEADS = 4
HK = 128
HV = 256
DK = HEADS * HK
DV = HEADS * HV
LOWRANK = 16
TAU = 16.0
GROUPS = 4
GC = D // GROUPS
GM_CHUNK = 128
F = 2816
N_MOD = 9
EPS = 1e-6
LOG2E = 1.4426950408889634

CH = 128
N_LEVELS = 7
LANE = 128

C_Q, C_K, C_ALR, C_U, C_GV, C_V, C_R, C_GA, C_GB = 0, 512, 1024, 1152, 2176, 3200, 4224, 5248, 6272
D_P = C_GB + D

PROJ_QK = (0, 0, 2 * DK, C_Q)
PROJ_ALR = (1, 0, LANE, C_ALR)
PROJ_UGV = (2, 0, 2 * D, C_U)
PROJ_V = (0, 2 * DK, 2 * DK + DV, C_V)
PROJ_R = (0, 2 * DK + DV, 2 * DK + 2 * DV, C_R)
PROJ_GATES = (2, 2 * D, 4 * D, C_GA)
PROJ_ALL = (PROJ_QK, PROJ_ALR, PROJ_UGV, PROJ_V, PROJ_R, PROJ_GATES)

TM_P = 256
TM_FFN = 512
NB = 8
VMEM_LIMIT = 58 * 1024 * 1024


def _gla_constants():
  t = np.arange(CH)
  masks = []
  for lvl in range(N_LEVELS):
    m = 1 << lvl
    upper = t >= (t // (2 * m)) * (2 * m) + m
    same_pair = (t[:, None] // (2 * m)) == (t[None, :] // (2 * m))
    masks.append((upper[:, None] & (~upper)[None, :] & same_pair).astype(np.float32))
  masks.append(np.eye(CH, dtype=np.float32))
  ltri = (t[None, :] <= t[:, None]).astype(np.float32)
  return ltri, np.stack(masks)


def _level_exponents(g, b):
  t = lax.broadcasted_iota(jnp.int32, g.shape, 0)
  g_prev = pltpu.roll(g, 1, 0)
  g_next = pltpu.roll(g, CH - 1, 0)
  r4 = t & 3
  levels = [
      jnp.where((t & 1) == 1, g, 0.0),
      jnp.where(r4 == 0, g_next, jnp.where(r4 == 1, 0.0, jnp.where(r4 == 2, g, g + g_prev))),
  ]
  groups = CH // 8
  row = lambda i: jnp.broadcast_to(b[i:i + 1, :], (8, b.shape[1]))
  b3 = [row(8 * j + 3) for j in range(groups)]
  b7 = [row(8 * j + 7) for j in range(groups)]
  levels.append(-jnp.abs(b - jnp.concatenate(b3, axis=0)))
  for lvl in range(3, N_LEVELS):
    m = 1 << lvl
    ref = [b7[((8 * j) // (2 * m)) * (2 * m) // 8 + m // 8 - 1] for j in range(groups)]
    levels.append(-jnp.abs(b - jnp.concatenate(ref, axis=0)))
  to_end = -jnp.abs(b - jnp.concatenate([b7[groups - 1]] * groups, axis=0))
  return levels, to_end


def _dot(a, b):
  return jnp.dot(a, b, preferred_element_type=F32)


def _dot_nt(a, b):
  return lax.dot_general(a, b, (((1,), (1,)), ((), ())), preferred_element_type=F32)


def _rms(x, w):
  ms = jnp.mean(x * x, axis=-1, keepdims=True)
  return x * lax.rsqrt(ms + EPS) * w


def _norm_mod(x, nw, sc, sh):
  return (_rms(x, nw) * (1.0 + sc) + sh).astype(BF16)


def _gelu(x):
  return 0.5 * x * (1.0 + lax.erf(x * (2.0 ** -0.5)))


def _log_sigmoid(z):
  return jnp.minimum(z, 0.0) - jnp.log1p(jnp.exp(-jnp.abs(z)))


def _const_spec(shape):
  nd = len(shape)
  return pl.BlockSpec(shape, lambda *_: (0,) * nd, pipeline_mode=pl.Buffered(1))


def _ada_kernel(c_ref, w_ref, b_ref, o_ref):
  c = c_ref[...]
  s = (c * jax.nn.sigmoid(c)).astype(BF16)
  o_ref[...] = _dot(s, w_ref[...].astype(BF16)) + b_ref[...]


def _ada_call(c_all, w_ada, b_ada):
  rows = c_all.shape[0]
  tn = 1024
  return pl.pallas_call(
      _ada_kernel,
      grid=(N_MOD * D // tn,),
      in_specs=[
          pl.BlockSpec((rows, D), lambda i: (0, 0)),
          pl.BlockSpec((D, tn), lambda i: (0, i)),
          pl.BlockSpec((1, tn), lambda i: (0, i)),
      ],
      out_specs=pl.BlockSpec((rows, tn), lambda i: (0, i)),
      out_shape=jax.ShapeDtypeStruct((rows, N_MOD * D), F32),
      compiler_params=pltpu.CompilerParams(dimension_semantics=("arbitrary",)),
      name="ada",
  )(c_all, w_ada, b_ada)


def _ffn_kernel(x_ref, sh_ref, sc_ref, g_ref, nw_ref, w13_ref, w2_ref, *rest, final_norm):
  o_ref = rest[-1]
  x = x_ref[...]
  h = _norm_mod(x, nw_ref[...], sc_ref[...], sh_ref[...])
  a = _dot(h, w13_ref[:, :F])
  b = _dot(h, w13_ref[:, F:])
  p = (a * jax.nn.sigmoid(a) * b).astype(BF16)
  out = x + 0.5 * g_ref[...] * _dot(p, w2_ref[...])
  if final_norm:
    out = _rms(out, rest[0][...])
  o_ref[...] = out


CAST_STEPS = 32
W2_CAST_ROWS = 2 * F // CAST_STEPS


def _ffn_cast_kernel(x_ref, sh_ref, sc_ref, g_ref, nw_ref, w13_ref, w2_ref,
                     w_in_ref, w_pa_ref, w_pb_ref, w_o_ref, n13_ref, n2_ref,
                     o_ref, wa_ref, walr_ref, wb_ref, pa_ref, pb_ref, po_ref, o13_ref, o2_ref):
  _ffn_kernel(x_ref, sh_ref, sc_ref, g_ref, nw_ref, w13_ref, w2_ref, o_ref, final_norm=False)
  lo = 2 * DK + 2 * DV
  hi = lo + LOWRANK
  wa_ref[...] = w_in_ref[:, :lo].astype(BF16)
  alr = w_in_ref[:, lo:hi]
  walr_ref[...] = jnp.concatenate(
      [alr, jnp.zeros((alr.shape[0], LANE - LOWRANK), F32)], axis=1).astype(BF16)
  wb_ref[...] = w_in_ref[:, hi:].astype(BF16)
  pa_ref[...] = w_pa_ref[...].astype(BF16)
  pb_ref[...] = w_pb_ref[...].astype(BF16)
  po_ref[...] = w_o_ref[...].astype(BF16)
  o13_ref[...] = n13_ref[...].astype(BF16)
  o2_ref[...] = n2_ref[...].astype(BF16)


def _ffn_cast_call(x, mod, segs, nw, w13, w2, w_in, w_pa, w_pb, w_o, n13, n2, *, rows_per_batch, tm):
  rows = x.shape[0]
  assert rows // tm == CAST_STEPS and D % CAST_STEPS == 0
  rb = D // CAST_STEPS
  d_in = w_in.shape[1]
  row_blk = lambda cols: pl.BlockSpec((rb, cols), lambda i: (i, 0))
  w2_blk = pl.BlockSpec((W2_CAST_ROWS, D), lambda i: (i // 2, 0))
  in_specs = [pl.BlockSpec((tm, D), lambda i: (i, 0))]
  in_specs += [_mod_spec(False, rows_per_batch, tm, s) for s in segs]
  in_specs += [_const_spec((1, D)), _const_spec((D, 2 * F)), _const_spec((F, D))]
  in_specs += [row_blk(d_in), row_blk(D), row_blk(D), row_blk(D), row_blk(2 * F), w2_blk]
  lo = 2 * DK + 2 * DV
  out_specs = [pl.BlockSpec((tm, D), lambda i: (i, 0)),
               row_blk(lo), row_blk(LANE), row_blk(d_in - lo - LOWRANK),
               row_blk(D), row_blk(D), row_blk(D), row_blk(2 * F), w2_blk]
  bf = lambda r, c: jax.ShapeDtypeStruct((r, c), BF16)
  out_shape = [jax.ShapeDtypeStruct((rows, D), F32),
               bf(D, lo), bf(D, LANE), bf(D, d_in - lo - LOWRANK),
               bf(D, D), bf(D, D), bf(D, D), bf(D, 2 * F), bf(F, D)]
  return pl.pallas_call(
      _ffn_cast_kernel,
      grid=(CAST_STEPS,),
      in_specs=in_specs,
      out_specs=out_specs,
      out_shape=out_shape,
      compiler_params=pltpu.CompilerParams(
          dimension_semantics=("arbitrary",), vmem_limit_bytes=VMEM_LIMIT),
      name="ffn1_prompt",
  )(x, mod, mod, mod, nw, w13, w2, w_in, w_pa, w_pb, w_o, n13, n2)


def _mod_spec(per_row, rows_per_batch, tm, seg):
  if per_row:
    return pl.BlockSpec((tm, D), lambda i: (i, seg))
  tiles = rows_per_batch // tm
  return pl.BlockSpec((None, 1, D), lambda i: (i // tiles, 0, seg))


def _ffn_call(x, mod, segs, nw, w13, w2, nf, *, per_row, rows_per_batch, tm, name):
  rows = x.shape[0]
  in_specs = [pl.BlockSpec((tm, D), lambda i: (i, 0))]
  in_specs += [_mod_spec(per_row, rows_per_batch, tm, s) for s in segs]
  in_specs += [_const_spec((1, D)), _const_spec((D, 2 * F)), _const_spec((F, D))]
  args = [x, mod, mod, mod, nw, w13, w2]
  if nf is not None:
    in_specs.append(_const_spec((1, D)))
    args.append(nf)
  return pl.pallas_call(
      functools.partial(_ffn_kernel, final_norm=nf is not None),
      grid=(rows // tm,),
      in_specs=in_specs,
      out_specs=pl.BlockSpec((tm, D), lambda i: (i, 0)),
      out_shape=jax.ShapeDtypeStruct((rows, D), F32),
      compiler_params=pltpu.CompilerParams(
          dimension_semantics=("arbitrary",), vmem_limit_bytes=VMEM_LIMIT),
      name=name,
  )(*args)


def _project_piece(h, w_refs, p_ref, piece):
  idx, lo, hi, dst = piece
  p_ref[:, dst:dst + hi - lo] = _dot(h, w_refs[idx][:, lo:hi])


def _log_decay(p_ref, w_a2_ref, b_a_ref):
  a_lr = p_ref[:, C_ALR:C_ALR + LANE].astype(BF16)
  z = _dot(a_lr, w_a2_ref[...]) + b_a_ref[...]
  return _log_sigmoid(z) * (1.0 / TAU)


def _head_norm_gate(o, p_ref, gnw_ref):
  parts = []
  for hh in range(HEADS):
    sl = slice(hh * HV, (hh + 1) * HV)
    parts.append(_rms(o[:, sl], gnw_ref[:, sl]))
  r = p_ref[:, C_R:C_R + DV]
  return jnp.concatenate(parts, axis=-1) * (r * jax.nn.sigmoid(r))


def _gv_norm(p_ref, lnw_ref, lnb_ref):
  gv = _gelu(p_ref[:, C_GV:C_GV + D])
  mu = jnp.mean(gv, axis=-1, keepdims=True)
  d = gv - mu
  var = jnp.mean(d * d, axis=-1, keepdims=True)
  return d * lax.rsqrt(var + EPS) * lnw_ref[...] + lnb_ref[...]


def _merge_out(x, g2, ya, yb, p_ref, w_o_ref):
  ga = p_ref[:, C_GA:C_GA + D]
  gb = p_ref[:, C_GB:C_GB + D]
  merged = (jax.nn.sigmoid(ga) * ya + jax.nn.sigmoid(gb) * yb).astype(BF16)
  return x + g2 * _dot(merged, w_o_ref[...])


def _mix_prompt_kernel(x_ref, sh_ref, sc_ref, g2_ref, nw_ref,
                       wa_ref, walr_ref, wb_ref, w_a2_ref, b_a_ref,
                       ltri_ref, mask_ref, gnw_ref, w_pa_ref, lnw_ref, lnb_ref, ws_ref, bs_ref,
                       w_pb_ref, w_o_ref, o_ref, s_out_ref,
                       st_ref, p_ref, oa_ref, mx_ref, qk_ref):
  j = pl.program_id(1)
  tm = TM_P
  w_refs = (wa_ref, walr_ref, wb_ref)

  @pl.when(j == 0)
  def _():
    st_ref[...] = jnp.zeros_like(st_ref)

  ltri = ltri_ref[...]
  ti = lax.broadcasted_iota(jnp.int32, (GM_CHUNK, GM_CHUNK), 0)
  si = lax.broadcasted_iota(jnp.int32, (GM_CHUNK, GM_CHUNK), 1)

  def mix_stages():
    oa = oa_ref
    mx = mx_ref
    env = {}

    def gate():
      env["g"] = _log_decay(p_ref, w_a2_ref, b_a_ref) * LOG2E

    def cumsum():
      cums = []
      for c in range(tm // CH):
        gc = env["g"][c * CH:(c + 1) * CH, :]
        g_hi = gc.astype(BF16)
        g_lo = (gc - g_hi.astype(F32)).astype(BF16)
        cums.append(_dot(ltri, g_hi) + _dot(ltri, g_lo))
      env["cums"] = cums

    def prep():
      slot = lambda i: i
      decays = []
      for c in range(tm // CH):
        rows = slice(c * CH, (c + 1) * CH)
        b = env["cums"][c]
        e_lvl, e_end = _level_exponents(env["g"][rows, :], b)
        q = (p_ref[rows, C_Q:C_Q + DK] * (HK ** -0.5)).astype(BF16)
        k = p_ref[rows, C_K:C_K + DK].astype(BF16)
        for lvl in range(N_LEVELS):
          e = jnp.exp2(e_lvl[lvl]).astype(BF16)
          qk_ref[slot(2 * lvl), rows, :] = q * e
          qk_ref[slot(2 * lvl + 1), rows, :] = k * e
        qk_ref[slot(2 * N_LEVELS), rows, :] = q
        qk_ref[slot(2 * N_LEVELS + 1), rows, :] = k
        qk_ref[slot(2 * N_LEVELS + 2), rows, :] = q * jnp.exp2(b).astype(BF16)
        qk_ref[slot(2 * N_LEVELS + 3), rows, :] = k * jnp.exp2(e_end).astype(BF16)
        decays.append(jnp.exp2(b[CH - 1:CH, :]))
      env["decays"] = decays
      env["gvn"] = _gv_norm(p_ref, lnw_ref, lnb_ref).astype(BF16)
      env["u"] = _gelu(p_ref[:, C_U:C_U + D])
      env["pending"] = None

    def level_scores(c, hh):
      rows = slice(c * CH, (c + 1) * CH)
      ck = slice(hh * HK, (hh + 1) * HK)
      scores = None
      for lvl in range(N_LEVELS + 1):
        part = mask_ref[lvl] * _dot_nt(qk_ref[2 * lvl, rows, ck], qk_ref[2 * lvl + 1, rows, ck])
        scores = part if scores is None else scores + part
      return scores.astype(BF16)

    def finish(c, hh, scores):
      rows = slice(c * CH, (c + 1) * CH)
      ck = slice(hh * HK, (hh + 1) * HK)
      v32 = p_ref[rows, C_V + hh * HV:C_V + (hh + 1) * HV]
      st = st_ref[hh]
      oa[rows, hh * HV:(hh + 1) * HV] = (
          _dot(scores, v32.astype(BF16))
          + _dot_nt(qk_ref[2 * N_LEVELS + 2, rows, ck], st.astype(BF16)))
      st_ref[hh] = (st * env["decays"][c][:, ck]
                    + _dot(v32.T.astype(BF16), qk_ref[2 * N_LEVELS + 3, rows, ck]))

    def gla(c):
      def run():
        for hh in range(HEADS):
          scores = level_scores(c, hh)
          if env["pending"] is not None:
            finish(*env["pending"])
          env["pending"] = (c, hh, scores)
      return run

    def gmlp():
      for gg in range(GROUPS):
        ws = jnp.where(si <= ti, ws_ref[gg], 0.0).astype(BF16)
        cols = slice(gg * GC, (gg + 1) * GC)
        for c in range(tm // GM_CHUNK):
          rows = slice(c * GM_CHUNK, (c + 1) * GM_CHUNK)
          mx[rows, cols] = _dot(ws, env["gvn"][rows, cols]) + bs_ref[:, cols]
      finish(*env["pending"])

    def branches():
      env["yb"] = _dot((env["u"] * mx[...]).astype(BF16), w_pb_ref[...])
      env["ya"] = _dot(_head_norm_gate(oa[...], p_ref, gnw_ref).astype(BF16), w_pa_ref[...])

    def out():
      o_ref[...] = _merge_out(x_ref[...], g2_ref[...], env["ya"], env["yb"], p_ref, w_o_ref)

    return [gate, cumsum, prep] + [gla(c) for c in range(tm // CH)] + [gmlp, branches, out]

  assert tm // CH == 2
  gate, cumsum, prep, gla0, gla1, gmlp, branches, out = mix_stages()
  h = _norm_mod(x_ref[...], nw_ref[...], sc_ref[...], sh_ref[...])
  proj = lambda piece: _project_piece(h, w_refs, p_ref, piece)
  proj(PROJ_QK)
  proj(PROJ_ALR)
  proj(PROJ_UGV)
  gate()
  proj(PROJ_V)
  cumsum()
  proj(PROJ_R)
  proj(PROJ_GATES)
  prep()
  gla0()
  gla1()
  gmlp()
  branches()
  out()

  @pl.when(j == pl.num_programs(1) - 1)
  def _():
    for hh in range(HEADS):
      s_out_ref[hh] = st_ref[hh].T


def _mix_prompt_call(x, mod3, nw, w_in3, w_a2, b_a, ltri, masks, gnw, w_pa, lnw, lnb, ws, bs_full,
                     w_pb, w_o, *, batch, seq):
  tm = TM_P
  tiles = seq // tm
  row_spec = pl.BlockSpec((tm, D), lambda b, j: (b * tiles + j, 0))

  def mod_spec(seg):
    return pl.BlockSpec((None, 1, D), lambda b, j: (b, 0, seg))

  wa, walr, wb = w_in3
  in_specs = [
      row_spec, mod_spec(3), mod_spec(4), mod_spec(5),
      _const_spec((1, D)), _const_spec(wa.shape), _const_spec(walr.shape), _const_spec(wb.shape),
      _const_spec((LANE, DK)), _const_spec((1, DK)),
      _const_spec(ltri.shape), _const_spec(masks.shape), _const_spec((1, DV)),
      _const_spec((DV, D)), _const_spec((1, D)), _const_spec((1, D)),
      _const_spec((GROUPS, GM_CHUNK, GM_CHUNK)), _const_spec((GM_CHUNK, D)),
      _const_spec((D, D)), _const_spec((D, D)),
  ]
  out_specs = [
      row_spec,
      pl.BlockSpec((None, None, HEADS, HK, HV), lambda b, j: (0, b, 0, 0, 0)),
  ]
  return pl.pallas_call(
      _mix_prompt_kernel,
      grid=(batch, tiles),
      in_specs=in_specs,
      out_specs=out_specs,
      out_shape=[jax.ShapeDtypeStruct((batch * seq, D), F32),
                 jax.ShapeDtypeStruct((1, batch, HEADS, HK, HV), F32)],
      scratch_shapes=[
          pltpu.VMEM((HEADS, HV, HK), F32),
          pltpu.VMEM((tm, D_P), F32),
          pltpu.VMEM((tm, DV), F32),
          pltpu.VMEM((tm, D), F32),
          pltpu.VMEM((2 * N_LEVELS + 4, tm, DK), BF16),
      ],
      compiler_params=pltpu.CompilerParams(
          dimension_semantics=("arbitrary", "arbitrary"), vmem_limit_bytes=VMEM_LIMIT),
      name="mix_prompt",
  )(x, mod3, mod3, mod3, nw, wa, walr, wb, w_a2, b_a, ltri, masks, gnw, w_pa,
    lnw, lnb, ws, bs_full, w_pb, w_o)


def _mix_sample_kernel(x_ref, sh_ref, sc_ref, g2_ref, nw_ref, wa_ref, walr_ref, wb_ref,
                       w_a2_ref, b_a_ref, gnw_ref, w_pa_ref, lnw_ref, lnb_ref, ws0_ref, bs0_ref,
                       w_pb_ref, w_o_ref, s_ref, o_ref, s_out_ref, gvn_ref,
                       p_ref, oa_ref, bc_ref, kt_ref, v_ref):
  i = pl.program_id(0)
  rows = x_ref.shape[0]

  @pl.when(i == 0)
  def _():
    h = _norm_mod(x_ref[...], nw_ref[...], sc_ref[...], sh_ref[...])
    for piece in PROJ_ALL:
      _project_piece(h, (wa_ref, walr_ref, wb_ref), p_ref, piece)
    g = _log_decay(p_ref, w_a2_ref, b_a_ref)
    q = p_ref[:, C_Q:C_Q + DK] * (HK ** -0.5)
    k = p_ref[:, C_K:C_K + DK]
    eb = jnp.exp(g)
    eb_hi = eb.astype(BF16)
    eb_lo = (eb - eb_hi.astype(F32)).astype(BF16)
    qe = q * eb
    for hh in range(HEADS):
      ck = slice(hh * HK, (hh + 1) * HK)
      bc_ref[hh, 0:HK, :] = eb_hi[:, ck].astype(F32).T.astype(BF16)
      bc_ref[hh, HK:2 * HK, :] = eb_lo[:, ck].astype(F32).T.astype(BF16)
      bc_ref[hh, 2 * HK:3 * HK, :] = qe[:, ck].T.astype(BF16)
      kt_ref[hh] = k[:, ck].T.astype(BF16)
      vh = p_ref[:, C_V + hh * HV:C_V + (hh + 1) * HV]
      v_ref[hh] = vh.astype(BF16)
      qk = jnp.sum(q[:, ck] * k[:, ck], axis=-1, keepdims=True)
      oa_ref[:, hh * HV:(hh + 1) * HV] = qk * vh

  row_id = lax.broadcasted_iota(jnp.int32, (rows, HV), 0)
  for nl in range(NB):
    n = i * NB + nl
    hot = row_id == n
    one_hot = jnp.where(hot, 1.0, 0.0).astype(BF16)
    for hh in range(HEADS):
      bc = _dot(bc_ref[hh], one_hot)
      decay = bc[0:HK] + bc[HK:2 * HK]
      qe_b = bc[2 * HK:3 * HK]
      kv = _dot(kt_ref[hh], jnp.where(hot, v_ref[hh], jnp.zeros((), BF16)))
      st = s_ref[nl, hh]
      o_inter = jnp.sum(qe_b * st, axis=0, keepdims=True)
      cur = oa_ref[pl.ds(n, 1), hh * HV:(hh + 1) * HV]
      oa_ref[pl.ds(n, 1), hh * HV:(hh + 1) * HV] = cur + o_inter
      s_out_ref[nl, hh] = decay * st + kv

  @pl.when(i == pl.num_programs(0) - 1)
  def _():
    ya = _dot(_head_norm_gate(oa_ref[...], p_ref, gnw_ref).astype(BF16), w_pa_ref[...])
    gvn = _gv_norm(p_ref, lnw_ref, lnb_ref)
    gvn_ref[...] = gvn
    mixed = ws0_ref[...] * gvn + bs0_ref[...]
    u = _gelu(p_ref[:, C_U:C_U + D])
    yb = _dot((u * mixed).astype(BF16), w_pb_ref[...])
    o_ref[...] = _merge_out(x_ref[...], g2_ref[...], ya, yb, p_ref, w_o_ref)


def _mix_sample_call(x, mod, nw, w_in3, w_a2, b_a, gnw, w_pa, lnw, lnb, ws0, bs0, w_pb, w_o, state):
  rows = x.shape[0]

  def mod_spec(seg):
    return pl.BlockSpec((rows, D), lambda i: (0, seg), pipeline_mode=pl.Buffered(1))

  wa, walr, wb = w_in3
  state_spec = pl.BlockSpec((None, NB, HEADS, HK, HV), lambda i: (0, i, 0, 0, 0))
  in_specs = [
      _const_spec((rows, D)), mod_spec(3), mod_spec(4), mod_spec(5),
      _const_spec((1, D)), _const_spec(wa.shape), _const_spec(walr.shape), _const_spec(wb.shape),
      _const_spec((LANE, DK)), _const_spec((1, DK)),
      _const_spec((1, DV)), _const_spec((DV, D)), _const_spec((1, D)), _const_spec((1, D)),
      _const_spec((1, D)), _const_spec((1, D)), _const_spec((D, D)), _const_spec((D, D)),
      state_spec,
  ]
  out_specs = [
      pl.BlockSpec((rows, D), lambda i: (0, 0)),
      state_spec,
      pl.BlockSpec((rows, D), lambda i: (0, 0)),
  ]
  return pl.pallas_call(
      _mix_sample_kernel,
      grid=(rows // NB,),
      in_specs=in_specs,
      out_specs=out_specs,
      out_shape=[jax.ShapeDtypeStruct((rows, D), F32),
                 jax.ShapeDtypeStruct(state.shape, F32),
                 jax.ShapeDtypeStruct((rows, D), F32)],
      scratch_shapes=[
          pltpu.VMEM((rows, D_P), F32),
          pltpu.VMEM((rows, DV), F32),
          pltpu.VMEM((HEADS, 3 * HK, rows), BF16),
          pltpu.VMEM((HEADS, HK, rows), BF16),
          pltpu.VMEM((HEADS, rows, HV), BF16),
      ],
      compiler_params=pltpu.CompilerParams(
          dimension_semantics=("arbitrary",), vmem_limit_bytes=VMEM_LIMIT),
      name="mix_sample",
  )(x, mod, mod, mod, nw, wa, walr, wb, w_a2, b_a, gnw, w_pa, lnw, lnb, ws0, bs0, w_pb, w_o, state)


def kernel(x_prompt, x_sample, state_gla, c_prompt, c_sample, w_ada, b_ada, norm1_w, ffn1_w13, ffn1_w2, norm2_w, w_in, w_a2, b_a, gla_norm_w, w_pa, gm_ln_w, gm_ln_b, gm_ws, gm_bs, w_pb, w_o, norm3_w, ffn2_w13, ffn2_w2, normf_w):
  batch, seq, _ = x_prompt.shape
  dec = x_sample.shape[0]
  depth = w_ada.shape[0]
  assert depth == 1 and x_sample.shape[1] == 1 and dec % NB == 0
  assert seq % TM_P == 0 and seq % TM_FFN == 0

  ltri_np, masks_np = _gla_constants()
  ltri = jnp.asarray(ltri_np, BF16)
  masks = jnp.asarray(masks_np, F32)

  xp = x_prompt.reshape(batch * seq, D)
  xs = x_sample.reshape(dec, D)
  l = 0
  mod_all = _ada_call(jnp.concatenate([c_sample, c_prompt], axis=0), w_ada[l],
                      b_ada[l].reshape(1, N_MOD * D))
  mod_s = mod_all
  mod_p = mod_all[dec:].reshape(batch, 1, N_MOD * D)

  row = lambda a: a.reshape(1, -1)
  w13_1, w2_1 = ffn1_w13[l].astype(BF16), ffn1_w2[l].astype(BF16)
  w_a2_p = jnp.pad(w_a2[l], ((0, LANE - LOWRANK), (0, 0))).astype(BF16)
  gnw = gla_norm_w[l].reshape(1, DV)
  bs_full = jnp.repeat(gm_bs[l].T, GC, axis=1)
  ws0 = jnp.repeat(gm_ws[l][:, 0, 0], GC).reshape(1, D)
  bs0 = bs_full[0:1]
  normf = row(normf_w)

  (xp, wa, walr, wb, w_pa_b, w_pb_b, w_o_b, w13_2, w2_2) = _ffn_cast_call(
      xp, mod_p, (0, 1, 2), row(norm1_w[l]), w13_1, w2_1, w_in[l], w_pa[l], w_pb[l], w_o[l],
      ffn2_w13[l], ffn2_w2[l], rows_per_batch=seq, tm=TM_FFN)
  w_in3 = (wa, walr, wb)
  xp, s_prompt = _mix_prompt_call(xp, mod_p, row(norm2_w[l]), w_in3, w_a2_p, row(b_a[l]), ltri, masks,
                                  gnw, w_pa_b, row(gm_ln_w[l]), row(gm_ln_b[l]), gm_ws[l], bs_full,
                                  w_pb_b, w_o_b, batch=batch, seq=seq)
  yp = _ffn_call(xp, mod_p, (6, 7, 8), row(norm3_w[l]), w13_2, w2_2, normf,
                 per_row=False, rows_per_batch=seq, tm=TM_FFN, name="ffn2_prompt")

  xs = _ffn_call(xs, mod_s, (0, 1, 2), row(norm1_w[l]), w13_1, w2_1, None,
                 per_row=True, rows_per_batch=dec, tm=dec, name="ffn1_sample")
  xs, s_sample, gvn = _mix_sample_call(xs, mod_s, row(norm2_w[l]), w_in3, w_a2_p, row(b_a[l]), gnw,
                                       w_pa_b, row(gm_ln_w[l]), row(gm_ln_b[l]), ws0, bs0,
                                       w_pb_b, w_o_b, state_gla)
  ys = _ffn_call(xs, mod_s, (6, 7, 8), row(norm3_w[l]), w13_2, w2_2, normf,
                 per_row=True, rows_per_batch=dec, tm=dec, name="ffn2_sample")

  return (yp.reshape(batch, seq, D), ys.reshape(dec, 1, D), s_prompt, s_sample,
          gvn.reshape(1, dec, 1, D))
```

```python
import functools

import numpy as np
import jax
import jax.numpy as jnp
from jax import lax
from jax.experimental import pallas as pl
from jax.experimental.pallas import tpu as pltpu

F32 = jnp.float32
BF16 = jnp.bfloat16

D = 1024
HEADS = 4
HK = 128
HV = 256
DK = HEADS * HK
DV = HEADS * HV
LOWRANK = 16
TAU = 16.0
GROUPS = 4
GC = D // GROUPS
GM_CHUNK = 128
F = 2816
N_MOD = 9
EPS = 1e-6
LOG2E = 1.4426950408889634

CH = 128
N_LEVELS = 7
LANE = 128

C_Q, C_K, C_ALR, C_U, C_GV, C_V, C_R, C_GA, C_GB = 0, 512, 1024, 1152, 2176, 3200, 4224, 5248, 6272
D_P = C_GB + D

PROJ_QK = (0, 0, 2 * DK, C_Q)
PROJ_ALR = (1, 0, LANE, C_ALR)
PROJ_UGV = (2, 0, 2 * D, C_U)
PROJ_V = (0, 2 * DK, 2 * DK + DV, C_V)
PROJ_R = (0, 2 * DK + DV, 2 * DK + 2 * DV, C_R)
PROJ_GATES = (2, 2 * D, 4 * D, C_GA)
PROJ_ALL = (PROJ_QK, PROJ_ALR, PROJ_UGV, PROJ_V, PROJ_R, PROJ_GATES)

TM_P = 256
TM_FFN = 512
NB = 8
VMEM_LIMIT = 58 * 1024 * 1024


def _gla_constants():
  t = np.arange(CH)
  masks = []
  for lvl in range(N_LEVELS):
    m = 1 << lvl
    upper = t >= (t // (2 * m)) * (2 * m) + m
    same_pair = (t[:, None] // (2 * m)) == (t[None, :] // (2 * m))
    masks.append((upper[:, None] & (~upper)[None, :] & same_pair).astype(np.float32))
  masks.append(np.eye(CH, dtype=np.float32))
  ltri = (t[None, :] <= t[:, None]).astype(np.float32)
  return ltri, np.stack(masks)


def _level_exponents(g, b):
  t = lax.broadcasted_iota(jnp.int32, g.shape, 0)
  g_prev = pltpu.roll(g, 1, 0)
  g_next = pltpu.roll(g, CH - 1, 0)
  r4 = t & 3
  levels = [
      jnp.where((t & 1) == 1, g, 0.0),
      jnp.where(r4 == 0, g_next, jnp.where(r4 == 1, 0.0, jnp.where(r4 == 2, g, g + g_prev))),
  ]
  groups = CH // 8
  row = lambda i: jnp.broadcast_to(b[i:i + 1, :], (8, b.shape[1]))
  b3 = [row(8 * j + 3) for j in range(groups)]
  b7 = [row(8 * j + 7) for j in range(groups)]
  levels.append(-jnp.abs(b - jnp.concatenate(b3, axis=0)))
  for lvl in range(3, N_LEVELS):
    m = 1 << lvl
    ref = [b7[((8 * j) // (2 * m)) * (2 * m) // 8 + m // 8 - 1] for j in range(groups)]
    levels.append(-jnp.abs(b - jnp.concatenate(ref, axis=0)))
  to_end = -jnp.abs(b - jnp.concatenate([b7[groups - 1]] * groups, axis=0))
  return levels, to_end


def _dot(a, b):
  return jnp.dot(a, b, preferred_element_type=F32)


def _dot_nt(a, b):
  return lax.dot_general(a, b, (((1,), (1,)), ((), ())), preferred_element_type=F32)


def _rms(x, w):
  ms = jnp.mean(x * x, axis=-1, keepdims=True)
  return x * lax.rsqrt(ms + EPS) * w


def _norm_mod(x, nw, sc, sh):
  return (_rms(x, nw) * (1.0 + sc) + sh).astype(BF16)


def _gelu(x):
  return 0.5 * x * (1.0 + lax.erf(x * (2.0 ** -0.5)))


def _log_sigmoid(z):
  return jnp.minimum(z, 0.0) - jnp.log1p(jnp.exp(-jnp.abs(z)))


def _const_spec(shape):
  nd = len(shape)
  return pl.BlockSpec(shape, lambda *_: (0,) * nd, pipeline_mode=pl.Buffered(1))


def _ada_kernel(c_ref, w_ref, b_ref, o_ref):
  c = c_ref[...]
  s = (c * jax.nn.sigmoid(c)).astype(BF16)
  o_ref[...] = _dot(s, w_ref[...].astype(BF16)) + b_ref[...]


def _ada_call(c_all, w_ada, b_ada):
  rows = c_all.shape[0]
  tn = 1024
  return pl.pallas_call(
      _ada_kernel,
      grid=(N_MOD * D // tn,),
      in_specs=[
          pl.BlockSpec((rows, D), lambda i: (0, 0)),
          pl.BlockSpec((D, tn), lambda i: (0, i)),
          pl.BlockSpec((1, tn), lambda i: (0, i)),
      ],
      out_specs=pl.BlockSpec((rows, tn), lambda i: (0, i)),
      out_shape=jax.ShapeDtypeStruct((rows, N_MOD * D), F32),
      compiler_params=pltpu.CompilerParams(dimension_semantics=("arbitrary",)),
      name="ada",
  )(c_all, w_ada, b_ada)


def _ffn_rows(x_ref, sh_ref, sc_ref, g_ref, nw_ref, w13_ref, w2_ref, nf_ref, o_ref):
  x = x_ref[...]
  h = _norm_mod(x, nw_ref[...], sc_ref[...], sh_ref[...])
  a = _dot(h, w13_ref[:, :F])
  b = _dot(h, w13_ref[:, F:])
  p = (a * jax.nn.sigmoid(a) * b).astype(BF16)
  out = x + 0.5 * g_ref[...] * _dot(p, w2_ref[...])
  if nf_ref is not None:
    out = _rms(out, nf_ref[...])
  o_ref[...] = out


def _ffn_kernel(x_ref, sh_ref, sc_ref, g_ref, xs_ref, shs_ref, scs_ref, gs_ref, nw_ref, w13_ref,
                w2_ref, *rest, final_norm, cast):
  rest = list(rest)
  nf_ref = rest.pop(0) if final_norm else None
  cast_in = [rest.pop(0) for _ in range(8)] if cast else None
  o_ref, os_ref = rest[0], rest[1]
  i = pl.program_id(0)
  n = pl.num_programs(0) - 1

  @pl.when(i < n)
  def _():
    _ffn_rows(x_ref, sh_ref, sc_ref, g_ref, nw_ref, w13_ref, w2_ref, nf_ref, o_ref)
    if cast:
      wta_ref, wtalr_ref, wtb_ref, w_pa_ref, w_pb_ref, w_o_ref, n13_ref, n2_ref = cast_in
      wa_ref, walr_ref, wb_ref, pa_ref, pb_ref, po_ref, o13_ref, o2_ref = rest[2:]
      wa_ref[...] = wta_ref[...].T.astype(BF16)
      wb_ref[...] = wtb_ref[...].T.astype(BF16)
      alr = jnp.concatenate([wtalr_ref[...], jnp.zeros((LANE - LOWRANK, D), F32)], axis=0)
      walr_ref[...] = alr.T.astype(BF16)
      pa_ref[...] = w_pa_ref[...].astype(BF16)
      pb_ref[...] = w_pb_ref[...].astype(BF16)
      po_ref[...] = w_o_ref[...].astype(BF16)
      o13_ref[...] = n13_ref[...].astype(BF16)
      o2_ref[...] = n2_ref[...].astype(BF16)

  @pl.when(i == n)
  def _():
    _ffn_rows(xs_ref, shs_ref, scs_ref, gs_ref, nw_ref, w13_ref, w2_ref, nf_ref, os_ref)


CAST_STEPS = 32
W2_CAST_ROWS = 2 * F // CAST_STEPS


def _ffn_call(x, xs, mod_p, mod_s, segs, nw, w13, w2, nf, cast_weights, *, rows_per_batch, tm, name):
  rows, rows_s = x.shape[0], xs.shape[0]
  n = rows // tm
  tiles = rows_per_batch // tm
  last = lambda i: jnp.minimum(i, n - 1)
  x_spec = pl.BlockSpec((tm, D), lambda i: (last(i), 0))
  xs_spec = pl.BlockSpec((rows_s, D), lambda i: (0, 0))
  in_specs = [x_spec]
  in_specs += [pl.BlockSpec((None, 1, D), lambda i, s=s: (last(i) // tiles, 0, s)) for s in segs]
  in_specs += [_const_spec((rows_s, D))]
  in_specs += [pl.BlockSpec((rows_s, D), lambda i, s=s: (0, s), pipeline_mode=pl.Buffered(1))
               for s in segs]
  in_specs += [_const_spec((1, D)), _const_spec((D, 2 * F)), _const_spec((F, D))]
  args = [x, mod_p, mod_p, mod_p, xs, mod_s, mod_s, mod_s, nw, w13, w2]
  if nf is not None:
    in_specs.append(_const_spec((1, D)))
    args.append(nf)
  out_specs = [x_spec, xs_spec]
  out_shape = [jax.ShapeDtypeStruct((rows, D), F32), jax.ShapeDtypeStruct((rows_s, D), F32)]
  if cast_weights is not None:
    assert n == CAST_STEPS and D % CAST_STEPS == 0
    w_t, w_pa, w_pb, w_o, n13, n2 = cast_weights
    lo = 2 * DK + 2 * DV
    n_a, n_b = lo // LANE, (w_t.shape[0] - lo - LOWRANK) // LANE
    assert n_a <= CAST_STEPS and n_b == CAST_STEPS
    rb = D // CAST_STEPS
    step = lambda i: jnp.minimum(i, CAST_STEPS - 1)
    a_blk = lambda i: jnp.minimum(i, n_a - 1)
    elem = lambda r: (pl.Element(r), pl.Element(D))
    row_blk = lambda cols: pl.BlockSpec((rb, cols), lambda i: (step(i), 0))
    w2_blk = pl.BlockSpec((W2_CAST_ROWS, D), lambda i: (step(i) // 2, 0))
    in_specs += [
        pl.BlockSpec(elem(LANE), lambda i: (pl.multiple_of(LANE * a_blk(i), LANE), 0)),
        pl.BlockSpec(elem(LOWRANK), lambda i: (lo, 0)),
        pl.BlockSpec(elem(LANE), lambda i: (pl.multiple_of(lo + LOWRANK + LANE * step(i), LOWRANK), 0)),
        row_blk(D), row_blk(D), row_blk(D), row_blk(2 * F), w2_blk]
    args += [w_t, w_t, w_t, w_pa, w_pb, w_o, n13, n2]
    out_specs += [
        pl.BlockSpec((D, LANE), lambda i: (0, a_blk(i))),
        pl.BlockSpec((D, LANE), lambda i: (0, 0)),
        pl.BlockSpec((D, LANE), lambda i: (0, step(i))),
        row_blk(D), row_blk(D), row_blk(D), row_blk(2 * F), w2_blk]
    bf = lambda r, c: jax.ShapeDtypeStruct((r, c), BF16)
    out_shape += [bf(D, lo), bf(D, LANE), bf(D, n_b * LANE),
                  bf(D, D), bf(D, D), bf(D, D), bf(D, 2 * F), bf(F, D)]
  return pl.pallas_call(
      functools.partial(_ffn_kernel, final_norm=nf is not None, cast=cast_weights is not None),
      grid=(n + 1,),
      in_specs=in_specs,
      out_specs=out_specs,
      out_shape=out_shape,
      compiler_params=pltpu.CompilerParams(
          dimension_semantics=("arbitrary",), vmem_limit_bytes=VMEM_LIMIT),
      name=name,
  )(*args)


def _project_piece(h, w_refs, p_ref, piece):
  idx, lo, hi, dst = piece
  p_ref[:, dst:dst + hi - lo] = _dot(h, w_refs[idx][:, lo:hi])


def _log_decay(p_ref, w_a2_ref, b_a_ref):
  a_lr = p_ref[:, C_ALR:C_ALR + LANE].astype(BF16)
  z = _dot(a_lr, w_a2_ref[...]) + b_a_ref[...]
  return _log_sigmoid(z) * (1.0 / TAU)


def _head_norm_gate(o, p_ref, gnw_ref):
  parts = []
  for hh in range(HEADS):
    sl = slice(hh * HV, (hh + 1) * HV)
    parts.append(_rms(o[:, sl], gnw_ref[:, sl]))
  r = p_ref[:, C_R:C_R + DV]
  return jnp.concatenate(parts, axis=-1) * (r * jax.nn.sigmoid(r))


def _gv_norm(p_ref, lnw_ref, lnb_ref):
  gv = _gelu(p_ref[:, C_GV:C_GV + D])
  mu = jnp.mean(gv, axis=-1, keepdims=True)
  d = gv - mu
  var = jnp.mean(d * d, axis=-1, keepdims=True)
  return d * lax.rsqrt(var + EPS) * lnw_ref[...] + lnb_ref[...]


def _merge_out(x, g2, ya, yb, p_ref, w_o_ref):
  ga = p_ref[:, C_GA:C_GA + D]
  gb = p_ref[:, C_GB:C_GB + D]
  merged = (jax.nn.sigmoid(ga) * ya + jax.nn.sigmoid(gb) * yb).astype(BF16)
  return x + g2 * _dot(merged, w_o_ref[...])


def _mix_prompt_kernel(x_ref, sh_ref, sc_ref, g2_ref, nw_ref,
                       wa_ref, walr_ref, wb_ref, w_a2_ref, b_a_ref,
                       ltri_ref, mask_ref, gnw_ref, w_pa_ref, lnw_ref, lnb_ref, ws_ref, bs_ref,
                       w_pb_ref, w_o_ref, o_ref, s_out_ref,
                       st_ref, p_ref, oa_ref, mx_ref, qk_ref):
  j = pl.program_id(1)
  tm = TM_P
  w_refs = (wa_ref, walr_ref, wb_ref)

  @pl.when(j == 0)
  def _():
    st_ref[...] = jnp.zeros_like(st_ref)

  ltri = ltri_ref[...]
  ti = lax.broadcasted_iota(jnp.int32, (GM_CHUNK, GM_CHUNK), 0)
  si = lax.broadcasted_iota(jnp.int32, (GM_CHUNK, GM_CHUNK), 1)

  def mix_stages():
    oa = oa_ref
    mx = mx_ref
    env = {}

    def gate():
      env["g"] = _log_decay(p_ref, w_a2_ref, b_a_ref) * LOG2E

    def cumsum():
      cums = []
      for c in range(tm // CH):
        gc = env["g"][c * CH:(c + 1) * CH, :]
        g_hi = gc.astype(BF16)
        g_lo = (gc - g_hi.astype(F32)).astype(BF16)
        cums.append(_dot(ltri, g_hi) + _dot(ltri, g_lo))
      env["cums"] = cums

    def prep():
      slot = lambda i: i
      decays = []
      for c in range(tm // CH):
        rows = slice(c * CH, (c + 1) * CH)
        b = env["cums"][c]
        e_lvl, e_end = _level_exponents(env["g"][rows, :], b)
        q = (p_ref[rows, C_Q:C_Q + DK] * (HK ** -0.5)).astype(BF16)
        k = p_ref[rows, C_K:C_K + DK].astype(BF16)
        for lvl in range(N_LEVELS):
          e = jnp.exp2(e_lvl[lvl]).astype(BF16)
          qk_ref[slot(2 * lvl), rows, :] = q * e
          qk_ref[slot(2 * lvl + 1), rows, :] = k * e
        qk_ref[slot(2 * N_LEVELS), rows, :] = q
        qk_ref[slot(2 * N_LEVELS + 1), rows, :] = k
        qk_ref[slot(2 * N_LEVELS + 2), rows, :] = q * jnp.exp2(b).astype(BF16)
        qk_ref[slot(2 * N_LEVELS + 3), rows, :] = k * jnp.exp2(e_end).astype(BF16)
        decays.append(jnp.exp2(b[CH - 1:CH, :]))
      env["decays"] = decays
      env["gvn"] = _gv_norm(p_ref, lnw_ref, lnb_ref).astype(BF16)
      env["u"] = _gelu(p_ref[:, C_U:C_U + D])
      env["pending"] = None

    def level_scores(c, hh):
      rows = slice(c * CH, (c + 1) * CH)
      ck = slice(hh * HK, (hh + 1) * HK)
      scores = None
      for lvl in range(N_LEVELS + 1):
        part = mask_ref[lvl] * _dot_nt(qk_ref[2 * lvl, rows, ck], qk_ref[2 * lvl + 1, rows, ck])
        scores = part if scores is None else scores + part
      return scores.astype(BF16)

    def finish(c, hh, scores):
      rows = slice(c * CH, (c + 1) * CH)
      ck = slice(hh * HK, (hh + 1) * HK)
      v32 = p_ref[rows, C_V + hh * HV:C_V + (hh + 1) * HV]
      st = st_ref[hh]
      oa[rows, hh * HV:(hh + 1) * HV] = (
          _dot(scores, v32.astype(BF16))
          + _dot_nt(qk_ref[2 * N_LEVELS + 2, rows, ck], st.astype(BF16)))
      st_ref[hh] = (st * env["decays"][c][:, ck]
                    + _dot(v32.T.astype(BF16), qk_ref[2 * N_LEVELS + 3, rows, ck]))

    def gla(c):
      def run():
        for hh in range(HEADS):
          scores = level_scores(c, hh)
          if env["pending"] is not None:
            finish(*env["pending"])
          env["pending"] = (c, hh, scores)
      return run

    def gmlp():
      for gg in range(GROUPS):
        ws = jnp.where(si <= ti, ws_ref[gg], 0.0).astype(BF16)
        cols = slice(gg * GC, (gg + 1) * GC)
        for c in range(tm // GM_CHUNK):
          rows = slice(c * GM_CHUNK, (c + 1) * GM_CHUNK)
          mx[rows, cols] = _dot(ws, env["gvn"][rows, cols]) + bs_ref[:, cols]
      finish(*env["pending"])

    def branches():
      env["yb"] = _dot((env["u"] * mx[...]).astype(BF16), w_pb_ref[...])
      env["ya"] = _dot(_head_norm_gate(oa[...], p_ref, gnw_ref).astype(BF16), w_pa_ref[...])

    def out():
      o_ref[...] = _merge_out(x_ref[...], g2_ref[...], env["ya"], env["yb"], p_ref, w_o_ref)

    return [gate, cumsum, prep] + [gla(c) for c in range(tm // CH)] + [gmlp, branches, out]

  assert tm // CH == 2
  gate, cumsum, prep, gla0, gla1, gmlp, branches, out = mix_stages()
  h = _norm_mod(x_ref[...], nw_ref[...], sc_ref[...], sh_ref[...])
  proj = lambda piece: _project_piece(h, w_refs, p_ref, piece)
  proj(PROJ_QK)
  proj(PROJ_ALR)
  proj(PROJ_UGV)
  gate()
  proj(PROJ_V)
  cumsum()
  proj(PROJ_R)
  proj(PROJ_GATES)
  prep()
  gla0()
  gla1()
  gmlp()
  branches()
  out()

  @pl.when(j == pl.num_programs(1) - 1)
  def _():
    for hh in range(HEADS):
      s_out_ref[hh] = st_ref[hh].T


def _mix_prompt_call(x, mod3, nw, w_in3, w_a2, b_a, ltri, masks, gnw, w_pa, lnw, lnb, ws, bs_full,
                     w_pb, w_o, *, batch, seq):
  tm = TM_P
  tiles = seq // tm
  row_spec = pl.BlockSpec((tm, D), lambda b, j: (b * tiles + j, 0))

  def mod_spec(seg):
    return pl.BlockSpec((None, 1, D), lambda b, j: (b, 0, seg))

  wa, walr, wb = w_in3
  in_specs = [
      row_spec, mod_spec(3), mod_spec(4), mod_spec(5),
      _const_spec((1, D)), _const_spec(wa.shape), _const_spec(walr.shape), _const_spec(wb.shape),
      _const_spec((LANE, DK)), _const_spec((1, DK)),
      _const_spec(ltri.shape), _const_spec(masks.shape), _const_spec((1, DV)),
      _const_spec((DV, D)), _const_spec((1, D)), _const_spec((1, D)),
      _const_spec((GROUPS, GM_CHUNK, GM_CHUNK)), _const_spec((GM_CHUNK, D)),
      _const_spec((D, D)), _const_spec((D, D)),
  ]
  out_specs = [
      row_spec,
      pl.BlockSpec((None, None, HEADS, HK, HV), lambda b, j: (0, b, 0, 0, 0)),
  ]
  return pl.pallas_call(
      _mix_prompt_kernel,
      grid=(batch, tiles),
      in_specs=in_specs,
      out_specs=out_specs,
      out_shape=[jax.ShapeDtypeStruct((batch * seq, D), F32),
                 jax.ShapeDtypeStruct((1, batch, HEADS, HK, HV), F32)],
      scratch_shapes=[
          pltpu.VMEM((HEADS, HV, HK), F32),
          pltpu.VMEM((tm, D_P), F32),
          pltpu.VMEM((tm, DV), F32),
          pltpu.VMEM((tm, D), F32),
          pltpu.VMEM((2 * N_LEVELS + 4, tm, DK), BF16),
      ],
      compiler_params=pltpu.CompilerParams(
          dimension_semantics=("arbitrary", "arbitrary"), vmem_limit_bytes=VMEM_LIMIT),
      name="mix_prompt",
  )(x, mod3, mod3, mod3, nw, wa, walr, wb, w_a2, b_a, ltri, masks, gnw, w_pa,
    lnw, lnb, ws, bs_full, w_pb, w_o)


def _mix_sample_kernel(x_ref, sh_ref, sc_ref, g2_ref, nw_ref, wa_ref, walr_ref, wb_ref,
                       w_a2_ref, b_a_ref, gnw_ref, w_pa_ref, lnw_ref, lnb_ref, ws0_ref, bs0_ref,
                       w_pb_ref, w_o_ref, s_ref, o_ref, s_out_ref, gvn_ref,
                       p_ref, oa_ref, bc_ref, kt_ref, v_ref):
  i = pl.program_id(0)
  rows = x_ref.shape[0]

  @pl.when(i == 0)
  def _():
    h = _norm_mod(x_ref[...], nw_ref[...], sc_ref[...], sh_ref[...])
    for piece in PROJ_ALL:
      _project_piece(h, (wa_ref, walr_ref, wb_ref), p_ref, piece)
    g = _log_decay(p_ref, w_a2_ref, b_a_ref)
    q = p_ref[:, C_Q:C_Q + DK] * (HK ** -0.5)
    k = p_ref[:, C_K:C_K + DK]
    eb = jnp.exp(g)
    eb_hi = eb.astype(BF16)
    eb_lo = (eb - eb_hi.astype(F32)).astype(BF16)
    qe = q * eb
    for hh in range(HEADS):
      ck = slice(hh * HK, (hh + 1) * HK)
      bc_ref[hh, 0:HK, :] = eb_hi[:, ck].astype(F32).T.astype(BF16)
      bc_ref[hh, HK:2 * HK, :] = eb_lo[:, ck].astype(F32).T.astype(BF16)
      bc_ref[hh, 2 * HK:3 * HK, :] = qe[:, ck].T.astype(BF16)
      kt_ref[hh] = k[:, ck].T.astype(BF16)
      vh = p_ref[:, C_V + hh * HV:C_V + (hh + 1) * HV]
      v_ref[hh] = vh.astype(BF16)
      qk = jnp.sum(q[:, ck] * k[:, ck], axis=-1, keepdims=True)
      oa_ref[:, hh * HV:(hh + 1) * HV] = qk * vh

  row_id = lax.broadcasted_iota(jnp.int32, (rows, HV), 0)
  for nl in range(NB):
    n = i * NB + nl
    hot = row_id == n
    one_hot = jnp.where(hot, 1.0, 0.0).astype(BF16)
    for hh in range(HEADS):
      bc = _dot(bc_ref[hh], one_hot)
      decay = bc[0:HK] + bc[HK:2 * HK]
      qe_b = bc[2 * HK:3 * HK]
      kv = _dot(kt_ref[hh], jnp.where(hot, v_ref[hh], jnp.zeros((), BF16)))
      st = s_ref[nl, hh]
      o_inter = jnp.sum(qe_b * st, axis=0, keepdims=True)
      cur = oa_ref[pl.ds(n, 1), hh * HV:(hh + 1) * HV]
      oa_ref[pl.ds(n, 1), hh * HV:(hh + 1) * HV] = cur + o_inter
      s_out_ref[nl, hh] = decay * st + kv

  @pl.when(i == pl.num_programs(0) - 1)
  def _():
    ya = _dot(_head_norm_gate(oa_ref[...], p_ref, gnw_ref).astype(BF16), w_pa_ref[...])
    gvn = _gv_norm(p_ref, lnw_ref, lnb_ref)
    gvn_ref[...] = gvn
    mixed = ws0_ref[...] * gvn + bs0_ref[...]
    u = _gelu(p_ref[:, C_U:C_U + D])
    yb = _dot((u * mixed).astype(BF16), w_pb_ref[...])
    o_ref[...] = _merge_out(x_ref[...], g2_ref[...], ya, yb, p_ref, w_o_ref)


def _mix_sample_call(x, mod, nw, w_in3, w_a2, b_a, gnw, w_pa, lnw, lnb, ws0, bs0, w_pb, w_o, state):
  rows = x.shape[0]

  def mod_spec(seg):
    return pl.BlockSpec((rows, D), lambda i: (0, seg), pipeline_mode=pl.Buffered(1))

  wa, walr, wb = w_in3
  state_spec = pl.BlockSpec((None, NB, HEADS, HK, HV), lambda i: (0, i, 0, 0, 0))
  in_specs = [
      _const_spec((rows, D)), mod_spec(3), mod_spec(4), mod_spec(5),
      _const_spec((1, D)), _const_spec(wa.shape), _const_spec(walr.shape), _const_spec(wb.shape),
      _const_spec((LANE, DK)), _const_spec((1, DK)),
      _const_spec((1, DV)), _const_spec((DV, D)), _const_spec((1, D)), _const_spec((1, D)),
      _const_spec((1, D)), _const_spec((1, D)), _const_spec((D, D)), _const_spec((D, D)),
      state_spec,
  ]
  out_specs = [
      pl.BlockSpec((rows, D), lambda i: (0, 0)),
      state_spec,
      pl.BlockSpec((rows, D), lambda i: (0, 0)),
  ]
  return pl.pallas_call(
      _mix_sample_kernel,
      grid=(rows // NB,),
      in_specs=in_specs,
      out_specs=out_specs,
      out_shape=[jax.ShapeDtypeStruct((rows, D), F32),
                 jax.ShapeDtypeStruct(state.shape, F32),
                 jax.ShapeDtypeStruct((rows, D), F32)],
      scratch_shapes=[
          pltpu.VMEM((rows, D_P), F32),
          pltpu.VMEM((rows, DV), F32),
          pltpu.VMEM((HEADS, 3 * HK, rows), BF16),
          pltpu.VMEM((HEADS, HK, rows), BF16),
          pltpu.VMEM((HEADS, rows, HV), BF16),
      ],
      compiler_params=pltpu.CompilerParams(
          dimension_semantics=("arbitrary",), vmem_limit_bytes=VMEM_LIMIT),
      name="mix_sample",
  )(x, mod, mod, mod, nw, wa, walr, wb, w_a2, b_a, gnw, w_pa, lnw, lnb, ws0, bs0, w_pb, w_o, state)


def kernel(x_prompt, x_sample, state_gla, c_prompt, c_sample, w_ada, b_ada, norm1_w, ffn1_w13, ffn1_w2, norm2_w, w_in, w_a2, b_a, gla_norm_w, w_pa, gm_ln_w, gm_ln_b, gm_ws, gm_bs, w_pb, w_o, norm3_w, ffn2_w13, ffn2_w2, normf_w):
  batch, seq, _ = x_prompt.shape
  dec = x_sample.shape[0]
  depth = w_ada.shape[0]
  assert depth == 1 and x_sample.shape[1] == 1 and dec % NB == 0
  assert seq % TM_P == 0 and seq % TM_FFN == 0

  ltri_np, masks_np = _gla_constants()
  ltri = jnp.asarray(ltri_np, BF16)
  masks = jnp.asarray(masks_np, F32)

  xp = x_prompt.reshape(batch * seq, D)
  xs = x_sample.reshape(dec, D)
  l = 0
  mod_all = _ada_call(jnp.concatenate([c_sample, c_prompt], axis=0), w_ada[l],
                      b_ada[l].reshape(1, N_MOD * D))
  mod_s = mod_all
  mod_p = mod_all[dec:].reshape(batch, 1, N_MOD * D)

  row = lambda a: a.reshape(1, -1)
  w13_1, w2_1 = ffn1_w13[l].astype(BF16), ffn1_w2[l].astype(BF16)
  w_a2_p = jnp.pad(w_a2[l], ((0, LANE - LOWRANK), (0, 0))).astype(BF16)
  gnw = gla_norm_w[l].reshape(1, DV)
  bs_full = jnp.repeat(gm_bs[l].T, GC, axis=1)
  ws0 = jnp.repeat(gm_ws[l][:, 0, 0], GC).reshape(1, D)
  bs0 = bs_full[0:1]
  normf = row(normf_w)

  cast_weights = (jnp.transpose(w_in[l]), w_pa[l], w_pb[l], w_o[l], ffn2_w13[l], ffn2_w2[l])
  (xp, xs, wa, walr, wb, w_pa_b, w_pb_b, w_o_b, w13_2, w2_2) = _ffn_call(
      xp, xs, mod_p, mod_s, (0, 1, 2), row(norm1_w[l]), w13_1, w2_1, None, cast_weights,
      rows_per_batch=seq, tm=TM_FFN, name="ffn1")
  w_in3 = (wa, walr, wb)
  xp, s_prompt = _mix_prompt_call(xp, mod_p, row(norm2_w[l]), w_in3, w_a2_p, row(b_a[l]), ltri, masks,
                                  gnw, w_pa_b, row(gm_ln_w[l]), row(gm_ln_b[l]), gm_ws[l], bs_full,
                                  w_pb_b, w_o_b, batch=batch, seq=seq)
  xs, s_sample, gvn = _mix_sample_call(xs, mod_s, row(norm2_w[l]), w_in3, w_a2_p, row(b_a[l]), gnw,
                                       w_pa_b, row(gm_ln_w[l]), row(gm_ln_b[l]), ws0, bs0,
                                       w_pb_b, w_o_b, state_gla)
  yp, ys = _ffn_call(xp, xs, mod_p, mod_s, (6, 7, 8), row(norm3_w[l]), w13_2, w2_2, normf, None,
                     rows_per_batch=seq, tm=TM_FFN, name="ffn2")

  return (yp.reshape(batch, seq, D), ys.reshape(dec, 1, D), s_prompt, s_sample,
          gvn.reshape(1, dec, 1, D))
```

```python
import functools

import numpy as np
import jax
import jax.numpy as jnp
from jax import lax
from jax.experimental import pallas as pl
from jax.experimental.pallas import tpu as pltpu

F32 = jnp.float32
BF16 = jnp.bfloat16

D = 1024
HEADS = 4
HK = 128
HV = 256
DK = HEADS * HK
DV = HEADS * HV
LOWRANK = 16
TAU = 16.0
GROUPS = 4
GC = D // GROUPS
GM_CHUNK = 128
F = 2816
N_MOD = 9
EPS = 1e-6
LOG2E = 1.4426950408889634

CH = 128
N_LEVELS = 7
LANE = 128

C_Q, C_K, C_ALR, C_U, C_GV, C_V, C_R, C_GA, C_GB = 0, 512, 1024, 1152, 2176, 3200, 4224, 5248, 6272
D_P = C_GB + D

PROJ_QK = (0, 0, 2 * DK, C_Q)
PROJ_ALR = (1, 0, LANE, C_ALR)
PROJ_UGV = (2, 0, 2 * D, C_U)
PROJ_V = (0, 2 * DK, 2 * DK + DV, C_V)
PROJ_R = (0, 2 * DK + DV, 2 * DK + 2 * DV, C_R)
PROJ_GATES = (2, 2 * D, 4 * D, C_GA)
PROJ_ALL = (PROJ_QK, PROJ_ALR, PROJ_UGV, PROJ_V, PROJ_R, PROJ_GATES)

TM_P = 512
TM_FFN = 512
NB = 8
VMEM_LIMIT = 58 * 1024 * 1024


def _gla_constants():
  t = np.arange(CH)
  masks = []
  for lvl in range(N_LEVELS):
    m = 1 << lvl
    upper = t >= (t // (2 * m)) * (2 * m) + m
    same_pair = (t[:, None] // (2 * m)) == (t[None, :] // (2 * m))
    masks.append((upper[:, None] & (~upper)[None, :] & same_pair).astype(np.float32))
  masks.append(np.eye(CH, dtype=np.float32))
  ltri = (t[None, :] <= t[:, None]).astype(np.float32)
  return ltri, np.stack(masks)


def _level_exponents(g, b):
  t = lax.broadcasted_iota(jnp.int32, g.shape, 0)
  g_prev = pltpu.roll(g, 1, 0)
  g_next = pltpu.roll(g, CH - 1, 0)
  r4 = t & 3
  levels = [
      jnp.where((t & 1) == 1, g, 0.0),
      jnp.where(r4 == 0, g_next, jnp.where(r4 == 1, 0.0, jnp.where(r4 == 2, g, g + g_prev))),
  ]
  groups = CH // 8
  row = lambda i: jnp.broadcast_to(b[i:i + 1, :], (8, b.shape[1]))
  b3 = [row(8 * j + 3) for j in range(groups)]
  b7 = [row(8 * j + 7) for j in range(groups)]
  levels.append(-jnp.abs(b - jnp.concatenate(b3, axis=0)))
  for lvl in range(3, N_LEVELS):
    m = 1 << lvl
    ref = [b7[((8 * j) // (2 * m)) * (2 * m) // 8 + m // 8 - 1] for j in range(groups)]
    levels.append(-jnp.abs(b - jnp.concatenate(ref, axis=0)))
  to_end = -jnp.abs(b - jnp.concatenate([b7[groups - 1]] * groups, axis=0))
  return levels, to_end


def _dot(a, b):
  return jnp.dot(a, b, preferred_element_type=F32)


def _dot_nt(a, b):
  return lax.dot_general(a, b, (((1,), (1,)), ((), ())), preferred_element_type=F32)


def _rms(x, w):
  ms = jnp.mean(x * x, axis=-1, keepdims=True)
  return x * lax.rsqrt(ms + EPS) * w


def _norm_mod(x, nw, sc, sh):
  return (_rms(x, nw) * (1.0 + sc) + sh).astype(BF16)


def _gelu(x):
  return 0.5 * x * (1.0 + lax.erf(x * (2.0 ** -0.5)))


def _log_sigmoid(z):
  return jnp.minimum(z, 0.0) - jnp.log1p(jnp.exp(-jnp.abs(z)))


def _const_spec(shape):
  nd = len(shape)
  return pl.BlockSpec(shape, lambda *_: (0,) * nd, pipeline_mode=pl.Buffered(1))


ADA_STEPS = 8


def _ada_kernel(c_ref, w_ref, b_ref, w13_ref, w2_ref, o_ref, o13_ref, o2_ref):
  c = c_ref[...]
  s = (c * jax.nn.sigmoid(c)).astype(BF16)
  o_ref[...] = _dot(s, w_ref[...].astype(BF16)) + b_ref[...]
  o13_ref[...] = w13_ref[...].astype(BF16)
  o2_ref[...] = w2_ref[...].astype(BF16)


def _ada_call(c_all, w_ada, b_ada, w13, w2):
  rows = c_all.shape[0]
  tn = N_MOD * D // ADA_STEPS
  r13, r2 = D // ADA_STEPS, F // ADA_STEPS
  assert tn % LANE == 0 and r13 % 16 == 0 and r2 % 16 == 0
  blk13 = pl.BlockSpec((r13, 2 * F), lambda i: (i, 0))
  blk2 = pl.BlockSpec((r2, D), lambda i: (i, 0))
  return pl.pallas_call(
      _ada_kernel,
      grid=(ADA_STEPS,),
      in_specs=[
          pl.BlockSpec((rows, D), lambda i: (0, 0)),
          pl.BlockSpec((D, tn), lambda i: (0, i)),
          pl.BlockSpec((1, tn), lambda i: (0, i)),
          blk13, blk2,
      ],
      out_specs=[pl.BlockSpec((rows, tn), lambda i: (0, i)), blk13, blk2],
      out_shape=[jax.ShapeDtypeStruct((rows, N_MOD * D), F32),
                 jax.ShapeDtypeStruct((D, 2 * F), BF16), jax.ShapeDtypeStruct((F, D), BF16)],
      compiler_params=pltpu.CompilerParams(
          dimension_semantics=("arbitrary",), vmem_limit_bytes=VMEM_LIMIT),
      name="ada",
  )(c_all, w_ada, b_ada, w13, w2)


def _ffn_rows(x_ref, sh_ref, sc_ref, g_ref, nw_ref, w13_ref, w2_ref, nf_ref, o_ref):
  x = x_ref[...]
  h = _norm_mod(x, nw_ref[...], sc_ref[...], sh_ref[...])
  a = _dot(h, w13_ref[:, :F])
  b = _dot(h, w13_ref[:, F:])
  p = (a * jax.nn.sigmoid(a) * b).astype(BF16)
  out = x + 0.5 * g_ref[...] * _dot(p, w2_ref[...])
  if nf_ref is not None:
    out = _rms(out, nf_ref[...])
  o_ref[...] = out


def _ffn_kernel(x_ref, sh_ref, sc_ref, g_ref, xs_ref, shs_ref, scs_ref, gs_ref, nw_ref, w13_ref,
                w2_ref, *rest, final_norm, cast):
  rest = list(rest)
  nf_ref = rest.pop(0) if final_norm else None
  cast_in = [rest.pop(0) for _ in range(8)] if cast else None
  o_ref, os_ref = rest[0], rest[1]
  i = pl.program_id(0)
  n = pl.num_programs(0) - 1

  @pl.when(i < n)
  def _():
    _ffn_rows(x_ref, sh_ref, sc_ref, g_ref, nw_ref, w13_ref, w2_ref, nf_ref, o_ref)
    if cast:
      wta_ref, wtalr_ref, wtb_ref, w_pa_ref, w_pb_ref, w_o_ref, n13_ref, n2_ref = cast_in
      wa_ref, walr_ref, wb_ref, pa_ref, pb_ref, po_ref, o13_ref, o2_ref = rest[2:]
      wa_ref[...] = wta_ref[...].T.astype(BF16)
      wb_ref[...] = wtb_ref[...].T.astype(BF16)
      alr = jnp.concatenate([wtalr_ref[...], jnp.zeros((LANE - LOWRANK, D), F32)], axis=0)
      walr_ref[...] = alr.T.astype(BF16)
      pa_ref[...] = w_pa_ref[...].astype(BF16)
      pb_ref[...] = w_pb_ref[...].astype(BF16)
      po_ref[...] = w_o_ref[...].astype(BF16)
      o13_ref[...] = n13_ref[...].astype(BF16)
      o2_ref[...] = n2_ref[...].astype(BF16)

  @pl.when(i == n)
  def _():
    _ffn_rows(xs_ref, shs_ref, scs_ref, gs_ref, nw_ref, w13_ref, w2_ref, nf_ref, os_ref)


CAST_STEPS = 32
W2_CAST_ROWS = 2 * F // CAST_STEPS


def _ffn_call(x, xs, mod_p, mod_s, segs, nw, w13, w2, nf, cast_weights, *, rows_per_batch, tm, name):
  rows, rows_s = x.shape[0], xs.shape[0]
  n = rows // tm
  tiles = rows_per_batch // tm
  last = lambda i: jnp.minimum(i, n - 1)
  x_spec = pl.BlockSpec((tm, D), lambda i: (last(i), 0))
  xs_spec = pl.BlockSpec((rows_s, D), lambda i: (0, 0))
  in_specs = [x_spec]
  in_specs += [pl.BlockSpec((None, 1, D), lambda i, s=s: (last(i) // tiles, 0, s)) for s in segs]
  in_specs += [_const_spec((rows_s, D))]
  in_specs += [pl.BlockSpec((rows_s, D), lambda i, s=s: (0, s), pipeline_mode=pl.Buffered(1))
               for s in segs]
  in_specs += [_const_spec((1, D)), _const_spec((D, 2 * F)), _const_spec((F, D))]
  args = [x, mod_p, mod_p, mod_p, xs, mod_s, mod_s, mod_s, nw, w13, w2]
  if nf is not None:
    in_specs.append(_const_spec((1, D)))
    args.append(nf)
  out_specs = [x_spec, xs_spec]
  out_shape = [jax.ShapeDtypeStruct((rows, D), F32), jax.ShapeDtypeStruct((rows_s, D), F32)]
  if cast_weights is not None:
    assert n == CAST_STEPS and D % CAST_STEPS == 0
    w_t, w_pa, w_pb, w_o, n13, n2 = cast_weights
    lo = 2 * DK + 2 * DV
    n_a, n_b = lo // LANE, (w_t.shape[0] - lo - LOWRANK) // LANE
    assert n_a <= CAST_STEPS and n_b == CAST_STEPS
    rb = D // CAST_STEPS
    step = lambda i: jnp.minimum(i, CAST_STEPS - 1)
    a_blk = lambda i: jnp.minimum(i, n_a - 1)
    elem = lambda r: (pl.Element(r), pl.Element(D))
    row_blk = lambda cols: pl.BlockSpec((rb, cols), lambda i: (step(i), 0))
    w2_blk = pl.BlockSpec((W2_CAST_ROWS, D), lambda i: (step(i) // 2, 0))
    in_specs += [
        pl.BlockSpec(elem(LANE), lambda i: (pl.multiple_of(LANE * a_blk(i), LANE), 0)),
        pl.BlockSpec(elem(LOWRANK), lambda i: (lo, 0)),
        pl.BlockSpec(elem(LANE), lambda i: (pl.multiple_of(lo + LOWRANK + LANE * step(i), LOWRANK), 0)),
        row_blk(D), row_blk(D), row_blk(D), row_blk(2 * F), w2_blk]
    args += [w_t, w_t, w_t, w_pa, w_pb, w_o, n13, n2]
    out_specs += [
        pl.BlockSpec((D, LANE), lambda i: (0, a_blk(i))),
        pl.BlockSpec((D, LANE), lambda i: (0, 0)),
        pl.BlockSpec((D, LANE), lambda i: (0, step(i))),
        row_blk(D), row_blk(D), row_blk(D), row_blk(2 * F), w2_blk]
    bf = lambda r, c: jax.ShapeDtypeStruct((r, c), BF16)
    out_shape += [bf(D, lo), bf(D, LANE), bf(D, n_b * LANE),
                  bf(D, D), bf(D, D), bf(D, D), bf(D, 2 * F), bf(F, D)]
  return pl.pallas_call(
      functools.partial(_ffn_kernel, final_norm=nf is not None, cast=cast_weights is not None),
      grid=(n + 1,),
      in_specs=in_specs,
      out_specs=out_specs,
      out_shape=out_shape,
      compiler_params=pltpu.CompilerParams(
          dimension_semantics=("arbitrary",), vmem_limit_bytes=VMEM_LIMIT),
      name=name,
  )(*args)


def _project_piece(h, w_refs, p_ref, piece):
  idx, lo, hi, dst = piece
  p_ref[:, dst:dst + hi - lo] = _dot(h, w_refs[idx][:, lo:hi])


def _log_decay(p_ref, w_a2_ref, b_a_ref):
  a_lr = p_ref[:, C_ALR:C_ALR + LANE].astype(BF16)
  z = _dot(a_lr, w_a2_ref[...]) + b_a_ref[...]
  return _log_sigmoid(z) * (1.0 / TAU)


def _head_norm_gate(o, p_ref, gnw_ref):
  parts = []
  for hh in range(HEADS):
    sl = slice(hh * HV, (hh + 1) * HV)
    parts.append(_rms(o[:, sl], gnw_ref[:, sl]))
  r = p_ref[:, C_R:C_R + DV]
  return jnp.concatenate(parts, axis=-1) * (r * jax.nn.sigmoid(r))


def _gv_norm(p_ref, lnw_ref, lnb_ref):
  gv = _gelu(p_ref[:, C_GV:C_GV + D])
  mu = jnp.mean(gv, axis=-1, keepdims=True)
  d = gv - mu
  var = jnp.mean(d * d, axis=-1, keepdims=True)
  return d * lax.rsqrt(var + EPS) * lnw_ref[...] + lnb_ref[...]


def _merge_out(x, g2, ya, yb, p_ref, w_o_ref):
  ga = p_ref[:, C_GA:C_GA + D]
  gb = p_ref[:, C_GB:C_GB + D]
  merged = (jax.nn.sigmoid(ga) * ya + jax.nn.sigmoid(gb) * yb).astype(BF16)
  return x + g2 * _dot(merged, w_o_ref[...])


def _mix_prompt_kernel(x_ref, sh_ref, sc_ref, g2_ref, nw_ref,
                       wa_ref, walr_ref, wb_ref, w_a2_ref, b_a_ref,
                       ltri_ref, mask_ref, gnw_ref, w_pa_ref, lnw_ref, lnb_ref, ws_ref, bs_ref,
                       w_pb_ref, w_o_ref, o_ref, s_out_ref,
                       st_ref, p_ref, oa_ref, mx_ref):
  j = pl.program_id(1)
  tm = TM_P
  w_refs = (wa_ref, walr_ref, wb_ref)

  @pl.when(j == 0)
  def _():
    st_ref[...] = jnp.zeros_like(st_ref)

  ltri = ltri_ref[...]
  ti = lax.broadcasted_iota(jnp.int32, (GM_CHUNK, GM_CHUNK), 0)
  si = lax.broadcasted_iota(jnp.int32, (GM_CHUNK, GM_CHUNK), 1)

  def mix_stages():
    oa = oa_ref
    mx = mx_ref
    env = {}

    def gate():
      env["g"] = _log_decay(p_ref, w_a2_ref, b_a_ref) * LOG2E

    def cumsum():
      cums = []
      for c in range(tm // CH):
        gc = env["g"][c * CH:(c + 1) * CH, :]
        g_hi = gc.astype(BF16)
        g_lo = (gc - g_hi.astype(F32)).astype(BF16)
        cums.append(_dot(ltri, g_hi) + _dot(ltri, g_lo))
      env["cums"] = cums

    def prep():
      decays = []
      qk = {}
      for c in range(tm // CH):
        rows = slice(c * CH, (c + 1) * CH)
        b = env["cums"][c]
        e_lvl, e_end = _level_exponents(env["g"][rows, :], b)
        q = (p_ref[rows, C_Q:C_Q + DK] * (HK ** -0.5)).astype(BF16)
        k = p_ref[rows, C_K:C_K + DK].astype(BF16)
        for lvl in range(N_LEVELS):
          e = jnp.exp2(e_lvl[lvl]).astype(BF16)
          qk[c, 2 * lvl] = q * e
          qk[c, 2 * lvl + 1] = k * e
        qk[c, 2 * N_LEVELS] = q
        qk[c, 2 * N_LEVELS + 1] = k
        qk[c, 2 * N_LEVELS + 2] = q * jnp.exp2(b).astype(BF16)
        qk[c, 2 * N_LEVELS + 3] = k * jnp.exp2(e_end).astype(BF16)
        decays.append(jnp.exp2(b[CH - 1:CH, :]))
      env["decays"] = decays
      env["qk"] = qk
      env["gvn"] = _gv_norm(p_ref, lnw_ref, lnb_ref).astype(BF16)
      env["u"] = _gelu(p_ref[:, C_U:C_U + D])
      env["pending"] = None

    def level_scores(c, hh):
      rows = slice(c * CH, (c + 1) * CH)
      ck = slice(hh * HK, (hh + 1) * HK)
      scores = None
      for lvl in range(N_LEVELS + 1):
        part = mask_ref[lvl] * _dot_nt(env["qk"][c, 2 * lvl][:, ck], env["qk"][c, 2 * lvl + 1][:, ck])
        scores = part if scores is None else scores + part
      return scores.astype(BF16)

    def finish(c, hh, scores):
      rows = slice(c * CH, (c + 1) * CH)
      ck = slice(hh * HK, (hh + 1) * HK)
      v32 = p_ref[rows, C_V + hh * HV:C_V + (hh + 1) * HV]
      st = st_ref[hh]
      oa[rows, hh * HV:(hh + 1) * HV] = (
          _dot(scores, v32.astype(BF16))
          + _dot_nt(env["qk"][c, 2 * N_LEVELS + 2][:, ck], st.astype(BF16)))
      st_ref[hh] = (st * env["decays"][c][:, ck]
                    + _dot(v32.T.astype(BF16), env["qk"][c, 2 * N_LEVELS + 3][:, ck]))

    def gla(c):
      def run():
        for hh in range(HEADS):
          scores = level_scores(c, hh)
          if env["pending"] is not None:
            finish(*env["pending"])
          env["pending"] = (c, hh, scores)
      return run

    def gmlp():
      for gg in range(GROUPS):
        ws = jnp.where(si <= ti, ws_ref[gg], 0.0).astype(BF16)
        cols = slice(gg * GC, (gg + 1) * GC)
        for c in range(tm // GM_CHUNK):
          rows = slice(c * GM_CHUNK, (c + 1) * GM_CHUNK)
          mx[rows, cols] = _dot(ws, env["gvn"][rows, cols]) + bs_ref[:, cols]
      finish(*env["pending"])

    def branches():
      env["yb"] = _dot((env["u"] * mx[...]).astype(BF16), w_pb_ref[...])
      env["ya"] = _dot(_head_norm_gate(oa[...], p_ref, gnw_ref).astype(BF16), w_pa_ref[...])

    def out():
      o_ref[...] = _merge_out(x_ref[...], g2_ref[...], env["ya"], env["yb"], p_ref, w_o_ref)

    return gate, cumsum, prep, [gla(c) for c in range(tm // CH)], gmlp, branches, out

  gate, cumsum, prep, glas, gmlp, branches, out = mix_stages()
  h = _norm_mod(x_ref[...], nw_ref[...], sc_ref[...], sh_ref[...])
  proj = lambda piece: _project_piece(h, w_refs, p_ref, piece)
  proj(PROJ_QK)
  proj(PROJ_ALR)
  proj(PROJ_UGV)
  gate()
  proj(PROJ_V)
  cumsum()
  proj(PROJ_R)
  proj(PROJ_GATES)
  prep()
  for gla_chunk in glas:
    gla_chunk()
  gmlp()
  branches()
  out()

  @pl.when(j == pl.num_programs(1) - 1)
  def _():
    for hh in range(HEADS):
      s_out_ref[hh] = st_ref[hh].T


def _mix_prompt_call(x, mod3, nw, w_in3, w_a2, b_a, ltri, masks, gnw, w_pa, lnw, lnb, ws, bs_full,
                     w_pb, w_o, *, batch, seq):
  tm = TM_P
  tiles = seq // tm
  row_spec = pl.BlockSpec((tm, D), lambda b, j: (b * tiles + j, 0))

  def mod_spec(seg):
    return pl.BlockSpec((None, 1, D), lambda b, j: (b, 0, seg))

  wa, walr, wb = w_in3
  in_specs = [
      row_spec, mod_spec(3), mod_spec(4), mod_spec(5),
      _const_spec((1, D)), _const_spec(wa.shape), _const_spec(walr.shape), _const_spec(wb.shape),
      _const_spec((LANE, DK)), _const_spec((1, DK)),
      _const_spec(ltri.shape), _const_spec(masks.shape), _const_spec((1, DV)),
      _const_spec((DV, D)), _const_spec((1, D)), _const_spec((1, D)),
      _const_spec((GROUPS, GM_CHUNK, GM_CHUNK)), _const_spec((GM_CHUNK, D)),
      _const_spec((D, D)), _const_spec((D, D)),
  ]
  out_specs = [
      row_spec,
      pl.BlockSpec((None, None, HEADS, HK, HV), lambda b, j: (0, b, 0, 0, 0)),
  ]
  return pl.pallas_call(
      _mix_prompt_kernel,
      grid=(batch, tiles),
      in_specs=in_specs,
      out_specs=out_specs,
      out_shape=[jax.ShapeDtypeStruct((batch * seq, D), F32),
                 jax.ShapeDtypeStruct((1, batch, HEADS, HK, HV), F32)],
      scratch_shapes=[
          pltpu.VMEM((HEADS, HV, HK), F32),
          pltpu.VMEM((tm, D_P), F32),
          pltpu.VMEM((tm, DV), F32),
          pltpu.VMEM((tm, D), F32),
      ],
      compiler_params=pltpu.CompilerParams(
          dimension_semantics=("arbitrary", "arbitrary"), vmem_limit_bytes=VMEM_LIMIT),
      name="mix_prompt",
  )(x, mod3, mod3, mod3, nw, wa, walr, wb, w_a2, b_a, ltri, masks, gnw, w_pa,
    lnw, lnb, ws, bs_full, w_pb, w_o)


def _mix_sample_kernel(x_ref, sh_ref, sc_ref, g2_ref, nw_ref, wa_ref, walr_ref, wb_ref,
                       w_a2_ref, b_a_ref, gnw_ref, w_pa_ref, lnw_ref, lnb_ref, ws0_ref, bs0_ref,
                       w_pb_ref, w_o_ref, s_ref, o_ref, s_out_ref, gvn_ref,
                       p_ref, oa_ref, bc_ref, kt_ref, v_ref):
  i = pl.program_id(0)
  rows = x_ref.shape[0]

  @pl.when(i == 0)
  def _():
    h = _norm_mod(x_ref[...], nw_ref[...], sc_ref[...], sh_ref[...])
    for piece in PROJ_ALL:
      _project_piece(h, (wa_ref, walr_ref, wb_ref), p_ref, piece)
    g = _log_decay(p_ref, w_a2_ref, b_a_ref)
    q = p_ref[:, C_Q:C_Q + DK] * (HK ** -0.5)
    k = p_ref[:, C_K:C_K + DK]
    eb = jnp.exp(g)
    eb_hi = eb.astype(BF16)
    eb_lo = (eb - eb_hi.astype(F32)).astype(BF16)
    qe = q * eb
    for hh in range(HEADS):
      ck = slice(hh * HK, (hh + 1) * HK)
      bc_ref[hh, 0:HK, :] = eb_hi[:, ck].astype(F32).T.astype(BF16)
      bc_ref[hh, HK:2 * HK, :] = eb_lo[:, ck].astype(F32).T.astype(BF16)
      bc_ref[hh, 2 * HK:3 * HK, :] = qe[:, ck].T.astype(BF16)
      kt_ref[hh] = k[:, ck].T.astype(BF16)
      vh = p_ref[:, C_V + hh * HV:C_V + (hh + 1) * HV]
      v_ref[hh] = vh.astype(BF16)
      qk = jnp.sum(q[:, ck] * k[:, ck], axis=-1, keepdims=True)
      oa_ref[:, hh * HV:(hh + 1) * HV] = qk * vh

  row_id = lax.broadcasted_iota(jnp.int32, (rows, HV), 0)
  for nl in range(NB):
    n = i * NB + nl
    hot = row_id == n
    one_hot = jnp.where(hot, 1.0, 0.0).astype(BF16)
    for hh in range(HEADS):
      bc = _dot(bc_ref[hh], one_hot)
      decay = bc[0:HK] + bc[HK:2 * HK]
      qe_b = bc[2 * HK:3 * HK]
      kv = _dot(kt_ref[hh], jnp.where(hot, v_ref[hh], jnp.zeros((), BF16)))
      st = s_ref[nl, hh]
      o_inter = jnp.sum(qe_b * st, axis=0, keepdims=True)
      cur = oa_ref[pl.ds(n, 1), hh * HV:(hh + 1) * HV]
      oa_ref[pl.ds(n, 1), hh * HV:(hh + 1) * HV] = cur + o_inter
      s_out_ref[nl, hh] = decay * st + kv

  @pl.when(i == pl.num_programs(0) - 1)
  def _():
    ya = _dot(_head_norm_gate(oa_ref[...], p_ref, gnw_ref).astype(BF16), w_pa_ref[...])
    gvn = _gv_norm(p_ref, lnw_ref, lnb_ref)
    gvn_ref[...] = gvn
    mixed = ws0_ref[...] * gvn + bs0_ref[...]
    u = _gelu(p_ref[:, C_U:C_U + D])
    yb = _dot((u * mixed).astype(BF16), w_pb_ref[...])
    o_ref[...] = _merge_out(x_ref[...], g2_ref[...], ya, yb, p_ref, w_o_ref)


def _mix_sample_call(x, mod, nw, w_in3, w_a2, b_a, gnw, w_pa, lnw, lnb, ws0, bs0, w_pb, w_o, state):
  rows = x.shape[0]

  def mod_spec(seg):
    return pl.BlockSpec((rows, D), lambda i: (0, seg), pipeline_mode=pl.Buffered(1))

  wa, walr, wb = w_in3
  state_spec = pl.BlockSpec((None, NB, HEADS, HK, HV), lambda i: (0, i, 0, 0, 0))
  in_specs = [
      _const_spec((rows, D)), mod_spec(3), mod_spec(4), mod_spec(5),
      _const_spec((1, D)), _const_spec(wa.shape), _const_spec(walr.shape), _const_spec(wb.shape),
      _const_spec((LANE, DK)), _const_spec((1, DK)),
      _const_spec((1, DV)), _const_spec((DV, D)), _const_spec((1, D)), _const_spec((1, D)),
      _const_spec((1, D)), _const_spec((1, D)), _const_spec((D, D)), _const_spec((D, D)),
      state_spec,
  ]
  out_specs = [
      pl.BlockSpec((rows, D), lambda i: (0, 0)),
      state_spec,
      pl.BlockSpec((rows, D), lambda i: (0, 0)),
  ]
  return pl.pallas_call(
      _mix_sample_kernel,
      grid=(rows // NB,),
      in_specs=in_specs,
      out_specs=out_specs,
      out_shape=[jax.ShapeDtypeStruct((rows, D), F32),
                 jax.ShapeDtypeStruct(state.shape, F32),
                 jax.ShapeDtypeStruct((rows, D), F32)],
      scratch_shapes=[
          pltpu.VMEM((rows, D_P), F32),
          pltpu.VMEM((rows, DV), F32),
          pltpu.VMEM((HEADS, 3 * HK, rows), BF16),
          pltpu.VMEM((HEADS, HK, rows), BF16),
          pltpu.VMEM((HEADS, rows, HV), BF16),
      ],
      compiler_params=pltpu.CompilerParams(
          dimension_semantics=("arbitrary",), vmem_limit_bytes=VMEM_LIMIT),
      name="mix_sample",
  )(x, mod, mod, mod, nw, wa, walr, wb, w_a2, b_a, gnw, w_pa, lnw, lnb, ws0, bs0, w_pb, w_o, state)


def kernel(x_prompt, x_sample, state_gla, c_prompt, c_sample, w_ada, b_ada, norm1_w, ffn1_w13, ffn1_w2, norm2_w, w_in, w_a2, b_a, gla_norm_w, w_pa, gm_ln_w, gm_ln_b, gm_ws, gm_bs, w_pb, w_o, norm3_w, ffn2_w13, ffn2_w2, normf_w):
  batch, seq, _ = x_prompt.shape
  dec = x_sample.shape[0]
  depth = w_ada.shape[0]
  assert depth == 1 and x_sample.shape[1] == 1 and dec % NB == 0
  assert seq % TM_P == 0 and seq % TM_FFN == 0

  ltri_np, masks_np = _gla_constants()
  ltri = jnp.asarray(ltri_np, BF16)
  masks = jnp.asarray(masks_np, F32)

  xp = x_prompt.reshape(batch * seq, D)
  xs = x_sample.reshape(dec, D)
  l = 0
  mod_all, w13_1, w2_1 = _ada_call(jnp.concatenate([c_sample, c_prompt], axis=0), w_ada[l],
                                   b_ada[l].reshape(1, N_MOD * D), ffn1_w13[l], ffn1_w2[l])
  mod_s = mod_all
  mod_p = mod_all[dec:].reshape(batch, 1, N_MOD * D)

  row = lambda a: a.reshape(1, -1)
  w_a2_p = jnp.pad(w_a2[l], ((0, LANE - LOWRANK), (0, 0))).astype(BF16)
  gnw = gla_norm_w[l].reshape(1, DV)
  bs_full = jnp.repeat(gm_bs[l].T, GC, axis=1)
  ws0 = jnp.repeat(gm_ws[l][:, 0, 0], GC).reshape(1, D)
  bs0 = bs_full[0:1]
  normf = row(normf_w)

  cast_weights = (jnp.transpose(w_in[l]), w_pa[l], w_pb[l], w_o[l], ffn2_w13[l], ffn2_w2[l])
  (xp, xs, wa, walr, wb, w_pa_b, w_pb_b, w_o_b, w13_2, w2_2) = _ffn_call(
      xp, xs, mod_p, mod_s, (0, 1, 2), row(norm1_w[l]), w13_1, w2_1, None, cast_weights,
      rows_per_batch=seq, tm=TM_FFN, name="ffn1")
  w_in3 = (wa, walr, wb)
  xp, s_prompt = _mix_prompt_call(xp, mod_p, row(norm2_w[l]), w_in3, w_a2_p, row(b_a[l]), ltri, masks,
                                  gnw, w_pa_b, row(gm_ln_w[l]), row(gm_ln_b[l]), gm_ws[l], bs_full,
                                  w_pb_b, w_o_b, batch=batch, seq=seq)
  xs, s_sample, gvn = _mix_sample_call(xs, mod_s, row(norm2_w[l]), w_in3, w_a2_p, row(b_a[l]), gnw,
                                       w_pa_b, row(gm_ln_w[l]), row(gm_ln_b[l]), ws0, bs0,
                                       w_pb_b, w_o_b, state_gla)
  yp, ys = _ffn_call(xp, xs, mod_p, mod_s, (6, 7, 8), row(norm3_w[l]), w13_2, w2_2, normf, None,
                     rows_per_batch=seq, tm=TM_FFN, name="ffn2")

  return (yp.reshape(batch, seq, D), ys.reshape(dec, 1, D), s_prompt, s_sample,
          gvn.reshape(1, dec, 1, D))
```

```python
import functools

import numpy as np
import jax
import jax.numpy as jnp
from jax import lax
from jax.experimental import pallas as pl
from jax.experimental.pallas import tpu as pltpu

F32 = jnp.float32
BF16 = jnp.bfloat16

D = 1024
HEADS = 4
HK = 128
HV = 256
DK = HEADS * HK
DV = HEADS * HV
LOWRANK = 16
TAU = 16.0
GROUPS = 4
GC = D // GROUPS
GM_CHUNK = 128
F = 2816
N_MOD = 9
EPS = 1e-6
LOG2E = 1.4426950408889634

CH = 128
N_LEVELS = 7
LANE = 128

C_Q, C_K, C_ALR, C_U, C_GV, C_V, C_R, C_GA, C_GB = 0, 512, 1024, 1152, 2176, 3200, 4224, 5248, 6272
D_P = C_GB + D

PROJ_QK = (0, 0, 2 * DK, C_Q)
PROJ_ALR = (1, 0, LANE, C_ALR)
PROJ_UGV = (2, 0, 2 * D, C_U)
PROJ_V = (0, 2 * DK, 2 * DK + DV, C_V)
PROJ_R = (0, 2 * DK + DV, 2 * DK + 2 * DV, C_R)
PROJ_GATES = (2, 2 * D, 4 * D, C_GA)
PROJ_ALL = (PROJ_QK, PROJ_ALR, PROJ_UGV, PROJ_V, PROJ_R, PROJ_GATES)

TM_P = 512
TM_FFN = 512
NB = 8
VMEM_LIMIT = 58 * 1024 * 1024


def _gla_constants():
  t = np.arange(CH)
  masks = []
  for lvl in range(N_LEVELS):
    m = 1 << lvl
    upper = t >= (t // (2 * m)) * (2 * m) + m
    same_pair = (t[:, None] // (2 * m)) == (t[None, :] // (2 * m))
    masks.append((upper[:, None] & (~upper)[None, :] & same_pair).astype(np.float32))
  masks.append(np.eye(CH, dtype=np.float32))
  ltri = (t[None, :] <= t[:, None]).astype(np.float32)
  return ltri, np.stack(masks)


def _level_exponents(g, b):
  t = lax.broadcasted_iota(jnp.int32, g.shape, 0)
  g_prev = pltpu.roll(g, 1, 0)
  g_next = pltpu.roll(g, CH - 1, 0)
  r4 = t & 3
  levels = [
      jnp.where((t & 1) == 1, g, 0.0),
      jnp.where(r4 == 0, g_next, jnp.where(r4 == 1, 0.0, jnp.where(r4 == 2, g, g + g_prev))),
  ]
  groups = CH // 8
  row = lambda i: jnp.broadcast_to(b[i:i + 1, :], (8, b.shape[1]))
  b3 = [row(8 * j + 3) for j in range(groups)]
  b7 = [row(8 * j + 7) for j in range(groups)]
  levels.append(-jnp.abs(b - jnp.concatenate(b3, axis=0)))
  for lvl in range(3, N_LEVELS):
    m = 1 << lvl
    ref = [b7[((8 * j) // (2 * m)) * (2 * m) // 8 + m // 8 - 1] for j in range(groups)]
    levels.append(-jnp.abs(b - jnp.concatenate(ref, axis=0)))
  to_end = -jnp.abs(b - jnp.concatenate([b7[groups - 1]] * groups, axis=0))
  return levels, to_end


def _dot(a, b):
  return jnp.dot(a, b, preferred_element_type=F32)


def _dot_nt(a, b):
  return lax.dot_general(a, b, (((1,), (1,)), ((), ())), preferred_element_type=F32)


def _rms(x, w):
  ms = jnp.mean(x * x, axis=-1, keepdims=True)
  return x * lax.rsqrt(ms + EPS) * w


def _norm_mod(x, nw, sc, sh):
  return (_rms(x, nw) * (1.0 + sc) + sh).astype(BF16)


def _gelu(x):
  return 0.5 * x * (1.0 + lax.erf(x * (2.0 ** -0.5)))


def _log_sigmoid(z):
  return jnp.minimum(z, 0.0) - jnp.log1p(jnp.exp(-jnp.abs(z)))


def _const_spec(shape):
  nd = len(shape)
  return pl.BlockSpec(shape, lambda *_: (0,) * nd, pipeline_mode=pl.Buffered(1))


ADA_STEPS = 4


def _ada_kernel(c_ref, w_ref, b_ref, w13_ref, w2_ref, o_ref, o13_ref, o2_ref):
  c = c_ref[...]
  s = (c * jax.nn.sigmoid(c)).astype(BF16)
  o_ref[...] = _dot(s, w_ref[...].astype(BF16)) + b_ref[...]
  o13_ref[...] = w13_ref[...].astype(BF16)
  o2_ref[...] = w2_ref[...].astype(BF16)


def _ada_call(c_all, w_ada, b_ada, w13, w2):
  rows = c_all.shape[0]
  tn = N_MOD * D // ADA_STEPS
  r13, r2 = D // ADA_STEPS, F // ADA_STEPS
  assert tn % LANE == 0 and r13 % 16 == 0 and r2 % 16 == 0
  blk13 = pl.BlockSpec((r13, 2 * F), lambda i: (i, 0))
  blk2 = pl.BlockSpec((r2, D), lambda i: (i, 0))
  return pl.pallas_call(
      _ada_kernel,
      grid=(ADA_STEPS,),
      in_specs=[
          pl.BlockSpec((rows, D), lambda i: (0, 0)),
          pl.BlockSpec((D, tn), lambda i: (0, i)),
          pl.BlockSpec((1, tn), lambda i: (0, i)),
          blk13, blk2,
      ],
      out_specs=[pl.BlockSpec((rows, tn), lambda i: (0, i)), blk13, blk2],
      out_shape=[jax.ShapeDtypeStruct((rows, N_MOD * D), F32),
                 jax.ShapeDtypeStruct((D, 2 * F), BF16), jax.ShapeDtypeStruct((F, D), BF16)],
      compiler_params=pltpu.CompilerParams(
          dimension_semantics=("arbitrary",), vmem_limit_bytes=VMEM_LIMIT),
      name="ada",
  )(c_all, w_ada, b_ada, w13, w2)


def _ffn_rows(x_ref, sh_ref, sc_ref, g_ref, nw_ref, w13_ref, w2_ref, nf_ref, o_ref):
  rows = x_ref.shape[0]
  step = min(rows, TM_FFN)
  for r0 in range(0, rows, step):
    sl = slice(r0, r0 + step)
    per_row = sh_ref.shape[0] == rows
    sh, sc, g = (ref[sl, :] if per_row else ref[...] for ref in (sh_ref, sc_ref, g_ref))
    x = x_ref[sl, :]
    h = _norm_mod(x, nw_ref[...], sc, sh)
    a = _dot(h, w13_ref[:, :F])
    b = _dot(h, w13_ref[:, F:])
    p = (a * jax.nn.sigmoid(a) * b).astype(BF16)
    out = x + 0.5 * g * _dot(p, w2_ref[...])
    if nf_ref is not None:
      out = _rms(out, nf_ref[...])
    o_ref[sl, :] = out


def _ffn_kernel(x_ref, sh_ref, sc_ref, g_ref, xs_ref, shs_ref, scs_ref, gs_ref, nw_ref, w13_ref,
                w2_ref, *rest, final_norm, cast):
  rest = list(rest)
  nf_ref = rest.pop(0) if final_norm else None
  cast_in = [rest.pop(0) for _ in range(8)] if cast else None
  o_ref, os_ref = rest[0], rest[1]
  i = pl.program_id(0)
  n = pl.num_programs(0) - 1

  @pl.when(i < n)
  def _():
    _ffn_rows(x_ref, sh_ref, sc_ref, g_ref, nw_ref, w13_ref, w2_ref, nf_ref, o_ref)
    if cast:
      wta_ref, wtalr_ref, wtb_ref, w_pa_ref, w_pb_ref, w_o_ref, n13_ref, n2_ref = cast_in
      wa_ref, walr_ref, wb_ref, pa_ref, pb_ref, po_ref, o13_ref, o2_ref = rest[2:]
      wa_ref[...] = wta_ref[...].T.astype(BF16)
      wb_ref[...] = wtb_ref[...].T.astype(BF16)
      alr = jnp.concatenate([wtalr_ref[...], jnp.zeros((LANE - LOWRANK, D), F32)], axis=0)
      walr_ref[...] = alr.T.astype(BF16)
      pa_ref[...] = w_pa_ref[...].astype(BF16)
      pb_ref[...] = w_pb_ref[...].astype(BF16)
      po_ref[...] = w_o_ref[...].astype(BF16)
      o13_ref[...] = n13_ref[...].astype(BF16)
      o2_ref[...] = n2_ref[...].astype(BF16)

  @pl.when(i == n)
  def _():
    _ffn_rows(xs_ref, shs_ref, scs_ref, gs_ref, nw_ref, w13_ref, w2_ref, nf_ref, os_ref)


CAST_STEPS = 32
W2_CAST_ROWS = 2 * F // CAST_STEPS


def _ffn_call(x, xs, mod_p, mod_s, segs, nw, w13, w2, nf, cast_weights, *, rows_per_batch, tm, name):
  rows, rows_s = x.shape[0], xs.shape[0]
  n = rows // tm
  tiles = rows_per_batch // tm
  last = lambda i: jnp.minimum(i, n - 1)
  x_spec = pl.BlockSpec((tm, D), lambda i: (last(i), 0))
  xs_spec = pl.BlockSpec((rows_s, D), lambda i: (0, 0))
  in_specs = [x_spec]
  in_specs += [pl.BlockSpec((None, 1, D), lambda i, s=s: (last(i) // tiles, 0, s)) for s in segs]
  in_specs += [_const_spec((rows_s, D))]
  in_specs += [pl.BlockSpec((rows_s, D), lambda i, s=s: (0, s), pipeline_mode=pl.Buffered(1))
               for s in segs]
  in_specs += [_const_spec((1, D)), _const_spec((D, 2 * F)), _const_spec((F, D))]
  args = [x, mod_p, mod_p, mod_p, xs, mod_s, mod_s, mod_s, nw, w13, w2]
  if nf is not None:
    in_specs.append(_const_spec((1, D)))
    args.append(nf)
  out_specs = [x_spec, xs_spec]
  out_shape = [jax.ShapeDtypeStruct((rows, D), F32), jax.ShapeDtypeStruct((rows_s, D), F32)]
  if cast_weights is not None:
    assert n == CAST_STEPS and D % CAST_STEPS == 0
    w_t, w_pa, w_pb, w_o, n13, n2 = cast_weights
    lo = 2 * DK + 2 * DV
    n_a, n_b = lo // LANE, (w_t.shape[0] - lo - LOWRANK) // LANE
    assert n_a <= CAST_STEPS and n_b == CAST_STEPS
    rb = D // CAST_STEPS
    step = lambda i: jnp.minimum(i, CAST_STEPS - 1)
    a_blk = lambda i: jnp.minimum(i, n_a - 1)
    elem = lambda r: (pl.Element(r), pl.Element(D))
    row_blk = lambda cols: pl.BlockSpec((rb, cols), lambda i: (step(i), 0))
    w2_blk = pl.BlockSpec((W2_CAST_ROWS, D), lambda i: (step(i) // 2, 0))
    in_specs += [
        pl.BlockSpec(elem(LANE), lambda i: (pl.multiple_of(LANE * a_blk(i), LANE), 0)),
        pl.BlockSpec(elem(LOWRANK), lambda i: (lo, 0)),
        pl.BlockSpec(elem(LANE), lambda i: (pl.multiple_of(lo + LOWRANK + LANE * step(i), LOWRANK), 0)),
        row_blk(D), row_blk(D), row_blk(D), row_blk(2 * F), w2_blk]
    args += [w_t, w_t, w_t, w_pa, w_pb, w_o, n13, n2]
    out_specs += [
        pl.BlockSpec((D, LANE), lambda i: (0, a_blk(i))),
        pl.BlockSpec((D, LANE), lambda i: (0, 0)),
        pl.BlockSpec((D, LANE), lambda i: (0, step(i))),
        row_blk(D), row_blk(D), row_blk(D), row_blk(2 * F), w2_blk]
    bf = lambda r, c: jax.ShapeDtypeStruct((r, c), BF16)
    out_shape += [bf(D, lo), bf(D, LANE), bf(D, n_b * LANE),
                  bf(D, D), bf(D, D), bf(D, D), bf(D, 2 * F), bf(F, D)]
  return pl.pallas_call(
      functools.partial(_ffn_kernel, final_norm=nf is not None, cast=cast_weights is not None),
      grid=(n + 1,),
      in_specs=in_specs,
      out_specs=out_specs,
      out_shape=out_shape,
      compiler_params=pltpu.CompilerParams(
          dimension_semantics=("arbitrary",), vmem_limit_bytes=VMEM_LIMIT),
      name=name,
  )(*args)


def _project_piece(h, w_refs, p_ref, piece):
  idx, lo, hi, dst = piece
  p_ref[:, dst:dst + hi - lo] = _dot(h, w_refs[idx][:, lo:hi])


def _log_decay(p_ref, w_a2_ref, b_a_ref):
  a_lr = p_ref[:, C_ALR:C_ALR + LANE].astype(BF16)
  z = _dot(a_lr, w_a2_ref[...]) + b_a_ref[...]
  return _log_sigmoid(z) * (1.0 / TAU)


def _head_norm_gate(o, p_ref, gnw_ref):
  parts = []
  for hh in range(HEADS):
    sl = slice(hh * HV, (hh + 1) * HV)
    parts.append(_rms(o[:, sl], gnw_ref[:, sl]))
  r = p_ref[:, C_R:C_R + DV]
  return jnp.concatenate(parts, axis=-1) * (r * jax.nn.sigmoid(r))


def _gv_norm(p_ref, lnw_ref, lnb_ref):
  gv = _gelu(p_ref[:, C_GV:C_GV + D])
  mu = jnp.mean(gv, axis=-1, keepdims=True)
  d = gv - mu
  var = jnp.mean(d * d, axis=-1, keepdims=True)
  return d * lax.rsqrt(var + EPS) * lnw_ref[...] + lnb_ref[...]


def _merge_out(x, g2, ya, yb, p_ref, w_o_ref):
  ga = p_ref[:, C_GA:C_GA + D]
  gb = p_ref[:, C_GB:C_GB + D]
  merged = (jax.nn.sigmoid(ga) * ya + jax.nn.sigmoid(gb) * yb).astype(BF16)
  return x + g2 * _dot(merged, w_o_ref[...])


def _mix_prompt_kernel(x_ref, sh_ref, sc_ref, g2_ref, nw_ref,
                       wa_ref, walr_ref, wb_ref, w_a2_ref, b_a_ref,
                       ltri_ref, mask_ref, gnw_ref, w_pa_ref, lnw_ref, lnb_ref, ws_ref, bs_ref,
                       w_pb_ref, w_o_ref, o_ref, s_out_ref,
                       st_ref, p_ref, oa_ref, mx_ref):
  j = pl.program_id(1)
  tm = TM_P
  w_refs = (wa_ref, walr_ref, wb_ref)

  @pl.when(j == 0)
  def _():
    st_ref[...] = jnp.zeros_like(st_ref)

  ltri = ltri_ref[...]
  ti = lax.broadcasted_iota(jnp.int32, (GM_CHUNK, GM_CHUNK), 0)
  si = lax.broadcasted_iota(jnp.int32, (GM_CHUNK, GM_CHUNK), 1)

  def mix_stages():
    oa = oa_ref
    mx = mx_ref
    env = {}

    def gate():
      env["g"] = _log_decay(p_ref, w_a2_ref, b_a_ref) * LOG2E

    def cumsum():
      cums = []
      for c in range(tm // CH):
        gc = env["g"][c * CH:(c + 1) * CH, :]
        g_hi = gc.astype(BF16)
        g_lo = (gc - g_hi.astype(F32)).astype(BF16)
        cums.append(_dot(ltri, g_hi) + _dot(ltri, g_lo))
      env["cums"] = cums

    def prep():
      decays = []
      qk = {}
      for c in range(tm // CH):
        rows = slice(c * CH, (c + 1) * CH)
        b = env["cums"][c]
        e_lvl, e_end = _level_exponents(env["g"][rows, :], b)
        q = (p_ref[rows, C_Q:C_Q + DK] * (HK ** -0.5)).astype(BF16)
        k = p_ref[rows, C_K:C_K + DK].astype(BF16)
        for lvl in range(N_LEVELS):
          e = jnp.exp2(e_lvl[lvl]).astype(BF16)
          qk[c, 2 * lvl] = q * e
          qk[c, 2 * lvl + 1] = k * e
        qk[c, 2 * N_LEVELS] = q
        qk[c, 2 * N_LEVELS + 1] = k
        qk[c, 2 * N_LEVELS + 2] = q * jnp.exp2(b).astype(BF16)
        qk[c, 2 * N_LEVELS + 3] = k * jnp.exp2(e_end).astype(BF16)
        decays.append(jnp.exp2(b[CH - 1:CH, :]))
      env["decays"] = decays
      env["qk"] = qk
      env["gvn"] = _gv_norm(p_ref, lnw_ref, lnb_ref).astype(BF16)
      env["u"] = _gelu(p_ref[:, C_U:C_U + D])
      env["pending"] = None

    def level_scores(c, hh):
      rows = slice(c * CH, (c + 1) * CH)
      ck = slice(hh * HK, (hh + 1) * HK)
      scores = None
      for lvl in range(N_LEVELS + 1):
        part = mask_ref[lvl] * _dot_nt(env["qk"][c, 2 * lvl][:, ck], env["qk"][c, 2 * lvl + 1][:, ck])
        scores = part if scores is None else scores + part
      return scores.astype(BF16)

    def finish(c, hh, scores):
      rows = slice(c * CH, (c + 1) * CH)
      ck = slice(hh * HK, (hh + 1) * HK)
      v32 = p_ref[rows, C_V + hh * HV:C_V + (hh + 1) * HV]
      st = st_ref[hh]
      oa[rows, hh * HV:(hh + 1) * HV] = (
          _dot(scores, v32.astype(BF16))
          + _dot_nt(env["qk"][c, 2 * N_LEVELS + 2][:, ck], st.astype(BF16)))
      st_ref[hh] = (st * env["decays"][c][:, ck]
                    + _dot(v32.T.astype(BF16), env["qk"][c, 2 * N_LEVELS + 3][:, ck]))

    def gla(c):
      def run():
        for hh in range(HEADS):
          scores = level_scores(c, hh)
          if env["pending"] is not None:
            finish(*env["pending"])
          env["pending"] = (c, hh, scores)
      return run

    def gmlp():
      for gg in range(GROUPS):
        ws = jnp.where(si <= ti, ws_ref[gg], 0.0).astype(BF16)
        cols = slice(gg * GC, (gg + 1) * GC)
        for c in range(tm // GM_CHUNK):
          rows = slice(c * GM_CHUNK, (c + 1) * GM_CHUNK)
          mx[rows, cols] = _dot(ws, env["gvn"][rows, cols]) + bs_ref[:, cols]
      finish(*env["pending"])

    def branches():
      env["yb"] = _dot((env["u"] * mx[...]).astype(BF16), w_pb_ref[...])
      env["ya"] = _dot(_head_norm_gate(oa[...], p_ref, gnw_ref).astype(BF16), w_pa_ref[...])

    def out():
      o_ref[...] = _merge_out(x_ref[...], g2_ref[...], env["ya"], env["yb"], p_ref, w_o_ref)

    return gate, cumsum, prep, [gla(c) for c in range(tm // CH)], gmlp, branches, out

  gate, cumsum, prep, glas, gmlp, branches, out = mix_stages()
  h = _norm_mod(x_ref[...], nw_ref[...], sc_ref[...], sh_ref[...])
  proj = lambda piece: _project_piece(h, w_refs, p_ref, piece)
  proj(PROJ_QK)
  proj(PROJ_ALR)
  proj(PROJ_UGV)
  gate()
  proj(PROJ_V)
  cumsum()
  proj(PROJ_R)
  proj(PROJ_GATES)
  prep()
  for gla_chunk in glas:
    gla_chunk()
  gmlp()
  branches()
  out()

  @pl.when(j == pl.num_programs(1) - 1)
  def _():
    for hh in range(HEADS):
      s_out_ref[hh] = st_ref[hh].T


def _mix_prompt_call(x, mod3, nw, w_in3, w_a2, b_a, ltri, masks, gnw, w_pa, lnw, lnb, ws, bs_full,
                     w_pb, w_o, *, batch, seq):
  tm = TM_P
  tiles = seq // tm
  row_spec = pl.BlockSpec((tm, D), lambda b, j: (b * tiles + j, 0))

  def mod_spec(seg):
    return pl.BlockSpec((None, 1, D), lambda b, j: (b, 0, seg))

  wa, walr, wb = w_in3
  in_specs = [
      row_spec, mod_spec(3), mod_spec(4), mod_spec(5),
      _const_spec((1, D)), _const_spec(wa.shape), _const_spec(walr.shape), _const_spec(wb.shape),
      _const_spec((LANE, DK)), _const_spec((1, DK)),
      _const_spec(ltri.shape), _const_spec(masks.shape), _const_spec((1, DV)),
      _const_spec((DV, D)), _const_spec((1, D)), _const_spec((1, D)),
      _const_spec((GROUPS, GM_CHUNK, GM_CHUNK)), _const_spec((GM_CHUNK, D)),
      _const_spec((D, D)), _const_spec((D, D)),
  ]
  out_specs = [
      row_spec,
      pl.BlockSpec((None, None, HEADS, HK, HV), lambda b, j: (0, b, 0, 0, 0)),
  ]
  return pl.pallas_call(
      _mix_prompt_kernel,
      grid=(batch, tiles),
      in_specs=in_specs,
      out_specs=out_specs,
      out_shape=[jax.ShapeDtypeStruct((batch * seq, D), F32),
                 jax.ShapeDtypeStruct((1, batch, HEADS, HK, HV), F32)],
      scratch_shapes=[
          pltpu.VMEM((HEADS, HV, HK), F32),
          pltpu.VMEM((tm, D_P), F32),
          pltpu.VMEM((tm, DV), F32),
          pltpu.VMEM((tm, D), F32),
      ],
      compiler_params=pltpu.CompilerParams(
          dimension_semantics=("arbitrary", "arbitrary"), vmem_limit_bytes=VMEM_LIMIT),
      name="mix_prompt",
  )(x, mod3, mod3, mod3, nw, wa, walr, wb, w_a2, b_a, ltri, masks, gnw, w_pa,
    lnw, lnb, ws, bs_full, w_pb, w_o)


def _mix_sample_kernel(x_ref, sh_ref, sc_ref, g2_ref, nw_ref, wa_ref, walr_ref, wb_ref,
                       w_a2_ref, b_a_ref, gnw_ref, w_pa_ref, lnw_ref, lnb_ref, ws0_ref, bs0_ref,
                       w_pb_ref, w_o_ref, s_ref, o_ref, s_out_ref, gvn_ref,
                       p_ref, oa_ref, bc_ref, kt_ref, v_ref):
  i = pl.program_id(0)
  rows = x_ref.shape[0]

  @pl.when(i == 0)
  def _():
    h = _norm_mod(x_ref[...], nw_ref[...], sc_ref[...], sh_ref[...])
    for piece in PROJ_ALL:
      _project_piece(h, (wa_ref, walr_ref, wb_ref), p_ref, piece)
    g = _log_decay(p_ref, w_a2_ref, b_a_ref)
    q = p_ref[:, C_Q:C_Q + DK] * (HK ** -0.5)
    k = p_ref[:, C_K:C_K + DK]
    eb = jnp.exp(g)
    eb_hi = eb.astype(BF16)
    eb_lo = (eb - eb_hi.astype(F32)).astype(BF16)
    qe = q * eb
    for hh in range(HEADS):
      ck = slice(hh * HK, (hh + 1) * HK)
      bc_ref[hh, 0:HK, :] = eb_hi[:, ck].astype(F32).T.astype(BF16)
      bc_ref[hh, HK:2 * HK, :] = eb_lo[:, ck].astype(F32).T.astype(BF16)
      bc_ref[hh, 2 * HK:3 * HK, :] = qe[:, ck].T.astype(BF16)
      kt_ref[hh] = k[:, ck].T.astype(BF16)
      vh = p_ref[:, C_V + hh * HV:C_V + (hh + 1) * HV]
      v_ref[hh] = vh.astype(BF16)
      qk = jnp.sum(q[:, ck] * k[:, ck], axis=-1, keepdims=True)
      oa_ref[:, hh * HV:(hh + 1) * HV] = qk * vh

  row_id = lax.broadcasted_iota(jnp.int32, (rows, HV), 0)
  for nl in range(NB):
    n = i * NB + nl
    hot = row_id == n
    one_hot = jnp.where(hot, 1.0, 0.0).astype(BF16)
    for hh in range(HEADS):
      bc = _dot(bc_ref[hh], one_hot)
      decay = bc[0:HK] + bc[HK:2 * HK]
      qe_b = bc[2 * HK:3 * HK]
      kv = _dot(kt_ref[hh], jnp.where(hot, v_ref[hh], jnp.zeros((), BF16)))
      st = s_ref[nl, hh]
      o_inter = jnp.sum(qe_b * st, axis=0, keepdims=True)
      cur = oa_ref[pl.ds(n, 1), hh * HV:(hh + 1) * HV]
      oa_ref[pl.ds(n, 1), hh * HV:(hh + 1) * HV] = cur + o_inter
      s_out_ref[nl, hh] = decay * st + kv

  @pl.when(i == pl.num_programs(0) - 1)
  def _():
    ya = _dot(_head_norm_gate(oa_ref[...], p_ref, gnw_ref).astype(BF16), w_pa_ref[...])
    gvn = _gv_norm(p_ref, lnw_ref, lnb_ref)
    gvn_ref[...] = gvn
    mixed = ws0_ref[...] * gvn + bs0_ref[...]
    u = _gelu(p_ref[:, C_U:C_U + D])
    yb = _dot((u * mixed).astype(BF16), w_pb_ref[...])
    o_ref[...] = _merge_out(x_ref[...], g2_ref[...], ya, yb, p_ref, w_o_ref)


def _mix_sample_call(x, mod, nw, w_in3, w_a2, b_a, gnw, w_pa, lnw, lnb, ws0, bs0, w_pb, w_o, state):
  rows = x.shape[0]

  def mod_spec(seg):
    return pl.BlockSpec((rows, D), lambda i: (0, seg), pipeline_mode=pl.Buffered(1))

  wa, walr, wb = w_in3
  state_spec = pl.BlockSpec((None, NB, HEADS, HK, HV), lambda i: (0, i, 0, 0, 0))
  in_specs = [
      _const_spec((rows, D)), mod_spec(3), mod_spec(4), mod_spec(5),
      _const_spec((1, D)), _const_spec(wa.shape), _const_spec(walr.shape), _const_spec(wb.shape),
      _const_spec((LANE, DK)), _const_spec((1, DK)),
      _const_spec((1, DV)), _const_spec((DV, D)), _const_spec((1, D)), _const_spec((1, D)),
      _const_spec((1, D)), _const_spec((1, D)), _const_spec((D, D)), _const_spec((D, D)),
      state_spec,
  ]
  out_specs = [
      pl.BlockSpec((rows, D), lambda i: (0, 0)),
      state_spec,
      pl.BlockSpec((rows, D), lambda i: (0, 0)),
  ]
  return pl.pallas_call(
      _mix_sample_kernel,
      grid=(rows // NB,),
      in_specs=in_specs,
      out_specs=out_specs,
      out_shape=[jax.ShapeDtypeStruct((rows, D), F32),
                 jax.ShapeDtypeStruct(state.shape, F32),
                 jax.ShapeDtypeStruct((rows, D), F32)],
      scratch_shapes=[
          pltpu.VMEM((rows, D_P), F32),
          pltpu.VMEM((rows, DV), F32),
          pltpu.VMEM((HEADS, 3 * HK, rows), BF16),
          pltpu.VMEM((HEADS, HK, rows), BF16),
          pltpu.VMEM((HEADS, rows, HV), BF16),
      ],
      compiler_params=pltpu.CompilerParams(
          dimension_semantics=("arbitrary",), vmem_limit_bytes=VMEM_LIMIT),
      name="mix_sample",
  )(x, mod, mod, mod, nw, wa, walr, wb, w_a2, b_a, gnw, w_pa, lnw, lnb, ws0, bs0, w_pb, w_o, state)


def kernel(x_prompt, x_sample, state_gla, c_prompt, c_sample, w_ada, b_ada, norm1_w, ffn1_w13, ffn1_w2, norm2_w, w_in, w_a2, b_a, gla_norm_w, w_pa, gm_ln_w, gm_ln_b, gm_ws, gm_bs, w_pb, w_o, norm3_w, ffn2_w13, ffn2_w2, normf_w):
  batch, seq, _ = x_prompt.shape
  dec = x_sample.shape[0]
  depth = w_ada.shape[0]
  assert depth == 1 and x_sample.shape[1] == 1 and dec % NB == 0
  assert seq % TM_P == 0 and seq % (2 * TM_FFN) == 0

  ltri_np, masks_np = _gla_constants()
  ltri = jnp.asarray(ltri_np, BF16)
  masks = jnp.asarray(masks_np, F32)

  xp = x_prompt.reshape(batch * seq, D)
  xs = x_sample.reshape(dec, D)
  l = 0
  mod_all, w13_1, w2_1 = _ada_call(jnp.concatenate([c_sample, c_prompt], axis=0), w_ada[l],
                                   b_ada[l].reshape(1, N_MOD * D), ffn1_w13[l], ffn1_w2[l])
  mod_s = mod_all
  mod_p = mod_all[dec:].reshape(batch, 1, N_MOD * D)

  row = lambda a: a.reshape(1, -1)
  w_a2_p = jnp.pad(w_a2[l], ((0, LANE - LOWRANK), (0, 0))).astype(BF16)
  gnw = gla_norm_w[l].reshape(1, DV)
  bs_full = jnp.repeat(gm_bs[l].T, GC, axis=1)
  ws0 = jnp.repeat(gm_ws[l][:, 0, 0], GC).reshape(1, D)
  bs0 = bs_full[0:1]
  normf = row(normf_w)

  cast_weights = (jnp.transpose(w_in[l]), w_pa[l], w_pb[l], w_o[l], ffn2_w13[l], ffn2_w2[l])
  (xp, xs, wa, walr, wb, w_pa_b, w_pb_b, w_o_b, w13_2, w2_2) = _ffn_call(
      xp, xs, mod_p, mod_s, (0, 1, 2), row(norm1_w[l]), w13_1, w2_1, None, cast_weights,
      rows_per_batch=seq, tm=TM_FFN, name="ffn1")
  w_in3 = (wa, walr, wb)
  xp, s_prompt = _mix_prompt_call(xp, mod_p, row(norm2_w[l]), w_in3, w_a2_p, row(b_a[l]), ltri, masks,
                                  gnw, w_pa_b, row(gm_ln_w[l]), row(gm_ln_b[l]), gm_ws[l], bs_full,
                                  w_pb_b, w_o_b, batch=batch, seq=seq)
  xs, s_sample, gvn = _mix_sample_call(xs, mod_s, row(norm2_w[l]), w_in3, w_a2_p, row(b_a[l]), gnw,
                                       w_pa_b, row(gm_ln_w[l]), row(gm_ln_b[l]), ws0, bs0,
                                       w_pb_b, w_o_b, state_gla)
  yp, ys = _ffn_call(xp, xs, mod_p, mod_s, (6, 7, 8), row(norm3_w[l]), w13_2, w2_2, normf, None,
                     rows_per_batch=seq, tm=2 * TM_FFN, name="ffn2")

  return (yp.reshape(batch, seq, D), ys.reshape(dec, 1, D), s_prompt, s_sample,
          gvn.reshape(1, dec, 1, D))
```

```python
import functools

import numpy as np
import jax
import jax.numpy as jnp
from jax import lax
from jax.experimental import pallas as pl
from jax.experimental.pallas import tpu as pltpu

F32 = jnp.float32
BF16 = jnp.bfloat16

D = 1024
HEADS = 4
HK = 128
HV = 256
DK = HEADS * HK
DV = HEADS * HV
LOWRANK = 16
TAU = 16.0
GROUPS = 4
GC = D // GROUPS
GM_CHUNK = 128
F = 2816
N_MOD = 9
EPS = 1e-6
LOG2E = 1.4426950408889634

CH = 128
N_LEVELS = 7
LANE = 128

C_Q, C_K, C_ALR, C_U, C_GV, C_V, C_R, C_GA, C_GB = 0, 512, 1024, 1152, 2176, 3200, 4224, 5248, 6272
D_P = C_GB + D

PROJ_QK = (0, 0, 2 * DK, C_Q)
PROJ_ALR = (1, 0, LANE, C_ALR)
PROJ_UGV = (2, 0, 2 * D, C_U)
PROJ_V = (0, 2 * DK, 2 * DK + DV, C_V)
PROJ_R = (0, 2 * DK + DV, 2 * DK + 2 * DV, C_R)
PROJ_GATES = (2, 2 * D, 4 * D, C_GA)
PROJ_ALL = (PROJ_QK, PROJ_ALR, PROJ_UGV, PROJ_V, PROJ_R, PROJ_GATES)

TM_P = 512
TM_FFN = 512
FFN_PRE = 1024
FFN_PRE_EARLY = 256
NB = 8
VMEM_LIMIT = 58 * 1024 * 1024


def _gla_constants():
  t = np.arange(CH)
  masks = []
  for lvl in range(N_LEVELS):
    m = 1 << lvl
    upper = t >= (t // (2 * m)) * (2 * m) + m
    same_pair = (t[:, None] // (2 * m)) == (t[None, :] // (2 * m))
    masks.append((upper[:, None] & (~upper)[None, :] & same_pair).astype(np.float32))
  masks.append(np.eye(CH, dtype=np.float32))
  ltri = (t[None, :] <= t[:, None]).astype(np.float32)
  return ltri, np.stack(masks)


def _level_exponents(g, b):
  t = lax.broadcasted_iota(jnp.int32, g.shape, 0)
  g_prev = pltpu.roll(g, 1, 0)
  g_next = pltpu.roll(g, CH - 1, 0)
  r4 = t & 3
  levels = [
      jnp.where((t & 1) == 1, g, 0.0),
      jnp.where(r4 == 0, g_next, jnp.where(r4 == 1, 0.0, jnp.where(r4 == 2, g, g + g_prev))),
  ]
  groups = CH // 8
  row = lambda i: jnp.broadcast_to(b[i:i + 1, :], (8, b.shape[1]))
  b3 = [row(8 * j + 3) for j in range(groups)]
  b7 = [row(8 * j + 7) for j in range(groups)]
  levels.append(-jnp.abs(b - jnp.concatenate(b3, axis=0)))
  for lvl in range(3, N_LEVELS):
    m = 1 << lvl
    ref = [b7[((8 * j) // (2 * m)) * (2 * m) // 8 + m // 8 - 1] for j in range(groups)]
    levels.append(-jnp.abs(b - jnp.concatenate(ref, axis=0)))
  to_end = -jnp.abs(b - jnp.concatenate([b7[groups - 1]] * groups, axis=0))
  return levels, to_end


def _dot(a, b):
  return jnp.dot(a, b, preferred_element_type=F32)


def _dot_nt(a, b):
  return lax.dot_general(a, b, (((1,), (1,)), ((), ())), preferred_element_type=F32)


def _rms(x, w):
  ms = jnp.mean(x * x, axis=-1, keepdims=True)
  return x * lax.rsqrt(ms + EPS) * w


def _norm_mod(x, nw, sc, sh):
  return (_rms(x, nw) * (1.0 + sc) + sh).astype(BF16)


def _gelu(x):
  return 0.5 * x * (1.0 + lax.erf(x * (2.0 ** -0.5)))


def _log_sigmoid(z):
  return jnp.minimum(z, 0.0) - jnp.log1p(jnp.exp(-jnp.abs(z)))


def _const_spec(shape):
  nd = len(shape)
  return pl.BlockSpec(shape, lambda *_: (0,) * nd, pipeline_mode=pl.Buffered(1))


ADA_STEPS = 4


def _ada_kernel(c_ref, w_ref, b_ref, w13_ref, w2_ref, o_ref, o13_ref, o2_ref):
  c = c_ref[...]
  s = (c * jax.nn.sigmoid(c)).astype(BF16)
  o_ref[...] = _dot(s, w_ref[...].astype(BF16)) + b_ref[...]
  o13_ref[...] = w13_ref[...].astype(BF16)
  o2_ref[...] = w2_ref[...].astype(BF16)


def _ada_call(c_all, w_ada, b_ada, w13, w2):
  rows = c_all.shape[0]
  tn = N_MOD * D // ADA_STEPS
  r13, r2 = D // ADA_STEPS, F // ADA_STEPS
  assert tn % LANE == 0 and r13 % 16 == 0 and r2 % 16 == 0
  blk13 = pl.BlockSpec((r13, 2 * F), lambda i: (i, 0))
  blk2 = pl.BlockSpec((r2, D), lambda i: (i, 0))
  return pl.pallas_call(
      _ada_kernel,
      grid=(ADA_STEPS,),
      in_specs=[
          pl.BlockSpec((rows, D), lambda i: (0, 0)),
          pl.BlockSpec((D, tn), lambda i: (0, i)),
          pl.BlockSpec((1, tn), lambda i: (0, i)),
          blk13, blk2,
      ],
      out_specs=[pl.BlockSpec((rows, tn), lambda i: (0, i)), blk13, blk2],
      out_shape=[jax.ShapeDtypeStruct((rows, N_MOD * D), F32),
                 jax.ShapeDtypeStruct((D, 2 * F), BF16), jax.ShapeDtypeStruct((F, D), BF16)],
      compiler_params=pltpu.CompilerParams(
          dimension_semantics=("arbitrary",), vmem_limit_bytes=VMEM_LIMIT),
      name="ada",
  )(c_all, w_ada, b_ada, w13, w2)


def _ffn_rows(h, x_ref, g_ref, w13_ref, w2_ref, nf_ref, o_ref):
  a = _dot(h, w13_ref[:, :F])
  b = _dot(h, w13_ref[:, F:])
  p = (a * jax.nn.sigmoid(a) * b).astype(BF16)
  out = x_ref[...] + 0.5 * g_ref[...] * _dot(p, w2_ref[...])
  if nf_ref is not None:
    out = _rms(out, nf_ref[...])
  o_ref[...] = out


def _ffn_kernel(x_ref, sh_ref, sc_ref, g_ref, xn_ref, shn_ref, scn_ref, xs_ref, shs_ref, scs_ref,
                gs_ref, nw_ref, w13_ref, w2_ref, *rest, final_norm, cast):
  rest = list(rest)
  nf_ref = rest.pop(0) if final_norm else None
  cast_in = [rest.pop(0) for _ in range(8)] if cast else None
  o_ref, os_ref, h_ref, pre_ref = rest[0], rest[1], rest[-2], rest[-1]
  i = pl.program_id(0)
  n = pl.num_programs(0) - 1
  nw = nw_ref[...]

  @pl.when(i == 0)
  def _():
    h0 = _norm_mod(x_ref[...], nw, sc_ref[...], sh_ref[...])
    h_ref[0] = h0
    pre_ref[0] = _dot(h0, w13_ref[:, :FFN_PRE])

  def prompt_step(cur, nxt):
    h = h_ref[cur]
    hn = _norm_mod(xn_ref[...], nw, scn_ref[...], shn_ref[...])
    h_ref[nxt] = hn
    half = FFN_PRE_EARLY
    a_rest = _dot(h, w13_ref[:, FFN_PRE:F])
    pre_ref[nxt, :, :half] = _dot(hn, w13_ref[:, :half])
    b = _dot(h, w13_ref[:, F:])
    a = jnp.concatenate([pre_ref[cur], a_rest], axis=1)
    p = (a * jax.nn.sigmoid(a) * b).astype(BF16)
    m = _dot(p, w2_ref[...])
    pre_ref[nxt, :, half:] = _dot(hn, w13_ref[:, half:FFN_PRE])
    out = x_ref[...] + 0.5 * g_ref[...] * m
    if nf_ref is not None:
      out = _rms(out, nf_ref[...])
    o_ref[...] = out
    if cast:
      wta_ref, wtalr_ref, wtb_ref, w_pa_ref, w_pb_ref, w_o_ref, n13_ref, n2_ref = cast_in
      wa_ref, walr_ref, wb_ref, pa_ref, pb_ref, po_ref, o13_ref, o2_ref = rest[2:-2]
      wa_ref[...] = wta_ref[...].T.astype(BF16)
      wb_ref[...] = wtb_ref[...].T.astype(BF16)
      alr = jnp.concatenate([wtalr_ref[...], jnp.zeros((LANE - LOWRANK, D), F32)], axis=0)
      walr_ref[...] = alr.T.astype(BF16)
      pa_ref[...] = w_pa_ref[...].astype(BF16)
      pb_ref[...] = w_pb_ref[...].astype(BF16)
      po_ref[...] = w_o_ref[...].astype(BF16)
      o13_ref[...] = n13_ref[...].astype(BF16)
      o2_ref[...] = n2_ref[...].astype(BF16)

  @pl.when((i < n) & (i % 2 == 0))
  def _():
    prompt_step(0, 1)

  @pl.when((i < n) & (i % 2 == 1))
  def _():
    prompt_step(1, 0)

  @pl.when(i == n)
  def _():
    hs = _norm_mod(xs_ref[...], nw, scs_ref[...], shs_ref[...])
    _ffn_rows(hs, xs_ref, gs_ref, w13_ref, w2_ref, nf_ref, os_ref)


CAST_STEPS = 32
W2_CAST_ROWS = 2 * F // CAST_STEPS


def _ffn_call(x, xs, mod_p, mod_s, segs, nw, w13, w2, nf, cast_weights, *, rows_per_batch, tm, name):
  rows, rows_s = x.shape[0], xs.shape[0]
  n = rows // tm
  tiles = rows_per_batch // tm
  last = lambda i: jnp.minimum(i, n - 1)
  x_spec = pl.BlockSpec((tm, D), lambda i: (last(i), 0))
  xs_spec = pl.BlockSpec((rows_s, D), lambda i: (0, 0))
  nxt = lambda i: jnp.minimum(i + 1, n - 1)
  in_specs = [x_spec]
  in_specs += [pl.BlockSpec((None, 1, D), lambda i, s=s: (last(i) // tiles, 0, s)) for s in segs]
  in_specs += [pl.BlockSpec((tm, D), lambda i: (nxt(i), 0))]
  in_specs += [pl.BlockSpec((None, 1, D), lambda i, s=s: (nxt(i) // tiles, 0, s)) for s in segs[:2]]
  in_specs += [_const_spec((rows_s, D))]
  in_specs += [pl.BlockSpec((rows_s, D), lambda i, s=s: (0, s), pipeline_mode=pl.Buffered(1))
               for s in segs]
  in_specs += [_const_spec((1, D)), _const_spec((D, 2 * F)), _const_spec((F, D))]
  args = [x, mod_p, mod_p, mod_p, x, mod_p, mod_p, xs, mod_s, mod_s, mod_s, nw, w13, w2]
  if nf is not None:
    in_specs.append(_const_spec((1, D)))
    args.append(nf)
  out_specs = [x_spec, xs_spec]
  out_shape = [jax.ShapeDtypeStruct((rows, D), F32), jax.ShapeDtypeStruct((rows_s, D), F32)]
  if cast_weights is not None:
    assert n == CAST_STEPS and D % CAST_STEPS == 0
    w_t, w_pa, w_pb, w_o, n13, n2 = cast_weights
    lo = 2 * DK + 2 * DV
    n_a, n_b = lo // LANE, (w_t.shape[0] - lo - LOWRANK) // LANE
    assert n_a <= CAST_STEPS and n_b == CAST_STEPS
    rb = D // CAST_STEPS
    step = lambda i: jnp.minimum(i, CAST_STEPS - 1)
    a_blk = lambda i: jnp.minimum(i, n_a - 1)
    elem = lambda r: (pl.Element(r), pl.Element(D))
    row_blk = lambda cols: pl.BlockSpec((rb, cols), lambda i: (step(i), 0))
    w2_blk = pl.BlockSpec((W2_CAST_ROWS, D), lambda i: (step(i) // 2, 0))
    in_specs += [
        pl.BlockSpec(elem(LANE), lambda i: (pl.multiple_of(LANE * a_blk(i), LANE), 0)),
        pl.BlockSpec(elem(LOWRANK), lambda i: (lo, 0)),
        pl.BlockSpec(elem(LANE), lambda i: (pl.multiple_of(lo + LOWRANK + LANE * step(i), LOWRANK), 0)),
        row_blk(D), row_blk(D), row_blk(D), row_blk(2 * F), w2_blk]
    args += [w_t, w_t, w_t, w_pa, w_pb, w_o, n13, n2]
    out_specs += [
        pl.BlockSpec((D, LANE), lambda i: (0, a_blk(i))),
        pl.BlockSpec((D, LANE), lambda i: (0, 0)),
        pl.BlockSpec((D, LANE), lambda i: (0, step(i))),
        row_blk(D), row_blk(D), row_blk(D), row_blk(2 * F), w2_blk]
    bf = lambda r, c: jax.ShapeDtypeStruct((r, c), BF16)
    out_shape += [bf(D, lo), bf(D, LANE), bf(D, n_b * LANE),
                  bf(D, D), bf(D, D), bf(D, D), bf(D, 2 * F), bf(F, D)]
  return pl.pallas_call(
      functools.partial(_ffn_kernel, final_norm=nf is not None, cast=cast_weights is not None),
      grid=(n + 1,),
      in_specs=in_specs,
      out_specs=out_specs,
      out_shape=out_shape,
      scratch_shapes=[pltpu.VMEM((2, tm, D), BF16), pltpu.VMEM((2, tm, FFN_PRE), F32)],
      compiler_params=pltpu.CompilerParams(
          dimension_semantics=("arbitrary",), vmem_limit_bytes=VMEM_LIMIT),
      name=name,
  )(*args)


def _project_piece(h, w_refs, p_ref, piece):
  idx, lo, hi, dst = piece
  p_ref[:, dst:dst + hi - lo] = _dot(h, w_refs[idx][:, lo:hi])


def _log_decay(p_ref, w_a2_ref, b_a_ref):
  a_lr = p_ref[:, C_ALR:C_ALR + LANE].astype(BF16)
  z = _dot(a_lr, w_a2_ref[...]) + b_a_ref[...]
  return _log_sigmoid(z) * (1.0 / TAU)


def _head_norm_gate(o, p_ref, gnw_ref):
  parts = []
  for hh in range(HEADS):
    sl = slice(hh * HV, (hh + 1) * HV)
    parts.append(_rms(o[:, sl], gnw_ref[:, sl]))
  r = p_ref[:, C_R:C_R + DV]
  return jnp.concatenate(parts, axis=-1) * (r * jax.nn.sigmoid(r))


def _gv_norm(p_ref, lnw_ref, lnb_ref):
  gv = _gelu(p_ref[:, C_GV:C_GV + D])
  mu = jnp.mean(gv, axis=-1, keepdims=True)
  d = gv - mu
  var = jnp.mean(d * d, axis=-1, keepdims=True)
  return d * lax.rsqrt(var + EPS) * lnw_ref[...] + lnb_ref[...]


def _merge_out(x, g2, ya, yb, p_ref, w_o_ref):
  ga = p_ref[:, C_GA:C_GA + D]
  gb = p_ref[:, C_GB:C_GB + D]
  merged = (jax.nn.sigmoid(ga) * ya + jax.nn.sigmoid(gb) * yb).astype(BF16)
  return x + g2 * _dot(merged, w_o_ref[...])


def _mix_prompt_kernel(x_ref, sh_ref, sc_ref, g2_ref, nw_ref,
                       wa_ref, walr_ref, wb_ref, w_a2_ref, b_a_ref,
                       ltri_ref, mask_ref, gnw_ref, w_pa_ref, lnw_ref, lnb_ref, ws_ref, bs_ref,
                       w_pb_ref, w_o_ref, o_ref, s_out_ref,
                       st_ref, p_ref, oa_ref, mx_ref):
  j = pl.program_id(1)
  tm = TM_P
  w_refs = (wa_ref, walr_ref, wb_ref)

  @pl.when(j == 0)
  def _():
    st_ref[...] = jnp.zeros_like(st_ref)

  ltri = ltri_ref[...]
  ti = lax.broadcasted_iota(jnp.int32, (GM_CHUNK, GM_CHUNK), 0)
  si = lax.broadcasted_iota(jnp.int32, (GM_CHUNK, GM_CHUNK), 1)

  def mix_stages():
    oa = oa_ref
    mx = mx_ref
    env = {}

    def gate():
      env["g"] = _log_decay(p_ref, w_a2_ref, b_a_ref) * LOG2E

    def cumsum():
      cums = []
      for c in range(tm // CH):
        gc = env["g"][c * CH:(c + 1) * CH, :]
        g_hi = gc.astype(BF16)
        g_lo = (gc - g_hi.astype(F32)).astype(BF16)
        cums.append(_dot(ltri, g_hi) + _dot(ltri, g_lo))
      env["cums"] = cums

    def prep():
      decays = []
      qk = {}
      for c in range(tm // CH):
        rows = slice(c * CH, (c + 1) * CH)
        b = env["cums"][c]
        e_lvl, e_end = _level_exponents(env["g"][rows, :], b)
        q = (p_ref[rows, C_Q:C_Q + DK] * (HK ** -0.5)).astype(BF16)
        k = p_ref[rows, C_K:C_K + DK].astype(BF16)
        for lvl in range(N_LEVELS):
          e = jnp.exp2(e_lvl[lvl]).astype(BF16)
          qk[c, 2 * lvl] = q * e
          qk[c, 2 * lvl + 1] = k * e
        qk[c, 2 * N_LEVELS] = q
        qk[c, 2 * N_LEVELS + 1] = k
        qk[c, 2 * N_LEVELS + 2] = q * jnp.exp2(b).astype(BF16)
        qk[c, 2 * N_LEVELS + 3] = k * jnp.exp2(e_end).astype(BF16)
        decays.append(jnp.exp2(b[CH - 1:CH, :]))
      env["decays"] = decays
      env["qk"] = qk
      env["gvn"] = _gv_norm(p_ref, lnw_ref, lnb_ref).astype(BF16)
      env["u"] = _gelu(p_ref[:, C_U:C_U + D])
      env["pending"] = None

    def level_scores(c, hh):
      rows = slice(c * CH, (c + 1) * CH)
      ck = slice(hh * HK, (hh + 1) * HK)
      scores = None
      for lvl in range(N_LEVELS + 1):
        part = mask_ref[lvl] * _dot_nt(env["qk"][c, 2 * lvl][:, ck], env["qk"][c, 2 * lvl + 1][:, ck])
        scores = part if scores is None else scores + part
      return scores.astype(BF16)

    def finish(c, hh, scores):
      rows = slice(c * CH, (c + 1) * CH)
      ck = slice(hh * HK, (hh + 1) * HK)
      v32 = p_ref[rows, C_V + hh * HV:C_V + (hh + 1) * HV]
      st = st_ref[hh]
      oa[rows, hh * HV:(hh + 1) * HV] = (
          _dot(scores, v32.astype(BF16))
          + _dot_nt(env["qk"][c, 2 * N_LEVELS + 2][:, ck], st.astype(BF16)))
      st_ref[hh] = (st * env["decays"][c][:, ck]
                    + _dot(v32.T.astype(BF16), env["qk"][c, 2 * N_LEVELS + 3][:, ck]))

    def gla(c):
      def run():
        for hh in range(HEADS):
          scores = level_scores(c, hh)
          if env["pending"] is not None:
            finish(*env["pending"])
          env["pending"] = (c, hh, scores)
      return run

    def gmlp():
      for gg in range(GROUPS):
        ws = jnp.where(si <= ti, ws_ref[gg], 0.0).astype(BF16)
        cols = slice(gg * GC, (gg + 1) * GC)
        for c in range(tm // GM_CHUNK):
          rows = slice(c * GM_CHUNK, (c + 1) * GM_CHUNK)
          mx[rows, cols] = _dot(ws, env["gvn"][rows, cols]) + bs_ref[:, cols]
      finish(*env["pending"])

    def branches():
      env["yb"] = _dot((env["u"] * mx[...]).astype(BF16), w_pb_ref[...])
      env["ya"] = _dot(_head_norm_gate(oa[...], p_ref, gnw_ref).astype(BF16), w_pa_ref[...])

    def out():
      o_ref[...] = _merge_out(x_ref[...], g2_ref[...], env["ya"], env["yb"], p_ref, w_o_ref)

    return gate, cumsum, prep, [gla(c) for c in range(tm // CH)], gmlp, branches, out

  gate, cumsum, prep, glas, gmlp, branches, out = mix_stages()
  h = _norm_mod(x_ref[...], nw_ref[...], sc_ref[...], sh_ref[...])
  proj = lambda piece: _project_piece(h, w_refs, p_ref, piece)
  proj(PROJ_QK)
  proj(PROJ_ALR)
  proj(PROJ_UGV)
  gate()
  proj(PROJ_V)
  cumsum()
  proj(PROJ_R)
  proj(PROJ_GATES)
  prep()
  for gla_chunk in glas:
    gla_chunk()
  gmlp()
  branches()
  out()

  @pl.when(j == pl.num_programs(1) - 1)
  def _():
    for hh in range(HEADS):
      s_out_ref[hh] = st_ref[hh].T


def _mix_prompt_call(x, mod3, nw, w_in3, w_a2, b_a, ltri, masks, gnw, w_pa, lnw, lnb, ws, bs_full,
                     w_pb, w_o, *, batch, seq):
  tm = TM_P
  tiles = seq // tm
  row_spec = pl.BlockSpec((tm, D), lambda b, j: (b * tiles + j, 0))

  def mod_spec(seg):
    return pl.BlockSpec((None, 1, D), lambda b, j: (b, 0, seg))

  wa, walr, wb = w_in3
  in_specs = [
      row_spec, mod_spec(3), mod_spec(4), mod_spec(5),
      _const_spec((1, D)), _const_spec(wa.shape), _const_spec(walr.shape), _const_spec(wb.shape),
      _const_spec((LANE, DK)), _const_spec((1, DK)),
      _const_spec(ltri.shape), _const_spec(masks.shape), _const_spec((1, DV)),
      _const_spec((DV, D)), _const_spec((1, D)), _const_spec((1, D)),
      _const_spec((GROUPS, GM_CHUNK, GM_CHUNK)), _const_spec((GM_CHUNK, D)),
      _const_spec((D, D)), _const_spec((D, D)),
  ]
  out_specs = [
      row_spec,
      pl.BlockSpec((None, None, HEADS, HK, HV), lambda b, j: (0, b, 0, 0, 0)),
  ]
  return pl.pallas_call(
      _mix_prompt_kernel,
      grid=(batch, tiles),
      in_specs=in_specs,
      out_specs=out_specs,
      out_shape=[jax.ShapeDtypeStruct((batch * seq, D), F32),
                 jax.ShapeDtypeStruct((1, batch, HEADS, HK, HV), F32)],
      scratch_shapes=[
          pltpu.VMEM((HEADS, HV, HK), F32),
          pltpu.VMEM((tm, D_P), F32),
          pltpu.VMEM((tm, DV), F32),
          pltpu.VMEM((tm, D), F32),
      ],
      compiler_params=pltpu.CompilerParams(
          dimension_semantics=("arbitrary", "arbitrary"), vmem_limit_bytes=VMEM_LIMIT),
      name="mix_prompt",
  )(x, mod3, mod3, mod3, nw, wa, walr, wb, w_a2, b_a, ltri, masks, gnw, w_pa,
    lnw, lnb, ws, bs_full, w_pb, w_o)


def _mix_sample_kernel(x_ref, sh_ref, sc_ref, g2_ref, nw_ref, wa_ref, walr_ref, wb_ref,
                       w_a2_ref, b_a_ref, gnw_ref, w_pa_ref, lnw_ref, lnb_ref, ws0_ref, bs0_ref,
                       w_pb_ref, w_o_ref, s_ref, o_ref, s_out_ref, gvn_ref,
                       p_ref, oa_ref, bc_ref, kt_ref, v_ref):
  i = pl.program_id(0)
  rows = x_ref.shape[0]

  @pl.when(i == 0)
  def _():
    h = _norm_mod(x_ref[...], nw_ref[...], sc_ref[...], sh_ref[...])
    for piece in PROJ_ALL:
      _project_piece(h, (wa_ref, walr_ref, wb_ref), p_ref, piece)
    g = _log_decay(p_ref, w_a2_ref, b_a_ref)
    q = p_ref[:, C_Q:C_Q + DK] * (HK ** -0.5)
    k = p_ref[:, C_K:C_K + DK]
    eb = jnp.exp(g)
    eb_hi = eb.astype(BF16)
    eb_lo = (eb - eb_hi.astype(F32)).astype(BF16)
    qe = q * eb
    for hh in range(HEADS):
      ck = slice(hh * HK, (hh + 1) * HK)
      bc_ref[hh, 0:HK, :] = eb_hi[:, ck].astype(F32).T.astype(BF16)
      bc_ref[hh, HK:2 * HK, :] = eb_lo[:, ck].astype(F32).T.astype(BF16)
      bc_ref[hh, 2 * HK:3 * HK, :] = qe[:, ck].T.astype(BF16)
      kt_ref[hh] = k[:, ck].T.astype(BF16)
      vh = p_ref[:, C_V + hh * HV:C_V + (hh + 1) * HV]
      v_ref[hh] = vh.astype(BF16)
      qk = jnp.sum(q[:, ck] * k[:, ck], axis=-1, keepdims=True)
      oa_ref[:, hh * HV:(hh + 1) * HV] = qk * vh

  row_id = lax.broadcasted_iota(jnp.int32, (rows, HV), 0)
  for nl in range(NB):
    n = i * NB + nl
    hot = row_id == n
    one_hot = jnp.where(hot, 1.0, 0.0).astype(BF16)
    for hh in range(HEADS):
      bc = _dot(bc_ref[hh], one_hot)
      decay = bc[0:HK] + bc[HK:2 * HK]
      qe_b = bc[2 * HK:3 * HK]
      kv = _dot(kt_ref[hh], jnp.where(hot, v_ref[hh], jnp.zeros((), BF16)))
      st = s_ref[nl, hh]
      o_inter = jnp.sum(qe_b * st, axis=0, keepdims=True)
      cur = oa_ref[pl.ds(n, 1), hh * HV:(hh + 1) * HV]
      oa_ref[pl.ds(n, 1), hh * HV:(hh + 1) * HV] = cur + o_inter
      s_out_ref[nl, hh] = decay * st + kv

  @pl.when(i == pl.num_programs(0) - 1)
  def _():
    ya = _dot(_head_norm_gate(oa_ref[...], p_ref, gnw_ref).astype(BF16), w_pa_ref[...])
    gvn = _gv_norm(p_ref, lnw_ref, lnb_ref)
    gvn_ref[...] = gvn
    mixed = ws0_ref[...] * gvn + bs0_ref[...]
    u = _gelu(p_ref[:, C_U:C_U + D])
    yb = _dot((u * mixed).astype(BF16), w_pb_ref[...])
    o_ref[...] = _merge_out(x_ref[...], g2_ref[...], ya, yb, p_ref, w_o_ref)


def _mix_sample_call(x, mod, nw, w_in3, w_a2, b_a, gnw, w_pa, lnw, lnb, ws0, bs0, w_pb, w_o, state):
  rows = x.shape[0]

  def mod_spec(seg):
    return pl.BlockSpec((rows, D), lambda i: (0, seg), pipeline_mode=pl.Buffered(1))

  wa, walr, wb = w_in3
  state_spec = pl.BlockSpec((None, NB, HEADS, HK, HV), lambda i: (0, i, 0, 0, 0))
  in_specs = [
      _const_spec((rows, D)), mod_spec(3), mod_spec(4), mod_spec(5),
      _const_spec((1, D)), _const_spec(wa.shape), _const_spec(walr.shape), _const_spec(wb.shape),
      _const_spec((LANE, DK)), _const_spec((1, DK)),
      _const_spec((1, DV)), _const_spec((DV, D)), _const_spec((1, D)), _const_spec((1, D)),
      _const_spec((1, D)), _const_spec((1, D)), _const_spec((D, D)), _const_spec((D, D)),
      state_spec,
  ]
  out_specs = [
      pl.BlockSpec((rows, D), lambda i: (0, 0)),
      state_spec,
      pl.BlockSpec((rows, D), lambda i: (0, 0)),
  ]
  return pl.pallas_call(
      _mix_sample_kernel,
      grid=(rows // NB,),
      in_specs=in_specs,
      out_specs=out_specs,
      out_shape=[jax.ShapeDtypeStruct((rows, D), F32),
                 jax.ShapeDtypeStruct(state.shape, F32),
                 jax.ShapeDtypeStruct((rows, D), F32)],
      scratch_shapes=[
          pltpu.VMEM((rows, D_P), F32),
          pltpu.VMEM((rows, DV), F32),
          pltpu.VMEM((HEADS, 3 * HK, rows), BF16),
          pltpu.VMEM((HEADS, HK, rows), BF16),
          pltpu.VMEM((HEADS, rows, HV), BF16),
      ],
      compiler_params=pltpu.CompilerParams(
          dimension_semantics=("arbitrary",), vmem_limit_bytes=VMEM_LIMIT),
      name="mix_sample",
  )(x, mod, mod, mod, nw, wa, walr, wb, w_a2, b_a, gnw, w_pa, lnw, lnb, ws0, bs0, w_pb, w_o, state)


def kernel(x_prompt, x_sample, state_gla, c_prompt, c_sample, w_ada, b_ada, norm1_w, ffn1_w13, ffn1_w2, norm2_w, w_in, w_a2, b_a, gla_norm_w, w_pa, gm_ln_w, gm_ln_b, gm_ws, gm_bs, w_pb, w_o, norm3_w, ffn2_w13, ffn2_w2, normf_w):
  batch, seq, _ = x_prompt.shape
  dec = x_sample.shape[0]
  depth = w_ada.shape[0]
  assert depth == 1 and x_sample.shape[1] == 1 and dec % NB == 0
  assert seq % TM_P == 0 and seq % (2 * TM_FFN) == 0

  ltri_np, masks_np = _gla_constants()
  ltri = jnp.asarray(ltri_np, BF16)
  masks = jnp.asarray(masks_np, F32)

  xp = x_prompt.reshape(batch * seq, D)
  xs = x_sample.reshape(dec, D)
  l = 0
  mod_all, w13_1, w2_1 = _ada_call(jnp.concatenate([c_sample, c_prompt], axis=0), w_ada[l],
                                   b_ada[l].reshape(1, N_MOD * D), ffn1_w13[l], ffn1_w2[l])
  mod_s = mod_all
  mod_p = mod_all[dec:].reshape(batch, 1, N_MOD * D)

  row = lambda a: a.reshape(1, -1)
  w_a2_p = jnp.pad(w_a2[l], ((0, LANE - LOWRANK), (0, 0))).astype(BF16)
  gnw = gla_norm_w[l].reshape(1, DV)
  bs_full = jnp.repeat(gm_bs[l].T, GC, axis=1)
  ws0 = jnp.repeat(gm_ws[l][:, 0, 0], GC).reshape(1, D)
  bs0 = bs_full[0:1]
  normf = row(normf_w)

  cast_weights = (jnp.transpose(w_in[l]), w_pa[l], w_pb[l], w_o[l], ffn2_w13[l], ffn2_w2[l])
  (xp, xs, wa, walr, wb, w_pa_b, w_pb_b, w_o_b, w13_2, w2_2) = _ffn_call(
      xp, xs, mod_p, mod_s, (0, 1, 2), row(norm1_w[l]), w13_1, w2_1, None, cast_weights,
      rows_per_batch=seq, tm=TM_FFN, name="ffn1")
  w_in3 = (wa, walr, wb)
  xp, s_prompt = _mix_prompt_call(xp, mod_p, row(norm2_w[l]), w_in3, w_a2_p, row(b_a[l]), ltri, masks,
                                  gnw, w_pa_b, row(gm_ln_w[l]), row(gm_ln_b[l]), gm_ws[l], bs_full,
                                  w_pb_b, w_o_b, batch=batch, seq=seq)
  xs, s_sample, gvn = _mix_sample_call(xs, mod_s, row(norm2_w[l]), w_in3, w_a2_p, row(b_a[l]), gnw,
                                       w_pa_b, row(gm_ln_w[l]), row(gm_ln_b[l]), ws0, bs0,
                                       w_pb_b, w_o_b, state_gla)
  yp, ys = _ffn_call(xp, xs, mod_p, mod_s, (6, 7, 8), row(norm3_w[l]), w13_2, w2_2, normf, None,
                     rows_per_batch=seq, tm=TM_FFN, name="ffn2")

  return (yp.reshape(batch, seq, D), ys.reshape(dec, 1, D), s_prompt, s_sample,
          gvn.reshape(1, dec, 1, D))
```

```python
import functools

import numpy as np
import jax
import jax.numpy as jnp
from jax import lax
from jax.experimental import pallas as pl
from jax.experimental.pallas import tpu as pltpu

F32 = jnp.float32
BF16 = jnp.bfloat16

D = 1024
HEADS = 4
HK = 128
HV = 256
DK = HEADS * HK
DV = HEADS * HV
LOWRANK = 16
TAU = 16.0
GROUPS = 4
GC = D // GROUPS
GM_CHUNK = 128
F = 2816
N_MOD = 9
EPS = 1e-6
LOG2E = 1.4426950408889634

CH = 128
N_LEVELS = 7
LANE = 128

C_Q, C_K, C_ALR, C_U, C_GV, C_V, C_R, C_GA, C_GB = 0, 512, 1024, 1152, 2176, 3200, 4224, 5248, 6272
D_P = C_GB + D

PROJ_QK = (0, 0, 2 * DK, C_Q)
PROJ_ALR = (1, 0, LANE, C_ALR)
PROJ_UGV = (2, 0, 2 * D, C_U)
PROJ_V = (0, 2 * DK, 2 * DK + DV, C_V)
PROJ_R = (0, 2 * DK + DV, 2 * DK + 2 * DV, C_R)
PROJ_GATES = (2, 2 * D, 4 * D, C_GA)
PROJ_ALL = (PROJ_QK, PROJ_ALR, PROJ_UGV, PROJ_V, PROJ_R, PROJ_GATES)

TM_P = 512
TM_FFN = 512
NB = 8
VMEM_LIMIT = 58 * 1024 * 1024


def _gla_constants():
  t = np.arange(CH)
  masks = []
  for lvl in range(N_LEVELS):
    m = 1 << lvl
    upper = t >= (t // (2 * m)) * (2 * m) + m
    same_pair = (t[:, None] // (2 * m)) == (t[None, :] // (2 * m))
    masks.append((upper[:, None] & (~upper)[None, :] & same_pair).astype(np.float32))
  masks.append(np.eye(CH, dtype=np.float32))
  ltri = (t[None, :] <= t[:, None]).astype(np.float32)
  return ltri, np.stack(masks)


def _level_exponents(g, b):
  t = lax.broadcasted_iota(jnp.int32, g.shape, 0)
  g_prev = pltpu.roll(g, 1, 0)
  g_next = pltpu.roll(g, CH - 1, 0)
  r4 = t & 3
  levels = [
      jnp.where((t & 1) == 1, g, 0.0),
      jnp.where(r4 == 0, g_next, jnp.where(r4 == 1, 0.0, jnp.where(r4 == 2, g, g + g_prev))),
  ]
  groups = CH // 8
  row = lambda i: jnp.broadcast_to(b[i:i + 1, :], (8, b.shape[1]))
  b3 = [row(8 * j + 3) for j in range(groups)]
  b7 = [row(8 * j + 7) for j in range(groups)]
  levels.append(-jnp.abs(b - jnp.concatenate(b3, axis=0)))
  for lvl in range(3, N_LEVELS):
    m = 1 << lvl
    ref = [b7[((8 * j) // (2 * m)) * (2 * m) // 8 + m // 8 - 1] for j in range(groups)]
    levels.append(-jnp.abs(b - jnp.concatenate(ref, axis=0)))
  to_end = -jnp.abs(b - jnp.concatenate([b7[groups - 1]] * groups, axis=0))
  return levels, to_end


def _dot(a, b):
  return jnp.dot(a, b, preferred_element_type=F32)


def _dot_nt(a, b):
  return lax.dot_general(a, b, (((1,), (1,)), ((), ())), preferred_element_type=F32)


def _rms(x, w):
  ms = jnp.mean(x * x, axis=-1, keepdims=True)
  return x * lax.rsqrt(ms + EPS) * w


def _norm_mod(x, nw, sc, sh):
  return (_rms(x, nw) * (1.0 + sc) + sh).astype(BF16)


def _gelu(x):
  return 0.5 * x * (1.0 + lax.erf(x * (2.0 ** -0.5)))


def _log_sigmoid(z):
  return jnp.minimum(z, 0.0) - jnp.log1p(jnp.exp(-jnp.abs(z)))


def _const_spec(shape):
  nd = len(shape)
  return pl.BlockSpec(shape, lambda *_: (0,) * nd, pipeline_mode=pl.Buffered(1))


ADA_STEPS = 4


def _ada_kernel(c_ref, w_ref, b_ref, w13_ref, w2_ref, o_ref, o13_ref, o2_ref):
  c = c_ref[...]
  s = (c * jax.nn.sigmoid(c)).astype(BF16)
  o_ref[...] = _dot(s, w_ref[...].astype(BF16)) + b_ref[...]
  o13_ref[...] = w13_ref[...].astype(BF16)
  o2_ref[...] = w2_ref[...].astype(BF16)


def _ada_call(c_all, w_ada, b_ada, w13, w2):
  rows = c_all.shape[0]
  tn = N_MOD * D // ADA_STEPS
  r13, r2 = D // ADA_STEPS, F // ADA_STEPS
  assert tn % LANE == 0 and r13 % 16 == 0 and r2 % 16 == 0
  blk13 = pl.BlockSpec((r13, 2 * F), lambda i: (i, 0))
  blk2 = pl.BlockSpec((r2, D), lambda i: (i, 0))
  return pl.pallas_call(
      _ada_kernel,
      grid=(ADA_STEPS,),
      in_specs=[
          pl.BlockSpec((rows, D), lambda i: (0, 0)),
          pl.BlockSpec((D, tn), lambda i: (0, i)),
          pl.BlockSpec((1, tn), lambda i: (0, i)),
          blk13, blk2,
      ],
      out_specs=[pl.BlockSpec((rows, tn), lambda i: (0, i)), blk13, blk2],
      out_shape=[jax.ShapeDtypeStruct((rows, N_MOD * D), F32),
                 jax.ShapeDtypeStruct((D, 2 * F), BF16), jax.ShapeDtypeStruct((F, D), BF16)],
      compiler_params=pltpu.CompilerParams(
          dimension_semantics=("arbitrary",), vmem_limit_bytes=VMEM_LIMIT),
      name="ada",
  )(c_all, w_ada, b_ada, w13, w2)


def _ffn_rows(x_ref, sh_ref, sc_ref, g_ref, nw_ref, w13_ref, w2_ref, nf_ref, o_ref):
  rows = x_ref.shape[0]
  step = min(rows, TM_FFN)
  for r0 in range(0, rows, step):
    sl = slice(r0, r0 + step)
    per_row = sh_ref.shape[0] == rows
    sh, sc, g = (ref[sl, :] if per_row else ref[...] for ref in (sh_ref, sc_ref, g_ref))
    x = x_ref[sl, :]
    h = _norm_mod(x, nw_ref[...], sc, sh)
    a = _dot(h, w13_ref[:, :F])
    b = _dot(h, w13_ref[:, F:])
    p = (a * jax.nn.sigmoid(a) * b).astype(BF16)
    out = x + 0.5 * g * _dot(p, w2_ref[...])
    if nf_ref is not None:
      out = _rms(out, nf_ref[...])
    o_ref[sl, :] = out


def _ffn_kernel(x_ref, sh_ref, sc_ref, g_ref, xs_ref, shs_ref, scs_ref, gs_ref, nw_ref, w13_ref,
                w2_ref, *rest, final_norm, cast):
  rest = list(rest)
  nf_ref = rest.pop(0) if final_norm else None
  cast_in = [rest.pop(0) for _ in range(8)] if cast else None
  o_ref, os_ref = rest[0], rest[1]
  i = pl.program_id(0)
  n = pl.num_programs(0) - 1

  @pl.when(i < n)
  def _():
    _ffn_rows(x_ref, sh_ref, sc_ref, g_ref, nw_ref, w13_ref, w2_ref, nf_ref, o_ref)
    if cast:
      wta_ref, wtalr_ref, wtb_ref, w_pa_ref, w_pb_ref, w_o_ref, n13_ref, n2_ref = cast_in
      wa_ref, walr_ref, wb_ref, pa_ref, pb_ref, po_ref, o13_ref, o2_ref = rest[2:]
      wa_ref[...] = wta_ref[...].T.astype(BF16)
      wb_ref[...] = wtb_ref[...].T.astype(BF16)
      alr = jnp.concatenate([wtalr_ref[...], jnp.zeros((LANE - LOWRANK, D), F32)], axis=0)
      walr_ref[...] = alr.T.astype(BF16)
      pa_ref[...] = w_pa_ref[...].astype(BF16)
      pb_ref[...] = w_pb_ref[...].astype(BF16)
      po_ref[...] = w_o_ref[...].astype(BF16)
      o13_ref[...] = n13_ref[...].astype(BF16)
      o2_ref[...] = n2_ref[...].astype(BF16)

  @pl.when(i == n)
  def _():
    _ffn_rows(xs_ref, shs_ref, scs_ref, gs_ref, nw_ref, w13_ref, w2_ref, nf_ref, os_ref)


CAST_STEPS = 32
W2_CAST_ROWS = 2 * F // CAST_STEPS


def _ffn_call(x, xs, mod_p, mod_s, segs, nw, w13, w2, nf, cast_weights, *, rows_per_batch, tm, name):
  rows, rows_s = x.shape[0], xs.shape[0]
  n = rows // tm
  tiles = rows_per_batch // tm
  last = lambda i: jnp.minimum(i, n - 1)
  x_spec = pl.BlockSpec((tm, D), lambda i: (last(i), 0))
  xs_spec = pl.BlockSpec((rows_s, D), lambda i: (0, 0))
  in_specs = [x_spec]
  in_specs += [pl.BlockSpec((None, 1, D), lambda i, s=s: (last(i) // tiles, 0, s)) for s in segs]
  in_specs += [_const_spec((rows_s, D))]
  in_specs += [pl.BlockSpec((rows_s, D), lambda i, s=s: (0, s), pipeline_mode=pl.Buffered(1))
               for s in segs]
  in_specs += [_const_spec((1, D)), _const_spec((D, 2 * F)), _const_spec((F, D))]
  args = [x, mod_p, mod_p, mod_p, xs, mod_s, mod_s, mod_s, nw, w13, w2]
  if nf is not None:
    in_specs.append(_const_spec((1, D)))
    args.append(nf)
  out_specs = [x_spec, xs_spec]
  out_shape = [jax.ShapeDtypeStruct((rows, D), F32), jax.ShapeDtypeStruct((rows_s, D), F32)]
  if cast_weights is not None:
    assert n == CAST_STEPS and D % CAST_STEPS == 0
    w_t, w_pa, w_pb, w_o, n13, n2 = cast_weights
    lo = 2 * DK + 2 * DV
    n_a, n_b = lo // LANE, (w_t.shape[0] - lo - LOWRANK) // LANE
    assert n_a <= CAST_STEPS and n_b == CAST_STEPS
    rb = D // CAST_STEPS
    step = lambda i: jnp.minimum(i, CAST_STEPS - 1)
    a_blk = lambda i: jnp.minimum(i, n_a - 1)
    elem = lambda r: (pl.Element(r), pl.Element(D))
    row_blk = lambda cols: pl.BlockSpec((rb, cols), lambda i: (step(i), 0))
    w2_blk = pl.BlockSpec((W2_CAST_ROWS, D), lambda i: (step(i) // 2, 0))
    in_specs += [
        pl.BlockSpec(elem(LANE), lambda i: (pl.multiple_of(LANE * a_blk(i), LANE), 0)),
        pl.BlockSpec(elem(LOWRANK), lambda i: (lo, 0)),
        pl.BlockSpec(elem(LANE), lambda i: (pl.multiple_of(lo + LOWRANK + LANE * step(i), LOWRANK), 0)),
        row_blk(D), row_blk(D), row_blk(D), row_blk(2 * F), w2_blk]
    args += [w_t, w_t, w_t, w_pa, w_pb, w_o, n13, n2]
    out_specs += [
        pl.BlockSpec((D, LANE), lambda i: (0, a_blk(i))),
        pl.BlockSpec((D, LANE), lambda i: (0, 0)),
        pl.BlockSpec((D, LANE), lambda i: (0, step(i))),
        row_blk(D), row_blk(D), row_blk(D), row_blk(2 * F), w2_blk]
    bf = lambda r, c: jax.ShapeDtypeStruct((r, c), BF16)
    out_shape += [bf(D, lo), bf(D, LANE), bf(D, n_b * LANE),
                  bf(D, D), bf(D, D), bf(D, D), bf(D, 2 * F), bf(F, D)]
  return pl.pallas_call(
      functools.partial(_ffn_kernel, final_norm=nf is not None, cast=cast_weights is not None),
      grid=(n + 1,),
      in_specs=in_specs,
      out_specs=out_specs,
      out_shape=out_shape,
      compiler_params=pltpu.CompilerParams(
          dimension_semantics=("arbitrary",), vmem_limit_bytes=VMEM_LIMIT),
      name=name,
  )(*args)


def _project_piece(h, w_refs, p_ref, piece):
  idx, lo, hi, dst = piece
  p_ref[:, dst:dst + hi - lo] = _dot(h, w_refs[idx][:, lo:hi])


def _log_decay(p_ref, w_a2_ref, b_a_ref):
  a_lr = p_ref[:, C_ALR:C_ALR + LANE].astype(BF16)
  z = _dot(a_lr, w_a2_ref[...]) + b_a_ref[...]
  return _log_sigmoid(z) * (1.0 / TAU)


def _head_norm_gate(o, p_ref, gnw_ref):
  parts = []
  for hh in range(HEADS):
    sl = slice(hh * HV, (hh + 1) * HV)
    parts.append(_rms(o[:, sl], gnw_ref[:, sl]))
  r = p_ref[:, C_R:C_R + DV]
  return jnp.concatenate(parts, axis=-1) * (r * jax.nn.sigmoid(r))


def _gv_norm(p_ref, lnw_ref, lnb_ref):
  gv = _gelu(p_ref[:, C_GV:C_GV + D])
  mu = jnp.mean(gv, axis=-1, keepdims=True)
  d = gv - mu
  var = jnp.mean(d * d, axis=-1, keepdims=True)
  return d * lax.rsqrt(var + EPS) * lnw_ref[...] + lnb_ref[...]


def _merge_out(x, g2, ya, yb, p_ref, w_o_ref):
  ga = p_ref[:, C_GA:C_GA + D]
  gb = p_ref[:, C_GB:C_GB + D]
  merged = (jax.nn.sigmoid(ga) * ya + jax.nn.sigmoid(gb) * yb).astype(BF16)
  return x + g2 * _dot(merged, w_o_ref[...])


def _mix_prompt_kernel(x_ref, sh_ref, sc_ref, g2_ref, nw_ref,
                       wa_ref, walr_ref, wb_ref, w_a2_ref, b_a_ref,
                       ltri_ref, mask_ref, gnw_ref, w_pa_ref, lnw_ref, lnb_ref, ws_ref, bs_ref,
                       w_pb_ref, w_o_ref, o_ref, s_out_ref,
                       st_ref, p_ref, oa_ref, mx_ref, kt_ref, s_ref):
  j = pl.program_id(1)
  tm = TM_P
  w_refs = (wa_ref, walr_ref, wb_ref)

  @pl.when(j == 0)
  def _():
    st_ref[...] = jnp.zeros_like(st_ref)

  ltri = ltri_ref[...]
  ti = lax.broadcasted_iota(jnp.int32, (GM_CHUNK, GM_CHUNK), 0)
  si = lax.broadcasted_iota(jnp.int32, (GM_CHUNK, GM_CHUNK), 1)

  def mix_stages():
    oa = oa_ref
    mx = mx_ref
    env = {}

    def gate():
      env["g"] = _log_decay(p_ref, w_a2_ref, b_a_ref) * LOG2E

    def cumsum():
      cums = []
      for c in range(tm // CH):
        gc = env["g"][c * CH:(c + 1) * CH, :]
        g_hi = gc.astype(BF16)
        g_lo = (gc - g_hi.astype(F32)).astype(BF16)
        cums.append(_dot(ltri, g_hi) + _dot(ltri, g_lo))
      env["cums"] = cums

    def prep():
      decays = []
      qk = {}
      for c in range(tm // CH):
        rows = slice(c * CH, (c + 1) * CH)
        b = env["cums"][c]
        e_lvl, e_end = _level_exponents(env["g"][rows, :], b)
        q = (p_ref[rows, C_Q:C_Q + DK] * (HK ** -0.5)).astype(BF16)
        k = p_ref[rows, C_K:C_K + DK].astype(BF16)
        for lvl in range(N_LEVELS):
          e = jnp.exp2(e_lvl[lvl]).astype(BF16)
          qk[c, 2 * lvl] = q * e
          qk[c, 2 * lvl + 1] = k * e
        qk[c, 2 * N_LEVELS] = q
        qk[c, 2 * N_LEVELS + 1] = k
        qk[c, 2 * N_LEVELS + 2] = q * jnp.exp2(b).astype(BF16)
        qk[c, 2 * N_LEVELS + 3] = k * jnp.exp2(e_end).astype(BF16)
        decays.append(jnp.exp2(b[CH - 1:CH, :]))
      env["decays"] = decays
      env["qk"] = qk
      env["gvn"] = _gv_norm(p_ref, lnw_ref, lnb_ref).astype(BF16)
      env["u"] = _gelu(p_ref[:, C_U:C_U + D])
      env["pending"] = None

    def level_scores(c, hh):
      rows = slice(c * CH, (c + 1) * CH)
      ck = slice(hh * HK, (hh + 1) * HK)
      scores = None
      for lvl in range(N_LEVELS + 1):
        kt_ref[lvl] = env["qk"][c, 2 * lvl + 1][:, ck].T
        part = mask_ref[lvl] * _dot(env["qk"][c, 2 * lvl][:, ck], kt_ref[lvl])
        scores = part if scores is None else scores + part
      return scores.astype(BF16)

    def finish(c, hh, scores):
      rows = slice(c * CH, (c + 1) * CH)
      ck = slice(hh * HK, (hh + 1) * HK)
      v32 = p_ref[rows, C_V + hh * HV:C_V + (hh + 1) * HV]
      st = st_ref[hh]
      s_ref[hh] = st.T.astype(BF16)
      oa[rows, hh * HV:(hh + 1) * HV] = (
          _dot(scores, v32.astype(BF16))
          + _dot(env["qk"][c, 2 * N_LEVELS + 2][:, ck], s_ref[hh]))
      st_ref[hh] = (st * env["decays"][c][:, ck]
                    + _dot(v32.T.astype(BF16), env["qk"][c, 2 * N_LEVELS + 3][:, ck]))

    def gla(c):
      def run():
        for hh in range(HEADS):
          scores = level_scores(c, hh)
          if env["pending"] is not None:
            finish(*env["pending"])
          env["pending"] = (c, hh, scores)
      return run

    def gmlp():
      for gg in range(GROUPS):
        ws = jnp.where(si <= ti, ws_ref[gg], 0.0).astype(BF16)
        cols = slice(gg * GC, (gg + 1) * GC)
        for c in range(tm // GM_CHUNK):
          rows = slice(c * GM_CHUNK, (c + 1) * GM_CHUNK)
          mx[rows, cols] = _dot(ws, env["gvn"][rows, cols]) + bs_ref[:, cols]
      finish(*env["pending"])

    def branches():
      env["yb"] = _dot((env["u"] * mx[...]).astype(BF16), w_pb_ref[...])
      env["ya"] = _dot(_head_norm_gate(oa[...], p_ref, gnw_ref).astype(BF16), w_pa_ref[...])

    def out():
      o_ref[...] = _merge_out(x_ref[...], g2_ref[...], env["ya"], env["yb"], p_ref, w_o_ref)

    return gate, cumsum, prep, [gla(c) for c in range(tm // CH)], gmlp, branches, out

  gate, cumsum, prep, glas, gmlp, branches, out = mix_stages()
  h = _norm_mod(x_ref[...], nw_ref[...], sc_ref[...], sh_ref[...])
  proj = lambda piece: _project_piece(h, w_refs, p_ref, piece)
  proj(PROJ_QK)
  proj(PROJ_ALR)
  proj(PROJ_UGV)
  gate()
  proj(PROJ_V)
  cumsum()
  proj(PROJ_R)
  proj(PROJ_GATES)
  prep()
  for gla_chunk in glas:
    gla_chunk()
  gmlp()
  branches()
  out()

  @pl.when(j == pl.num_programs(1) - 1)
  def _():
    for hh in range(HEADS):
      s_out_ref[hh] = st_ref[hh].T


def _mix_prompt_call(x, mod3, nw, w_in3, w_a2, b_a, ltri, masks, gnw, w_pa, lnw, lnb, ws, bs_full,
                     w_pb, w_o, *, batch, seq):
  tm = TM_P
  tiles = seq // tm
  row_spec = pl.BlockSpec((tm, D), lambda b, j: (b * tiles + j, 0))

  def mod_spec(seg):
    return pl.BlockSpec((None, 1, D), lambda b, j: (b, 0, seg))

  wa, walr, wb = w_in3
  in_specs = [
      row_spec, mod_spec(3), mod_spec(4), mod_spec(5),
      _const_spec((1, D)), _const_spec(wa.shape), _const_spec(walr.shape), _const_spec(wb.shape),
      _const_spec((LANE, DK)), _const_spec((1, DK)),
      _const_spec(ltri.shape), _const_spec(masks.shape), _const_spec((1, DV)),
      _const_spec((DV, D)), _const_spec((1, D)), _const_spec((1, D)),
      _const_spec((GROUPS, GM_CHUNK, GM_CHUNK)), _const_spec((GM_CHUNK, D)),
      _const_spec((D, D)), _const_spec((D, D)),
  ]
  out_specs = [
      row_spec,
      pl.BlockSpec((None, None, HEADS, HK, HV), lambda b, j: (0, b, 0, 0, 0)),
  ]
  return pl.pallas_call(
      _mix_prompt_kernel,
      grid=(batch, tiles),
      in_specs=in_specs,
      out_specs=out_specs,
      out_shape=[jax.ShapeDtypeStruct((batch * seq, D), F32),
                 jax.ShapeDtypeStruct((1, batch, HEADS, HK, HV), F32)],
      scratch_shapes=[
          pltpu.VMEM((HEADS, HV, HK), F32),
          pltpu.VMEM((tm, D_P), F32),
          pltpu.VMEM((tm, DV), F32),
          pltpu.VMEM((tm, D), F32),
          pltpu.VMEM((N_LEVELS + 1, HK, CH), BF16),
          pltpu.VMEM((HEADS, HK, HV), BF16),
      ],
      compiler_params=pltpu.CompilerParams(
          dimension_semantics=("arbitrary", "arbitrary"), vmem_limit_bytes=VMEM_LIMIT),
      name="mix_prompt",
  )(x, mod3, mod3, mod3, nw, wa, walr, wb, w_a2, b_a, ltri, masks, gnw, w_pa,
    lnw, lnb, ws, bs_full, w_pb, w_o)


def _mix_sample_kernel(x_ref, sh_ref, sc_ref, g2_ref, nw_ref, wa_ref, walr_ref, wb_ref,
                       w_a2_ref, b_a_ref, gnw_ref, w_pa_ref, lnw_ref, lnb_ref, ws0_ref, bs0_ref,
                       w_pb_ref, w_o_ref, s_ref, o_ref, s_out_ref, gvn_ref,
                       p_ref, oa_ref, bc_ref, kt_ref, v_ref):
  i = pl.program_id(0)
  rows = x_ref.shape[0]

  @pl.when(i == 0)
  def _():
    h = _norm_mod(x_ref[...], nw_ref[...], sc_ref[...], sh_ref[...])
    for piece in PROJ_ALL:
      _project_piece(h, (wa_ref, walr_ref, wb_ref), p_ref, piece)
    g = _log_decay(p_ref, w_a2_ref, b_a_ref)
    q = p_ref[:, C_Q:C_Q + DK] * (HK ** -0.5)
    k = p_ref[:, C_K:C_K + DK]
    eb = jnp.exp(g)
    eb_hi = eb.astype(BF16)
    eb_lo = (eb - eb_hi.astype(F32)).astype(BF16)
    qe = q * eb
    for hh in range(HEADS):
      ck = slice(hh * HK, (hh + 1) * HK)
      bc_ref[hh, 0:HK, :] = eb_hi[:, ck].astype(F32).T.astype(BF16)
      bc_ref[hh, HK:2 * HK, :] = eb_lo[:, ck].astype(F32).T.astype(BF16)
      bc_ref[hh, 2 * HK:3 * HK, :] = qe[:, ck].T.astype(BF16)
      kt_ref[hh] = k[:, ck].T.astype(BF16)
      vh = p_ref[:, C_V + hh * HV:C_V + (hh + 1) * HV]
      v_ref[hh] = vh.astype(BF16)
      qk = jnp.sum(q[:, ck] * k[:, ck], axis=-1, keepdims=True)
      oa_ref[:, hh * HV:(hh + 1) * HV] = qk * vh

  row_id = lax.broadcasted_iota(jnp.int32, (rows, HV), 0)
  for nl in range(NB):
    n = i * NB + nl
    hot = row_id == n
    one_hot = jnp.where(hot, 1.0, 0.0).astype(BF16)
    for hh in range(HEADS):
      bc = _dot(bc_ref[hh], one_hot)
      decay = bc[0:HK] + bc[HK:2 * HK]
      qe_b = bc[2 * HK:3 * HK]
      kv = _dot(kt_ref[hh], jnp.where(hot, v_ref[hh], jnp.zeros((), BF16)))
      st = s_ref[nl, hh]
      o_inter = jnp.sum(qe_b * st, axis=0, keepdims=True)
      cur = oa_ref[pl.ds(n, 1), hh * HV:(hh + 1) * HV]
      oa_ref[pl.ds(n, 1), hh * HV:(hh + 1) * HV] = cur + o_inter
      s_out_ref[nl, hh] = decay * st + kv

  @pl.when(i == pl.num_programs(0) - 1)
  def _():
    ya = _dot(_head_norm_gate(oa_ref[...], p_ref, gnw_ref).astype(BF16), w_pa_ref[...])
    gvn = _gv_norm(p_ref, lnw_ref, lnb_ref)
    gvn_ref[...] = gvn
    mixed = ws0_ref[...] * gvn + bs0_ref[...]
    u = _gelu(p_ref[:, C_U:C_U + D])
    yb = _dot((u * mixed).astype(BF16), w_pb_ref[...])
    o_ref[...] = _merge_out(x_ref[...], g2_ref[...], ya, yb, p_ref, w_o_ref)


def _mix_sample_call(x, mod, nw, w_in3, w_a2, b_a, gnw, w_pa, lnw, lnb, ws0, bs0, w_pb, w_o, state):
  rows = x.shape[0]

  def mod_spec(seg):
    return pl.BlockSpec((rows, D), lambda i: (0, seg), pipeline_mode=pl.Buffered(1))

  wa, walr, wb = w_in3
  state_spec = pl.BlockSpec((None, NB, HEADS, HK, HV), lambda i: (0, i, 0, 0, 0))
  in_specs = [
      _const_spec((rows, D)), mod_spec(3), mod_spec(4), mod_spec(5),
      _const_spec((1, D)), _const_spec(wa.shape), _const_spec(walr.shape), _const_spec(wb.shape),
      _const_spec((LANE, DK)), _const_spec((1, DK)),
      _const_spec((1, DV)), _const_spec((DV, D)), _const_spec((1, D)), _const_spec((1, D)),
      _const_spec((1, D)), _const_spec((1, D)), _const_spec((D, D)), _const_spec((D, D)),
      state_spec,
  ]
  out_specs = [
      pl.BlockSpec((rows, D), lambda i: (0, 0)),
      state_spec,
      pl.BlockSpec((rows, D), lambda i: (0, 0)),
  ]
  return pl.pallas_call(
      _mix_sample_kernel,
      grid=(rows // NB,),
      in_specs=in_specs,
      out_specs=out_specs,
      out_shape=[jax.ShapeDtypeStruct((rows, D), F32),
                 jax.ShapeDtypeStruct(state.shape, F32),
                 jax.ShapeDtypeStruct((rows, D), F32)],
      scratch_shapes=[
          pltpu.VMEM((rows, D_P), F32),
          pltpu.VMEM((rows, DV), F32),
          pltpu.VMEM((HEADS, 3 * HK, rows), BF16),
          pltpu.VMEM((HEADS, HK, rows), BF16),
          pltpu.VMEM((HEADS, rows, HV), BF16),
      ],
      compiler_params=pltpu.CompilerParams(
          dimension_semantics=("arbitrary",), vmem_limit_bytes=VMEM_LIMIT),
      name="mix_sample",
  )(x, mod, mod, mod, nw, wa, walr, wb, w_a2, b_a, gnw, w_pa, lnw, lnb, ws0, bs0, w_pb, w_o, state)


def kernel(x_prompt, x_sample, state_gla, c_prompt, c_sample, w_ada, b_ada, norm1_w, ffn1_w13, ffn1_w2, norm2_w, w_in, w_a2, b_a, gla_norm_w, w_pa, gm_ln_w, gm_ln_b, gm_ws, gm_bs, w_pb, w_o, norm3_w, ffn2_w13, ffn2_w2, normf_w):
  batch, seq, _ = x_prompt.shape
  dec = x_sample.shape[0]
  depth = w_ada.shape[0]
  assert depth == 1 and x_sample.shape[1] == 1 and dec % NB == 0
  assert seq % TM_P == 0 and seq % (2 * TM_FFN) == 0

  ltri_np, masks_np = _gla_constants()
  ltri = jnp.asarray(ltri_np, BF16)
  masks = jnp.asarray(masks_np, F32)

  xp = x_prompt.reshape(batch * seq, D)
  xs = x_sample.reshape(dec, D)
  l = 0
  mod_all, w13_1, w2_1 = _ada_call(jnp.concatenate([c_sample, c_prompt], axis=0), w_ada[l],
                                   b_ada[l].reshape(1, N_MOD * D), ffn1_w13[l], ffn1_w2[l])
  mod_s = mod_all
  mod_p = mod_all[dec:].reshape(batch, 1, N_MOD * D)

  row = lambda a: a.reshape(1, -1)
  w_a2_p = jnp.pad(w_a2[l], ((0, LANE - LOWRANK), (0, 0))).astype(BF16)
  gnw = gla_norm_w[l].reshape(1, DV)
  bs_full = jnp.repeat(gm_bs[l].T, GC, axis=1)
  ws0 = jnp.repeat(gm_ws[l][:, 0, 0], GC).reshape(1, D)
  bs0 = bs_full[0:1]
  normf = row(normf_w)

  cast_weights = (jnp.transpose(w_in[l]), w_pa[l], w_pb[l], w_o[l], ffn2_w13[l], ffn2_w2[l])
  (xp, xs, wa, walr, wb, w_pa_b, w_pb_b, w_o_b, w13_2, w2_2) = _ffn_call(
      xp, xs, mod_p, mod_s, (0, 1, 2), row(norm1_w[l]), w13_1, w2_1, None, cast_weights,
      rows_per_batch=seq, tm=TM_FFN, name="ffn1")
  w_in3 = (wa, walr, wb)
  xp, s_prompt = _mix_prompt_call(xp, mod_p, row(norm2_w[l]), w_in3, w_a2_p, row(b_a[l]), ltri, masks,
                                  gnw, w_pa_b, row(gm_ln_w[l]), row(gm_ln_b[l]), gm_ws[l], bs_full,
                                  w_pb_b, w_o_b, batch=batch, seq=seq)
  xs, s_sample, gvn = _mix_sample_call(xs, mod_s, row(norm2_w[l]), w_in3, w_a2_p, row(b_a[l]), gnw,
                                       w_pa_b, row(gm_ln_w[l]), row(gm_ln_b[l]), ws0, bs0,
                                       w_pb_b, w_o_b, state_gla)
  yp, ys = _ffn_call(xp, xs, mod_p, mod_s, (6, 7, 8), row(norm3_w[l]), w13_2, w2_2, normf, None,
                     rows_per_batch=seq, tm=2 * TM_FFN, name="ffn2")

  return (yp.reshape(batch, seq, D), ys.reshape(dec, 1, D), s_prompt, s_sample,
          gvn.reshape(1, dec, 1, D))
```

```python
import functools

import numpy as np
import jax
import jax.numpy as jnp
from jax import lax
from jax.experimental import pallas as pl
from jax.experimental.pallas import tpu as pltpu

F32 = jnp.float32
BF16 = jnp.bfloat16

D = 1024
HEADS = 4
HK = 128
HV = 256
DK = HEADS * HK
DV = HEADS * HV
LOWRANK = 16
TAU = 16.0
GROUPS = 4
GC = D // GROUPS
GM_CHUNK = 128
F = 2816
N_MOD = 9
EPS = 1e-6
LOG2E = 1.4426950408889634

CH = 128
N_LEVELS = 7
LANE = 128

C_Q, C_K, C_ALR, C_U, C_GV, C_V, C_R, C_GA, C_GB = 0, 512, 1024, 1152, 2176, 3200, 4224, 5248, 6272
D_P = C_GB + D

PROJ_QK = (0, 0, 2 * DK, C_Q)
PROJ_ALR = (1, 0, LANE, C_ALR)
PROJ_UGV = (2, 0, 2 * D, C_U)
PROJ_V = (0, 2 * DK, 2 * DK + DV, C_V)
PROJ_R = (0, 2 * DK + DV, 2 * DK + 2 * DV, C_R)
PROJ_GATES = (2, 2 * D, 4 * D, C_GA)
PROJ_ALL = (PROJ_QK, PROJ_ALR, PROJ_UGV, PROJ_V, PROJ_R, PROJ_GATES)

TM_P = 512
TM_FFN = 512
NB = 8
VMEM_LIMIT = 58 * 1024 * 1024


def _gla_constants():
  t = np.arange(CH)
  masks = []
  for lvl in range(N_LEVELS):
    m = 1 << lvl
    upper = t >= (t // (2 * m)) * (2 * m) + m
    same_pair = (t[:, None] // (2 * m)) == (t[None, :] // (2 * m))
    masks.append((upper[:, None] & (~upper)[None, :] & same_pair).astype(np.float32))
  masks.append(np.eye(CH, dtype=np.float32))
  ltri = (t[None, :] <= t[:, None]).astype(np.float32)
  return ltri, np.stack(masks)


def _level_exponents(g, b):
  t = lax.broadcasted_iota(jnp.int32, g.shape, 0)
  g_prev = pltpu.roll(g, 1, 0)
  g_next = pltpu.roll(g, CH - 1, 0)
  r4 = t & 3
  levels = [
      jnp.where((t & 1) == 1, g, 0.0),
      jnp.where(r4 == 0, g_next, jnp.where(r4 == 1, 0.0, jnp.where(r4 == 2, g, g + g_prev))),
  ]
  groups = CH // 8
  row = lambda i: jnp.broadcast_to(b[i:i + 1, :], (8, b.shape[1]))
  b3 = [row(8 * j + 3) for j in range(groups)]
  b7 = [row(8 * j + 7) for j in range(groups)]
  levels.append(-jnp.abs(b - jnp.concatenate(b3, axis=0)))
  for lvl in range(3, N_LEVELS):
    m = 1 << lvl
    ref = [b7[((8 * j) // (2 * m)) * (2 * m) // 8 + m // 8 - 1] for j in range(groups)]
    levels.append(-jnp.abs(b - jnp.concatenate(ref, axis=0)))
  to_end = -jnp.abs(b - jnp.concatenate([b7[groups - 1]] * groups, axis=0))
  return levels, to_end


def _dot(a, b):
  return jnp.dot(a, b, preferred_element_type=F32)


def _dot_nt(a, b):
  return lax.dot_general(a, b, (((1,), (1,)), ((), ())), preferred_element_type=F32)


def _rms(x, w):
  ms = jnp.mean(x * x, axis=-1, keepdims=True)
  return x * lax.rsqrt(ms + EPS) * w


def _norm_mod(x, nw, sc, sh):
  if sc.shape[0] != 1:
    return (_rms(x, nw) * (1.0 + sc) + sh).astype(BF16)
  ms = jnp.mean(x * x, axis=-1, keepdims=True)
  return (x * lax.rsqrt(ms + EPS) * (nw * (1.0 + sc)) + sh).astype(BF16)


def _gelu(x):
  return 0.5 * x * (1.0 + lax.erf(x * (2.0 ** -0.5)))


def _log_sigmoid(z):
  return jnp.minimum(z, 0.0) - jnp.log1p(jnp.exp(-jnp.abs(z)))


def _const_spec(shape):
  nd = len(shape)
  return pl.BlockSpec(shape, lambda *_: (0,) * nd, pipeline_mode=pl.Buffered(1))


ADA_STEPS = 4


def _ada_kernel(c_ref, w_ref, b_ref, w13_ref, w2_ref, o_ref, o13_ref, o2_ref):
  c = c_ref[...]
  s = (c * jax.nn.sigmoid(c)).astype(BF16)
  o_ref[...] = _dot(s, w_ref[...].astype(BF16)) + b_ref[...]
  o13_ref[...] = w13_ref[...].astype(BF16)
  o2_ref[...] = w2_ref[...].astype(BF16)


def _ada_call(c_all, w_ada, b_ada, w13, w2):
  rows = c_all.shape[0]
  tn = N_MOD * D // ADA_STEPS
  r13, r2 = D // ADA_STEPS, F // ADA_STEPS
  assert tn % LANE == 0 and r13 % 16 == 0 and r2 % 16 == 0
  blk13 = pl.BlockSpec((r13, 2 * F), lambda i: (i, 0))
  blk2 = pl.BlockSpec((r2, D), lambda i: (i, 0))
  return pl.pallas_call(
      _ada_kernel,
      grid=(ADA_STEPS,),
      in_specs=[
          pl.BlockSpec((rows, D), lambda i: (0, 0)),
          pl.BlockSpec((D, tn), lambda i: (0, i)),
          pl.BlockSpec((1, tn), lambda i: (0, i)),
          blk13, blk2,
      ],
      out_specs=[pl.BlockSpec((rows, tn), lambda i: (0, i)), blk13, blk2],
      out_shape=[jax.ShapeDtypeStruct((rows, N_MOD * D), F32),
                 jax.ShapeDtypeStruct((D, 2 * F), BF16), jax.ShapeDtypeStruct((F, D), BF16)],
      compiler_params=pltpu.CompilerParams(
          dimension_semantics=("arbitrary",), vmem_limit_bytes=VMEM_LIMIT),
      name="ada",
  )(c_all, w_ada, b_ada, w13, w2)


def _ffn_rows(x_ref, sh_ref, sc_ref, g_ref, nw_ref, w13_ref, w2_ref, nf_ref, o_ref):
  rows = x_ref.shape[0]
  step = min(rows, TM_FFN)
  for r0 in range(0, rows, step):
    sl = slice(r0, r0 + step)
    per_row = sh_ref.shape[0] == rows
    sh, sc, g = (ref[sl, :] if per_row else ref[...] for ref in (sh_ref, sc_ref, g_ref))
    x = x_ref[sl, :]
    h = _norm_mod(x, nw_ref[...], sc, sh)
    a = _dot(h, w13_ref[:, :F])
    b = _dot(h, w13_ref[:, F:])
    p = (a * jax.nn.sigmoid(a) * b).astype(BF16)
    out = x + 0.5 * g * _dot(p, w2_ref[...])
    if nf_ref is not None:
      out = _rms(out, nf_ref[...])
    o_ref[sl, :] = out


def _ffn_kernel(x_ref, sh_ref, sc_ref, g_ref, xs_ref, shs_ref, scs_ref, gs_ref, nw_ref, w13_ref,
                w2_ref, *rest, final_norm, cast):
  rest = list(rest)
  nf_ref = rest.pop(0) if final_norm else None
  cast_in = [rest.pop(0) for _ in range(8)] if cast else None
  o_ref, os_ref = rest[0], rest[1]
  i = pl.program_id(0)
  n = pl.num_programs(0) - 1

  @pl.when(i < n)
  def _():
    _ffn_rows(x_ref, sh_ref, sc_ref, g_ref, nw_ref, w13_ref, w2_ref, nf_ref, o_ref)
    if cast:
      wta_ref, wtalr_ref, wtb_ref, w_pa_ref, w_pb_ref, w_o_ref, n13_ref, n2_ref = cast_in
      wa_ref, walr_ref, wb_ref, pa_ref, pb_ref, po_ref, o13_ref, o2_ref = rest[2:]
      wa_ref[...] = wta_ref[...].T.astype(BF16)
      wb_ref[...] = wtb_ref[...].T.astype(BF16)
      alr = jnp.concatenate([wtalr_ref[...], jnp.zeros((LANE - LOWRANK, D), F32)], axis=0)
      walr_ref[...] = alr.T.astype(BF16)
      pa_ref[...] = w_pa_ref[...].astype(BF16)
      pb_ref[...] = w_pb_ref[...].astype(BF16)
      po_ref[...] = w_o_ref[...].astype(BF16)
      o13_ref[...] = n13_ref[...].astype(BF16)
      o2_ref[...] = n2_ref[...].astype(BF16)

  @pl.when(i == n)
  def _():
    _ffn_rows(xs_ref, shs_ref, scs_ref, gs_ref, nw_ref, w13_ref, w2_ref, nf_ref, os_ref)


CAST_STEPS = 32
W2_CAST_ROWS = 2 * F // CAST_STEPS


def _ffn_call(x, xs, mod_p, mod_s, segs, nw, w13, w2, nf, cast_weights, *, rows_per_batch, tm, name):
  rows, rows_s = x.shape[0], xs.shape[0]
  n = rows // tm
  tiles = rows_per_batch // tm
  last = lambda i: jnp.minimum(i, n - 1)
  x_spec = pl.BlockSpec((tm, D), lambda i: (last(i), 0))
  xs_spec = pl.BlockSpec((rows_s, D), lambda i: (0, 0))
  in_specs = [x_spec]
  in_specs += [pl.BlockSpec((None, 1, D), lambda i, s=s: (last(i) // tiles, 0, s)) for s in segs]
  in_specs += [_const_spec((rows_s, D))]
  in_specs += [pl.BlockSpec((rows_s, D), lambda i, s=s: (0, s), pipeline_mode=pl.Buffered(1))
               for s in segs]
  in_specs += [_const_spec((1, D)), _const_spec((D, 2 * F)), _const_spec((F, D))]
  args = [x, mod_p, mod_p, mod_p, xs, mod_s, mod_s, mod_s, nw, w13, w2]
  if nf is not None:
    in_specs.append(_const_spec((1, D)))
    args.append(nf)
  out_specs = [x_spec, xs_spec]
  out_shape = [jax.ShapeDtypeStruct((rows, D), F32), jax.ShapeDtypeStruct((rows_s, D), F32)]
  if cast_weights is not None:
    assert n == CAST_STEPS and D % CAST_STEPS == 0
    w_t, w_pa, w_pb, w_o, n13, n2 = cast_weights
    lo = 2 * DK + 2 * DV
    n_a, n_b = lo // LANE, (w_t.shape[0] - lo - LOWRANK) // LANE
    assert n_a <= CAST_STEPS and n_b == CAST_STEPS
    rb = D // CAST_STEPS
    step = lambda i: jnp.minimum(i, CAST_STEPS - 1)
    a_blk = lambda i: jnp.minimum(i, n_a - 1)
    elem = lambda r: (pl.Element(r), pl.Element(D))
    row_blk = lambda cols: pl.BlockSpec((rb, cols), lambda i: (step(i), 0))
    w2_blk = pl.BlockSpec((W2_CAST_ROWS, D), lambda i: (step(i) // 2, 0))
    in_specs += [
        pl.BlockSpec(elem(LANE), lambda i: (pl.multiple_of(LANE * a_blk(i), LANE), 0)),
        pl.BlockSpec(elem(LOWRANK), lambda i: (lo, 0)),
        pl.BlockSpec(elem(LANE), lambda i: (pl.multiple_of(lo + LOWRANK + LANE * step(i), LOWRANK), 0)),
        row_blk(D), row_blk(D), row_blk(D), row_blk(2 * F), w2_blk]
    args += [w_t, w_t, w_t, w_pa, w_pb, w_o, n13, n2]
    out_specs += [
        pl.BlockSpec((D, LANE), lambda i: (0, a_blk(i))),
        pl.BlockSpec((D, LANE), lambda i: (0, 0)),
        pl.BlockSpec((D, LANE), lambda i: (0, step(i))),
        row_blk(D), row_blk(D), row_blk(D), row_blk(2 * F), w2_blk]
    bf = lambda r, c: jax.ShapeDtypeStruct((r, c), BF16)
    out_shape += [bf(D, lo), bf(D, LANE), bf(D, n_b * LANE),
                  bf(D, D), bf(D, D), bf(D, D), bf(D, 2 * F), bf(F, D)]
  return pl.pallas_call(
      functools.partial(_ffn_kernel, final_norm=nf is not None, cast=cast_weights is not None),
      grid=(n + 1,),
      in_specs=in_specs,
      out_specs=out_specs,
      out_shape=out_shape,
      compiler_params=pltpu.CompilerParams(
          dimension_semantics=("arbitrary",), vmem_limit_bytes=VMEM_LIMIT),
      name=name,
  )(*args)


def _project_piece(h, w_refs, p_ref, piece):
  idx, lo, hi, dst = piece
  p_ref[:, dst:dst + hi - lo] = _dot(h, w_refs[idx][:, lo:hi])


def _log_decay(p_ref, w_a2_ref, b_a_ref):
  a_lr = p_ref[:, C_ALR:C_ALR + LANE].astype(BF16)
  z = _dot(a_lr, w_a2_ref[...]) + b_a_ref[...]
  return _log_sigmoid(z) * (1.0 / TAU)


def _head_norm_gate(o, p_ref, gnw_ref):
  parts = []
  for hh in range(HEADS):
    sl = slice(hh * HV, (hh + 1) * HV)
    parts.append(_rms(o[:, sl], gnw_ref[:, sl]))
  r = p_ref[:, C_R:C_R + DV]
  return jnp.concatenate(parts, axis=-1) * (r * jax.nn.sigmoid(r))


def _gv_norm(p_ref, lnw_ref, lnb_ref):
  gv = _gelu(p_ref[:, C_GV:C_GV + D])
  mu = jnp.mean(gv, axis=-1, keepdims=True)
  d = gv - mu
  var = jnp.mean(d * d, axis=-1, keepdims=True)
  return d * lax.rsqrt(var + EPS) * lnw_ref[...] + lnb_ref[...]


def _merge_out(x, g2, ya, yb, p_ref, w_o_ref):
  ga = p_ref[:, C_GA:C_GA + D]
  gb = p_ref[:, C_GB:C_GB + D]
  merged = (jax.nn.sigmoid(ga) * ya + jax.nn.sigmoid(gb) * yb).astype(BF16)
  return x + g2 * _dot(merged, w_o_ref[...])


def _mix_prompt_kernel(x_ref, sh_ref, sc_ref, g2_ref, nw_ref,
                       wa_ref, walr_ref, wb_ref, w_a2_ref, b_a_ref,
                       ltri_ref, mask_ref, gnw_ref, w_pa_ref, lnw_ref, lnb_ref, ws_ref, bs_ref,
                       w_pb_ref, w_o_ref, o_ref, s_out_ref,
                       st_ref, p_ref, oa_ref, mx_ref, kt_ref, s_ref):
  j = pl.program_id(1)
  tm = TM_P
  w_refs = (wa_ref, walr_ref, wb_ref)

  @pl.when(j == 0)
  def _():
    st_ref[...] = jnp.zeros_like(st_ref)

  ltri = ltri_ref[...]
  ti = lax.broadcasted_iota(jnp.int32, (GM_CHUNK, GM_CHUNK), 0)
  si = lax.broadcasted_iota(jnp.int32, (GM_CHUNK, GM_CHUNK), 1)

  def mix_stages():
    oa = oa_ref
    mx = mx_ref
    env = {}

    def gate():
      env["g"] = _log_decay(p_ref, w_a2_ref, b_a_ref) * LOG2E

    def cumsum():
      cums = []
      for c in range(tm // CH):
        gc = env["g"][c * CH:(c + 1) * CH, :]
        g_hi = gc.astype(BF16)
        g_lo = (gc - g_hi.astype(F32)).astype(BF16)
        cums.append(_dot(ltri, g_hi) + _dot(ltri, g_lo))
      env["cums"] = cums

    def prep():
      decays = []
      qk = {}
      for c in range(tm // CH):
        rows = slice(c * CH, (c + 1) * CH)
        b = env["cums"][c]
        e_lvl, e_end = _level_exponents(env["g"][rows, :], b)
        q = (p_ref[rows, C_Q:C_Q + DK] * (HK ** -0.5)).astype(BF16)
        k = p_ref[rows, C_K:C_K + DK].astype(BF16)
        for lvl in range(N_LEVELS):
          e = jnp.exp2(e_lvl[lvl]).astype(BF16)
          qk[c, 2 * lvl] = q * e
          qk[c, 2 * lvl + 1] = k * e
        qk[c, 2 * N_LEVELS] = q
        qk[c, 2 * N_LEVELS + 1] = k
        qk[c, 2 * N_LEVELS + 2] = q * jnp.exp2(b).astype(BF16)
        qk[c, 2 * N_LEVELS + 3] = k * jnp.exp2(e_end).astype(BF16)
        decays.append(jnp.exp2(b[CH - 1:CH, :]))
      env["decays"] = decays
      env["qk"] = qk
      env["gvn"] = _gv_norm(p_ref, lnw_ref, lnb_ref).astype(BF16)
      env["u"] = _gelu(p_ref[:, C_U:C_U + D])
      env["pending"] = None

    def level_scores(c, hh):
      rows = slice(c * CH, (c + 1) * CH)
      ck = slice(hh * HK, (hh + 1) * HK)
      scores = None
      for lvl in range(N_LEVELS + 1):
        kt_ref[lvl] = env["qk"][c, 2 * lvl + 1][:, ck].T
        part = mask_ref[lvl] * _dot(env["qk"][c, 2 * lvl][:, ck], kt_ref[lvl])
        scores = part if scores is None else scores + part
      return scores.astype(BF16)

    def finish(c, hh, scores):
      rows = slice(c * CH, (c + 1) * CH)
      ck = slice(hh * HK, (hh + 1) * HK)
      v32 = p_ref[rows, C_V + hh * HV:C_V + (hh + 1) * HV]
      st = st_ref[hh]
      s_ref[hh] = st.T.astype(BF16)
      oa[rows, hh * HV:(hh + 1) * HV] = (
          _dot(scores, v32.astype(BF16))
          + _dot(env["qk"][c, 2 * N_LEVELS + 2][:, ck], s_ref[hh]))
      st_ref[hh] = (st * env["decays"][c][:, ck]
                    + _dot(v32.T.astype(BF16), env["qk"][c, 2 * N_LEVELS + 3][:, ck]))

    def gla(c):
      def run():
        for hh in range(HEADS):
          scores = level_scores(c, hh)
          if env["pending"] is not None:
            finish(*env["pending"])
          env["pending"] = (c, hh, scores)
      return run

    def gmlp():
      for gg in range(GROUPS):
        ws = jnp.where(si <= ti, ws_ref[gg], 0.0).astype(BF16)
        cols = slice(gg * GC, (gg + 1) * GC)
        for c in range(tm // GM_CHUNK):
          rows = slice(c * GM_CHUNK, (c + 1) * GM_CHUNK)
          mx[rows, cols] = _dot(ws, env["gvn"][rows, cols]) + bs_ref[:, cols]
      finish(*env["pending"])

    def branches():
      env["yb"] = _dot((env["u"] * mx[...]).astype(BF16), w_pb_ref[...])
      env["ya"] = _dot(_head_norm_gate(oa[...], p_ref, gnw_ref).astype(BF16), w_pa_ref[...])

    def out():
      ga = p_ref[:, C_GA:C_GA + D]
      gb = p_ref[:, C_GB:C_GB + D]
      merged = (jax.nn.sigmoid(ga) * env["ya"] + jax.nn.sigmoid(gb) * env["yb"]).astype(BF16)
      for r0 in range(0, tm, tm // 2):
        sl = slice(r0, r0 + tm // 2)
        o_ref[sl, :] = x_ref[sl, :] + g2_ref[...] * _dot(merged[sl, :], w_o_ref[...])

    return gate, cumsum, prep, [gla(c) for c in range(tm // CH)], gmlp, branches, out

  gate, cumsum, prep, glas, gmlp, branches, out = mix_stages()
  halves = []
  for r0 in range(0, tm, tm // 2):
    sl = slice(r0, r0 + tm // 2)
    halves.append(_norm_mod(x_ref[sl, :], nw_ref[...], sc_ref[...], sh_ref[...]))
    idx, lo, hi, dst = PROJ_QK
    p_ref[sl, dst:dst + hi - lo] = _dot(halves[-1], w_refs[idx][:, lo:hi])
  h = jnp.concatenate(halves, axis=0)
  proj = lambda piece: _project_piece(h, w_refs, p_ref, piece)
  proj(PROJ_ALR)
  proj(PROJ_UGV)
  gate()
  proj(PROJ_V)
  cumsum()
  proj(PROJ_R)
  proj(PROJ_GATES)
  prep()
  for gla_chunk in glas:
    gla_chunk()
  gmlp()
  branches()
  out()

  @pl.when(j == pl.num_programs(1) - 1)
  def _():
    for hh in range(HEADS):
      s_out_ref[hh] = st_ref[hh].T


def _mix_prompt_call(x, mod3, nw, w_in3, w_a2, b_a, ltri, masks, gnw, w_pa, lnw, lnb, ws, bs_full,
                     w_pb, w_o, *, batch, seq):
  tm = TM_P
  tiles = seq // tm
  row_spec = pl.BlockSpec((tm, D), lambda b, j: (b * tiles + j, 0))

  def mod_spec(seg):
    return pl.BlockSpec((None, 1, D), lambda b, j: (b, 0, seg))

  wa, walr, wb = w_in3
  in_specs = [
      row_spec, mod_spec(3), mod_spec(4), mod_spec(5),
      _const_spec((1, D)), _const_spec(wa.shape), _const_spec(walr.shape), _const_spec(wb.shape),
      _const_spec((LANE, DK)), _const_spec((1, DK)),
      _const_spec(ltri.shape), _const_spec(masks.shape), _const_spec((1, DV)),
      _const_spec((DV, D)), _const_spec((1, D)), _const_spec((1, D)),
      _const_spec((GROUPS, GM_CHUNK, GM_CHUNK)), _const_spec((GM_CHUNK, D)),
      _const_spec((D, D)), _const_spec((D, D)),
  ]
  out_specs = [
      row_spec,
      pl.BlockSpec((None, None, HEADS, HK, HV), lambda b, j: (0, b, 0, 0, 0)),
  ]
  return pl.pallas_call(
      _mix_prompt_kernel,
      grid=(batch, tiles),
      in_specs=in_specs,
      out_specs=out_specs,
      out_shape=[jax.ShapeDtypeStruct((batch * seq, D), F32),
                 jax.ShapeDtypeStruct((1, batch, HEADS, HK, HV), F32)],
      scratch_shapes=[
          pltpu.VMEM((HEADS, HV, HK), F32),
          pltpu.VMEM((tm, D_P), F32),
          pltpu.VMEM((tm, DV), F32),
          pltpu.VMEM((tm, D), F32),
          pltpu.VMEM((N_LEVELS + 1, HK, CH), BF16),
          pltpu.VMEM((HEADS, HK, HV), BF16),
      ],
      compiler_params=pltpu.CompilerParams(
          dimension_semantics=("arbitrary", "arbitrary"), vmem_limit_bytes=VMEM_LIMIT),
      name="mix_prompt",
  )(x, mod3, mod3, mod3, nw, wa, walr, wb, w_a2, b_a, ltri, masks, gnw, w_pa,
    lnw, lnb, ws, bs_full, w_pb, w_o)


def _mix_sample_kernel(x_ref, sh_ref, sc_ref, g2_ref, nw_ref, wa_ref, walr_ref, wb_ref,
                       w_a2_ref, b_a_ref, gnw_ref, w_pa_ref, lnw_ref, lnb_ref, ws0_ref, bs0_ref,
                       w_pb_ref, w_o_ref, s_ref, o_ref, s_out_ref, gvn_ref,
                       p_ref, oa_ref, bc_ref, kt_ref, v_ref):
  i = pl.program_id(0)
  rows = x_ref.shape[0]

  @pl.when(i == 0)
  def _():
    h = _norm_mod(x_ref[...], nw_ref[...], sc_ref[...], sh_ref[...])
    for piece in PROJ_ALL:
      _project_piece(h, (wa_ref, walr_ref, wb_ref), p_ref, piece)
    g = _log_decay(p_ref, w_a2_ref, b_a_ref)
    q = p_ref[:, C_Q:C_Q + DK] * (HK ** -0.5)
    k = p_ref[:, C_K:C_K + DK]
    eb = jnp.exp(g)
    eb_hi = eb.astype(BF16)
    eb_lo = (eb - eb_hi.astype(F32)).astype(BF16)
    qe = q * eb
    for hh in range(HEADS):
      ck = slice(hh * HK, (hh + 1) * HK)
      bc_ref[hh, 0:HK, :] = eb_hi[:, ck].astype(F32).T.astype(BF16)
      bc_ref[hh, HK:2 * HK, :] = eb_lo[:, ck].astype(F32).T.astype(BF16)
      bc_ref[hh, 2 * HK:3 * HK, :] = qe[:, ck].T.astype(BF16)
      kt_ref[hh] = k[:, ck].T.astype(BF16)
      vh = p_ref[:, C_V + hh * HV:C_V + (hh + 1) * HV]
      v_ref[hh] = vh.astype(BF16)
      qk = jnp.sum(q[:, ck] * k[:, ck], axis=-1, keepdims=True)
      oa_ref[:, hh * HV:(hh + 1) * HV] = qk * vh

  row_id = lax.broadcasted_iota(jnp.int32, (rows, HV), 0)
  for nl in range(NB):
    n = i * NB + nl
    hot = row_id == n
    one_hot = jnp.where(hot, 1.0, 0.0).astype(BF16)
    for hh in range(HEADS):
      bc = _dot(bc_ref[hh], one_hot)
      decay = bc[0:HK] + bc[HK:2 * HK]
      qe_b = bc[2 * HK:3 * HK]
      kv = _dot(kt_ref[hh], jnp.where(hot, v_ref[hh], jnp.zeros((), BF16)))
      st = s_ref[nl, hh]
      o_inter = jnp.sum(qe_b * st, axis=0, keepdims=True)
      cur = oa_ref[pl.ds(n, 1), hh * HV:(hh + 1) * HV]
      oa_ref[pl.ds(n, 1), hh * HV:(hh + 1) * HV] = cur + o_inter
      s_out_ref[nl, hh] = decay * st + kv

  @pl.when(i == pl.num_programs(0) - 1)
  def _():
    ya = _dot(_head_norm_gate(oa_ref[...], p_ref, gnw_ref).astype(BF16), w_pa_ref[...])
    gvn = _gv_norm(p_ref, lnw_ref, lnb_ref)
    gvn_ref[...] = gvn
    mixed = ws0_ref[...] * gvn + bs0_ref[...]
    u = _gelu(p_ref[:, C_U:C_U + D])
    yb = _dot((u * mixed).astype(BF16), w_pb_ref[...])
    o_ref[...] = _merge_out(x_ref[...], g2_ref[...], ya, yb, p_ref, w_o_ref)


def _mix_sample_call(x, mod, nw, w_in3, w_a2, b_a, gnw, w_pa, lnw, lnb, ws0, bs0, w_pb, w_o, state):
  rows = x.shape[0]

  def mod_spec(seg):
    return pl.BlockSpec((rows, D), lambda i: (0, seg), pipeline_mode=pl.Buffered(1))

  wa, walr, wb = w_in3
  state_spec = pl.BlockSpec((None, NB, HEADS, HK, HV), lambda i: (0, i, 0, 0, 0))
  in_specs = [
      _const_spec((rows, D)), mod_spec(3), mod_spec(4), mod_spec(5),
      _const_spec((1, D)), _const_spec(wa.shape), _const_spec(walr.shape), _const_spec(wb.shape),
      _const_spec((LANE, DK)), _const_spec((1, DK)),
      _const_spec((1, DV)), _const_spec((DV, D)), _const_spec((1, D)), _const_spec((1, D)),
      _const_spec((1, D)), _const_spec((1, D)), _const_spec((D, D)), _const_spec((D, D)),
      state_spec,
  ]
  out_specs = [
      pl.BlockSpec((rows, D), lambda i: (0, 0)),
      state_spec,
      pl.BlockSpec((rows, D), lambda i: (0, 0)),
  ]
  return pl.pallas_call(
      _mix_sample_kernel,
      grid=(rows // NB,),
      in_specs=in_specs,
      out_specs=out_specs,
      out_shape=[jax.ShapeDtypeStruct((rows, D), F32),
                 jax.ShapeDtypeStruct(state.shape, F32),
                 jax.ShapeDtypeStruct((rows, D), F32)],
      scratch_shapes=[
          pltpu.VMEM((rows, D_P), F32),
          pltpu.VMEM((rows, DV), F32),
          pltpu.VMEM((HEADS, 3 * HK, rows), BF16),
          pltpu.VMEM((HEADS, HK, rows), BF16),
          pltpu.VMEM((HEADS, rows, HV), BF16),
      ],
      compiler_params=pltpu.CompilerParams(
          dimension_semantics=("arbitrary",), vmem_limit_bytes=VMEM_LIMIT),
      name="mix_sample",
  )(x, mod, mod, mod, nw, wa, walr, wb, w_a2, b_a, gnw, w_pa, lnw, lnb, ws0, bs0, w_pb, w_o, state)


def kernel(x_prompt, x_sample, state_gla, c_prompt, c_sample, w_ada, b_ada, norm1_w, ffn1_w13, ffn1_w2, norm2_w, w_in, w_a2, b_a, gla_norm_w, w_pa, gm_ln_w, gm_ln_b, gm_ws, gm_bs, w_pb, w_o, norm3_w, ffn2_w13, ffn2_w2, normf_w):
  batch, seq, _ = x_prompt.shape
  dec = x_sample.shape[0]
  depth = w_ada.shape[0]
  assert depth == 1 and x_sample.shape[1] == 1 and dec % NB == 0
  assert seq % TM_P == 0 and seq % (2 * TM_FFN) == 0

  ltri_np, masks_np = _gla_constants()
  ltri = jnp.asarray(ltri_np, BF16)
  masks = jnp.asarray(masks_np, F32)

  xp = x_prompt.reshape(batch * seq, D)
  xs = x_sample.reshape(dec, D)
  l = 0
  mod_all, w13_1, w2_1 = _ada_call(jnp.concatenate([c_sample, c_prompt], axis=0), w_ada[l],
                                   b_ada[l].reshape(1, N_MOD * D), ffn1_w13[l], ffn1_w2[l])
  mod_s = mod_all
  mod_p = mod_all[dec:].reshape(batch, 1, N_MOD * D)

  row = lambda a: a.reshape(1, -1)
  w_a2_p = jnp.pad(w_a2[l], ((0, LANE - LOWRANK), (0, 0))).astype(BF16)
  gnw = gla_norm_w[l].reshape(1, DV)
  bs_full = jnp.repeat(gm_bs[l].T, GC, axis=1)
  ws0 = jnp.repeat(gm_ws[l][:, 0, 0], GC).reshape(1, D)
  bs0 = bs_full[0:1]
  normf = row(normf_w)

  cast_weights = (jnp.transpose(w_in[l]), w_pa[l], w_pb[l], w_o[l], ffn2_w13[l], ffn2_w2[l])
  (xp, xs, wa, walr, wb, w_pa_b, w_pb_b, w_o_b, w13_2, w2_2) = _ffn_call(
      xp, xs, mod_p, mod_s, (0, 1, 2), row(norm1_w[l]), w13_1, w2_1, None, cast_weights,
      rows_per_batch=seq, tm=TM_FFN, name="ffn1")
  w_in3 = (wa, walr, wb)
  xp, s_prompt = _mix_prompt_call(xp, mod_p, row(norm2_w[l]), w_in3, w_a2_p, row(b_a[l]), ltri, masks,
                                  gnw, w_pa_b, row(gm_ln_w[l]), row(gm_ln_b[l]), gm_ws[l], bs_full,
                                  w_pb_b, w_o_b, batch=batch, seq=seq)
  xs, s_sample, gvn = _mix_sample_call(xs, mod_s, row(norm2_w[l]), w_in3, w_a2_p, row(b_a[l]), gnw,
                                       w_pa_b, row(gm_ln_w[l]), row(gm_ln_b[l]), ws0, bs0,
                                       w_pb_b, w_o_b, state_gla)
  yp, ys = _ffn_call(xp, xs, mod_p, mod_s, (6, 7, 8), row(norm3_w[l]), w13_2, w2_2, normf, None,
                     rows_per_batch=seq, tm=2 * TM_FFN, name="ffn2")

  return (yp.reshape(batch, seq, D), ys.reshape(dec, 1, D), s_prompt, s_sample,
          gvn.reshape(1, dec, 1, D))
```

```python
import functools

import numpy as np
import jax
import jax.numpy as jnp
from jax import lax
from jax.experimental import pallas as pl
from jax.experimental.pallas import tpu as pltpu

F32 = jnp.float32
BF16 = jnp.bfloat16

D = 1024
HEADS = 4
HK = 128
HV = 256
DK = HEADS * HK
DV = HEADS * HV
LOWRANK = 16
TAU = 16.0
GROUPS = 4
GC = D // GROUPS
GM_CHUNK = 128
F = 2816
N_MOD = 9
EPS = 1e-6
LOG2E = 1.4426950408889634

CH = 128
N_LEVELS = 7
LANE = 128

C_Q, C_K, C_ALR, C_U, C_GV, C_V, C_R, C_GA, C_GB = 0, 512, 1024, 1152, 2176, 3200, 4224, 5248, 6272
D_P = C_GB + D

PROJ_QK = (0, 0, 2 * DK, C_Q)
PROJ_ALR = (1, 0, LANE, C_ALR)
PROJ_UGV = (2, 0, 2 * D, C_U)
PROJ_V = (0, 2 * DK, 2 * DK + DV, C_V)
PROJ_R = (0, 2 * DK + DV, 2 * DK + 2 * DV, C_R)
PROJ_GATES = (2, 2 * D, 4 * D, C_GA)
PROJ_ALL = (PROJ_QK, PROJ_ALR, PROJ_UGV, PROJ_V, PROJ_R, PROJ_GATES)

TM_P = 512
TM_FFN = 512
NB = 8
VMEM_LIMIT = 58 * 1024 * 1024


def _gla_constants():
  t = np.arange(CH)
  masks = []
  for lvl in range(N_LEVELS):
    m = 1 << lvl
    upper = t >= (t // (2 * m)) * (2 * m) + m
    same_pair = (t[:, None] // (2 * m)) == (t[None, :] // (2 * m))
    masks.append((upper[:, None] & (~upper)[None, :] & same_pair).astype(np.float32))
  masks.append(np.eye(CH, dtype=np.float32))
  ltri = (t[None, :] <= t[:, None]).astype(np.float32)
  return ltri, np.stack(masks)


def _level_exponents(g, b):
  t = lax.broadcasted_iota(jnp.int32, g.shape, 0)
  g_prev = pltpu.roll(g, 1, 0)
  g_next = pltpu.roll(g, CH - 1, 0)
  r4 = t & 3
  levels = [
      jnp.where((t & 1) == 1, g, 0.0),
      jnp.where(r4 == 0, g_next, jnp.where(r4 == 1, 0.0, jnp.where(r4 == 2, g, g + g_prev))),
  ]
  groups = CH // 8
  row = lambda i: jnp.broadcast_to(b[i:i + 1, :], (8, b.shape[1]))
  b3 = [row(8 * j + 3) for j in range(groups)]
  b7 = [row(8 * j + 7) for j in range(groups)]
  levels.append(-jnp.abs(b - jnp.concatenate(b3, axis=0)))
  for lvl in range(3, N_LEVELS):
    m = 1 << lvl
    ref = [b7[((8 * j) // (2 * m)) * (2 * m) // 8 + m // 8 - 1] for j in range(groups)]
    levels.append(-jnp.abs(b - jnp.concatenate(ref, axis=0)))
  to_end = -jnp.abs(b - jnp.concatenate([b7[groups - 1]] * groups, axis=0))
  return levels, to_end


def _dot(a, b):
  return jnp.dot(a, b, preferred_element_type=F32)


def _rms(x, w):
  ms = jnp.mean(x * x, axis=-1, keepdims=True)
  return x * lax.rsqrt(ms + EPS) * w


def _norm_mod(x, nw, sc, sh):
  if sc.shape[0] != 1:
    return (_rms(x, nw) * (1.0 + sc) + sh).astype(BF16)
  ms = jnp.mean(x * x, axis=-1, keepdims=True)
  return (x * lax.rsqrt(ms + EPS) * (nw * (1.0 + sc)) + sh).astype(BF16)


def _gelu(x):
  return 0.5 * x * (1.0 + lax.erf(x * (2.0 ** -0.5)))


def _log_sigmoid(z):
  return jnp.minimum(z, 0.0) - jnp.log1p(jnp.exp(-jnp.abs(z)))


def _const_spec(shape):
  nd = len(shape)
  return pl.BlockSpec(shape, lambda *_: (0,) * nd, pipeline_mode=pl.Buffered(1))


ADA_STEPS = 4


def _ada_kernel(c_ref, w_ref, b_ref, w13_ref, w2_ref, o_ref, o13_ref, o2_ref):
  c = c_ref[...]
  s = (c * jax.nn.sigmoid(c)).astype(BF16)
  o_ref[...] = _dot(s, w_ref[...].astype(BF16)) + b_ref[...]
  o13_ref[...] = w13_ref[...].astype(BF16)
  o2_ref[...] = w2_ref[...].astype(BF16)


def _ada_call(c_all, w_ada, b_ada, w13, w2):
  rows = c_all.shape[0]
  tn = N_MOD * D // ADA_STEPS
  r13, r2 = D // ADA_STEPS, F // ADA_STEPS
  assert tn % LANE == 0 and r13 % 16 == 0 and r2 % 16 == 0
  blk13 = pl.BlockSpec((r13, 2 * F), lambda i: (i, 0))
  blk2 = pl.BlockSpec((r2, D), lambda i: (i, 0))
  return pl.pallas_call(
      _ada_kernel,
      grid=(ADA_STEPS,),
      in_specs=[
          pl.BlockSpec((rows, D), lambda i: (0, 0)),
          pl.BlockSpec((D, tn), lambda i: (0, i)),
          pl.BlockSpec((1, tn), lambda i: (0, i)),
          blk13, blk2,
      ],
      out_specs=[pl.BlockSpec((rows, tn), lambda i: (0, i)), blk13, blk2],
      out_shape=[jax.ShapeDtypeStruct((rows, N_MOD * D), F32),
                 jax.ShapeDtypeStruct((D, 2 * F), BF16), jax.ShapeDtypeStruct((F, D), BF16)],
      compiler_params=pltpu.CompilerParams(
          dimension_semantics=("arbitrary",), vmem_limit_bytes=VMEM_LIMIT),
      name="ada",
  )(c_all, w_ada, b_ada, w13, w2)


def _ffn_rows(x_ref, sh_ref, sc_ref, g_ref, nw_ref, w13_ref, w2_ref, nf_ref, o_ref):
  rows = x_ref.shape[0]
  step = min(rows, TM_FFN)
  for r0 in range(0, rows, step):
    sl = slice(r0, r0 + step)
    per_row = sh_ref.shape[0] == rows
    sh, sc, g = (ref[sl, :] if per_row else ref[...] for ref in (sh_ref, sc_ref, g_ref))
    x = x_ref[sl, :]
    h = _norm_mod(x, nw_ref[...], sc, sh)
    a = _dot(h, w13_ref[:, :F])
    b = _dot(h, w13_ref[:, F:])
    p = (a * jax.nn.sigmoid(a) * b).astype(BF16)
    out = x + 0.5 * g * _dot(p, w2_ref[...])
    if nf_ref is not None:
      out = _rms(out, nf_ref[...])
    o_ref[sl, :] = out


def _ffn_kernel(x_ref, sh_ref, sc_ref, g_ref, xs_ref, shs_ref, scs_ref, gs_ref, nw_ref, w13_ref,
                w2_ref, *rest, final_norm, cast):
  rest = list(rest)
  nf_ref = rest.pop(0) if final_norm else None
  cast_in = [rest.pop(0) for _ in range(8)] if cast else None
  o_ref, os_ref = rest[0], rest[1]
  i = pl.program_id(0)
  n = pl.num_programs(0) - 1

  @pl.when(i < n)
  def _():
    _ffn_rows(x_ref, sh_ref, sc_ref, g_ref, nw_ref, w13_ref, w2_ref, nf_ref, o_ref)
    if cast:
      wta_ref, wtalr_ref, wtb_ref, w_pa_ref, w_pb_ref, w_o_ref, n13_ref, n2_ref = cast_in
      wa_ref, walr_ref, wb_ref, pa_ref, pb_ref, po_ref, o13_ref, o2_ref = rest[2:]
      wa_ref[...] = wta_ref[...].T.astype(BF16)
      wb_ref[...] = wtb_ref[...].T.astype(BF16)
      alr = jnp.concatenate([wtalr_ref[...], jnp.zeros((LANE - LOWRANK, D), F32)], axis=0)
      walr_ref[...] = alr.T.astype(BF16)
      pa_ref[...] = w_pa_ref[...].astype(BF16)
      pb_ref[...] = w_pb_ref[...].astype(BF16)
      po_ref[...] = w_o_ref[...].astype(BF16)
      o13_ref[...] = n13_ref[...].astype(BF16)
      o2_ref[...] = n2_ref[...].astype(BF16)

  @pl.when(i == n)
  def _():
    _ffn_rows(xs_ref, shs_ref, scs_ref, gs_ref, nw_ref, w13_ref, w2_ref, nf_ref, os_ref)


CAST_STEPS = 32
W2_CAST_ROWS = 2 * F // CAST_STEPS


def _ffn_call(x, xs, mod_p, mod_s, segs, nw, w13, w2, nf, cast_weights, *, rows_per_batch, tm, name):
  rows, rows_s = x.shape[0], xs.shape[0]
  n = rows // tm
  tiles = rows_per_batch // tm
  last = lambda i: jnp.minimum(i, n - 1)
  x_spec = pl.BlockSpec((tm, D), lambda i: (last(i), 0))
  xs_spec = pl.BlockSpec((rows_s, D), lambda i: (0, 0))
  in_specs = [x_spec]
  in_specs += [pl.BlockSpec((None, 1, D), lambda i, s=s: (last(i) // tiles, 0, s)) for s in segs]
  in_specs += [_const_spec((rows_s, D))]
  in_specs += [pl.BlockSpec((rows_s, D), lambda i, s=s: (0, s), pipeline_mode=pl.Buffered(1))
               for s in segs]
  in_specs += [_const_spec((1, D)), _const_spec((D, 2 * F)), _const_spec((F, D))]
  args = [x, mod_p, mod_p, mod_p, xs, mod_s, mod_s, mod_s, nw, w13, w2]
  if nf is not None:
    in_specs.append(_const_spec((1, D)))
    args.append(nf)
  out_specs = [x_spec, xs_spec]
  out_shape = [jax.ShapeDtypeStruct((rows, D), F32), jax.ShapeDtypeStruct((rows_s, D), F32)]
  if cast_weights is not None:
    assert n == CAST_STEPS and D % CAST_STEPS == 0
    w_t, w_pa, w_pb, w_o, n13, n2 = cast_weights
    lo = 2 * DK + 2 * DV
    n_a, n_b = lo // LANE, (w_t.shape[0] - lo - LOWRANK) // LANE
    assert n_a <= CAST_STEPS and n_b == CAST_STEPS
    rb = D // CAST_STEPS
    step = lambda i: jnp.minimum(i, CAST_STEPS - 1)
    a_blk = lambda i: jnp.minimum(i, n_a - 1)
    elem = lambda r: (pl.Element(r), pl.Element(D))
    row_blk = lambda cols: pl.BlockSpec((rb, cols), lambda i: (step(i), 0))
    w2_blk = pl.BlockSpec((W2_CAST_ROWS, D), lambda i: (step(i) // 2, 0))
    in_specs += [
        pl.BlockSpec(elem(LANE), lambda i: (pl.multiple_of(LANE * a_blk(i), LANE), 0)),
        pl.BlockSpec(elem(LOWRANK), lambda i: (lo, 0)),
        pl.BlockSpec(elem(LANE), lambda i: (pl.multiple_of(lo + LOWRANK + LANE * step(i), LOWRANK), 0)),
        row_blk(D), row_blk(D), row_blk(D), row_blk(2 * F), w2_blk]
    args += [w_t, w_t, w_t, w_pa, w_pb, w_o, n13, n2]
    out_specs += [
        pl.BlockSpec((D, LANE), lambda i: (0, a_blk(i))),
        pl.BlockSpec((D, LANE), lambda i: (0, 0)),
        pl.BlockSpec((D, LANE), lambda i: (0, step(i))),
        row_blk(D), row_blk(D), row_blk(D), row_blk(2 * F), w2_blk]
    bf = lambda r, c: jax.ShapeDtypeStruct((r, c), BF16)
    out_shape += [bf(D, lo), bf(D, LANE), bf(D, n_b * LANE),
                  bf(D, D), bf(D, D), bf(D, D), bf(D, 2 * F), bf(F, D)]
  return pl.pallas_call(
      functools.partial(_ffn_kernel, final_norm=nf is not None, cast=cast_weights is not None),
      grid=(n + 1,),
      in_specs=in_specs,
      out_specs=out_specs,
      out_shape=out_shape,
      compiler_params=pltpu.CompilerParams(
          dimension_semantics=("arbitrary",), vmem_limit_bytes=VMEM_LIMIT),
      name=name,
  )(*args)


def _project_piece(h, w_refs, p_ref, piece):
  idx, lo, hi, dst = piece
  p_ref[:, dst:dst + hi - lo] = _dot(h, w_refs[idx][:, lo:hi])


def _log_decay(p_ref, w_a2_ref, b_a_ref):
  a_lr = p_ref[:, C_ALR:C_ALR + LANE].astype(BF16)
  z = _dot(a_lr, w_a2_ref[...]) + b_a_ref[...]
  return _log_sigmoid(z) * (1.0 / TAU)


def _head_norm_gate(o, p_ref, gnw_ref):
  parts = []
  for hh in range(HEADS):
    sl = slice(hh * HV, (hh + 1) * HV)
    parts.append(_rms(o[:, sl], gnw_ref[:, sl]))
  r = p_ref[:, C_R:C_R + DV]
  return jnp.concatenate(parts, axis=-1) * (r * jax.nn.sigmoid(r))


def _gv_norm(p_ref, lnw_ref, lnb_ref):
  gv = _gelu(p_ref[:, C_GV:C_GV + D])
  mu = jnp.mean(gv, axis=-1, keepdims=True)
  d = gv - mu
  var = jnp.mean(d * d, axis=-1, keepdims=True)
  return d * lax.rsqrt(var + EPS) * lnw_ref[...] + lnb_ref[...]


def _merge_out(x, g2, ya, yb, p_ref, w_o_ref):
  ga = p_ref[:, C_GA:C_GA + D]
  gb = p_ref[:, C_GB:C_GB + D]
  merged = (jax.nn.sigmoid(ga) * ya + jax.nn.sigmoid(gb) * yb).astype(BF16)
  return x + g2 * _dot(merged, w_o_ref[...])


def _mix_prompt_kernel(x_ref, sh_ref, sc_ref, g2_ref, nw_ref,
                       wa_ref, walr_ref, wb_ref, w_a2_ref, b_a_ref,
                       ltri_ref, mask_ref, gnw_ref, w_pa_ref, lnw_ref, lnb_ref, ws_ref, bs_ref,
                       w_pb_ref, w_o_ref, o_ref, s_out_ref,
                       st_ref, p_ref, oa_ref, mx_ref, kt_ref, s_ref):
  j = pl.program_id(1)
  tm = TM_P
  w_refs = (wa_ref, walr_ref, wb_ref)

  @pl.when(j == 0)
  def _():
    st_ref[...] = jnp.zeros_like(st_ref)

  ltri = ltri_ref[...]
  ti = lax.broadcasted_iota(jnp.int32, (GM_CHUNK, GM_CHUNK), 0)
  si = lax.broadcasted_iota(jnp.int32, (GM_CHUNK, GM_CHUNK), 1)

  def mix_stages():
    oa = oa_ref
    mx = mx_ref
    env = {}

    def gate():
      env["g"] = _log_decay(p_ref, w_a2_ref, b_a_ref) * LOG2E

    def cumsum():
      cums = []
      for c in range(tm // CH):
        gc = env["g"][c * CH:(c + 1) * CH, :]
        g_hi = gc.astype(BF16)
        g_lo = (gc - g_hi.astype(F32)).astype(BF16)
        cums.append(_dot(ltri, g_hi) + _dot(ltri, g_lo))
      env["cums"] = cums

    def prep():
      decays = []
      qk = {}
      for c in range(tm // CH):
        rows = slice(c * CH, (c + 1) * CH)
        b = env["cums"][c]
        e_lvl, e_end = _level_exponents(env["g"][rows, :], b)
        q = (p_ref[rows, C_Q:C_Q + DK] * (HK ** -0.5)).astype(BF16)
        k = p_ref[rows, C_K:C_K + DK].astype(BF16)
        for lvl in range(N_LEVELS):
          e = jnp.exp2(e_lvl[lvl]).astype(BF16)
          qk[c, 2 * lvl] = q * e
          qk[c, 2 * lvl + 1] = k * e
        qk[c, 2 * N_LEVELS] = q
        qk[c, 2 * N_LEVELS + 1] = k
        qk[c, 2 * N_LEVELS + 2] = q * jnp.exp2(b).astype(BF16)
        qk[c, 2 * N_LEVELS + 3] = k * jnp.exp2(e_end).astype(BF16)
        decays.append(jnp.exp2(b[CH - 1:CH, :]))
      env["decays"] = decays
      env["qk"] = qk
      env["gvn"] = _gv_norm(p_ref, lnw_ref, lnb_ref).astype(BF16)
      env["u"] = _gelu(p_ref[:, C_U:C_U + D])
      env["pending"] = None

    def level_scores(c, hh):
      ck = slice(hh * HK, (hh + 1) * HK)
      scores = None
      for lvl in range(N_LEVELS + 1):
        kt_ref[lvl] = env["qk"][c, 2 * lvl + 1][:, ck].T
        part = mask_ref[lvl] * _dot(env["qk"][c, 2 * lvl][:, ck], kt_ref[lvl])
        scores = part if scores is None else scores + part
      return scores.astype(BF16)

    def finish(c, hh, scores):
      rows = slice(c * CH, (c + 1) * CH)
      ck = slice(hh * HK, (hh + 1) * HK)
      v32 = p_ref[rows, C_V + hh * HV:C_V + (hh + 1) * HV]
      st = st_ref[hh]
      s_ref[hh] = st.T.astype(BF16)
      oa[rows, hh * HV:(hh + 1) * HV] = (
          _dot(scores, v32.astype(BF16))
          + _dot(env["qk"][c, 2 * N_LEVELS + 2][:, ck], s_ref[hh]))
      st_ref[hh] = (st * env["decays"][c][:, ck]
                    + _dot(v32.T.astype(BF16), env["qk"][c, 2 * N_LEVELS + 3][:, ck]))

    def gla(c):
      def run():
        for hh in range(HEADS):
          scores = level_scores(c, hh)
          if env["pending"] is not None:
            finish(*env["pending"])
          env["pending"] = (c, hh, scores)
      return run

    def gmlp():
      for gg in range(GROUPS):
        ws = jnp.where(si <= ti, ws_ref[gg], 0.0).astype(BF16)
        cols = slice(gg * GC, (gg + 1) * GC)
        for c in range(tm // GM_CHUNK):
          rows = slice(c * GM_CHUNK, (c + 1) * GM_CHUNK)
          mx[rows, cols] = _dot(ws, env["gvn"][rows, cols]) + bs_ref[:, cols]
      finish(*env["pending"])

    def branches():
      env["yb"] = _dot((env["u"] * mx[...]).astype(BF16), w_pb_ref[...])
      env["ya"] = _dot(_head_norm_gate(oa[...], p_ref, gnw_ref).astype(BF16), w_pa_ref[...])

    def out():
      ga = p_ref[:, C_GA:C_GA + D]
      gb = p_ref[:, C_GB:C_GB + D]
      merged = (jax.nn.sigmoid(ga) * env["ya"] + jax.nn.sigmoid(gb) * env["yb"]).astype(BF16)
      for r0 in range(0, tm, tm // 2):
        sl = slice(r0, r0 + tm // 2)
        o_ref[sl, :] = x_ref[sl, :] + g2_ref[...] * _dot(merged[sl, :], w_o_ref[...])

    return gate, cumsum, prep, [gla(c) for c in range(tm // CH)], gmlp, branches, out

  gate, cumsum, prep, glas, gmlp, branches, out = mix_stages()
  halves = []
  for r0 in range(0, tm, tm // 2):
    sl = slice(r0, r0 + tm // 2)
    halves.append(_norm_mod(x_ref[sl, :], nw_ref[...], sc_ref[...], sh_ref[...]))
    idx, lo, hi, dst = PROJ_QK
    p_ref[sl, dst:dst + hi - lo] = _dot(halves[-1], w_refs[idx][:, lo:hi])
  h = jnp.concatenate(halves, axis=0)
  proj = lambda piece: _project_piece(h, w_refs, p_ref, piece)
  proj(PROJ_ALR)
  proj(PROJ_UGV)
  gate()
  proj(PROJ_V)
  cumsum()
  proj(PROJ_R)
  proj(PROJ_GATES)
  prep()
  for gla_chunk in glas:
    gla_chunk()
  gmlp()
  branches()
  out()

  @pl.when(j == pl.num_programs(1) - 1)
  def _():
    for hh in range(HEADS):
      s_out_ref[hh] = st_ref[hh].T


def _mix_prompt_call(x, mod3, nw, w_in3, w_a2, b_a, ltri, masks, gnw, w_pa, lnw, lnb, ws, bs_full,
                     w_pb, w_o, *, batch, seq):
  tm = TM_P
  tiles = seq // tm
  row_spec = pl.BlockSpec((tm, D), lambda b, j: (b * tiles + j, 0))

  def mod_spec(seg):
    return pl.BlockSpec((None, 1, D), lambda b, j: (b, 0, seg))

  wa, walr, wb = w_in3
  in_specs = [
      row_spec, mod_spec(3), mod_spec(4), mod_spec(5),
      _const_spec((1, D)), _const_spec(wa.shape), _const_spec(walr.shape), _const_spec(wb.shape),
      _const_spec((LANE, DK)), _const_spec((1, DK)),
      _const_spec(ltri.shape), _const_spec(masks.shape), _const_spec((1, DV)),
      _const_spec((DV, D)), _const_spec((1, D)), _const_spec((1, D)),
      _const_spec((GROUPS, GM_CHUNK, GM_CHUNK)), _const_spec((GM_CHUNK, D)),
      _const_spec((D, D)), _const_spec((D, D)),
  ]
  out_specs = [
      row_spec,
      pl.BlockSpec((None, None, HEADS, HK, HV), lambda b, j: (0, b, 0, 0, 0)),
  ]
  return pl.pallas_call(
      _mix_prompt_kernel,
      grid=(batch, tiles),
      in_specs=in_specs,
      out_specs=out_specs,
      out_shape=[jax.ShapeDtypeStruct((batch * seq, D), F32),
                 jax.ShapeDtypeStruct((1, batch, HEADS, HK, HV), F32)],
      scratch_shapes=[
          pltpu.VMEM((HEADS, HV, HK), F32),
          pltpu.VMEM((tm, D_P), F32),
          pltpu.VMEM((tm, DV), F32),
          pltpu.VMEM((tm, D), F32),
          pltpu.VMEM((N_LEVELS + 1, HK, CH), BF16),
          pltpu.VMEM((HEADS, HK, HV), BF16),
      ],
      compiler_params=pltpu.CompilerParams(
          dimension_semantics=("arbitrary", "arbitrary"), vmem_limit_bytes=VMEM_LIMIT),
      name="mix_prompt",
  )(x, mod3, mod3, mod3, nw, wa, walr, wb, w_a2, b_a, ltri, masks, gnw, w_pa,
    lnw, lnb, ws, bs_full, w_pb, w_o)


def _mix_sample_kernel(x_ref, sh_ref, sc_ref, g2_ref, nw_ref, wa_ref, walr_ref, wb_ref,
                       w_a2_ref, b_a_ref, gnw_ref, w_pa_ref, lnw_ref, lnb_ref, ws0_ref, bs0_ref,
                       w_pb_ref, w_o_ref, s_ref, o_ref, s_out_ref, gvn_ref,
                       p_ref, oa_ref, bc_ref, kt_ref, v_ref):
  i = pl.program_id(0)
  rows = x_ref.shape[0]

  @pl.when(i == 0)
  def _():
    h = _norm_mod(x_ref[...], nw_ref[...], sc_ref[...], sh_ref[...])
    for piece in PROJ_ALL:
      _project_piece(h, (wa_ref, walr_ref, wb_ref), p_ref, piece)
    g = _log_decay(p_ref, w_a2_ref, b_a_ref)
    q = p_ref[:, C_Q:C_Q + DK] * (HK ** -0.5)
    k = p_ref[:, C_K:C_K + DK]
    eb = jnp.exp(g)
    eb_hi = eb.astype(BF16)
    eb_lo = (eb - eb_hi.astype(F32)).astype(BF16)
    qe = q * eb
    for hh in range(HEADS):
      ck = slice(hh * HK, (hh + 1) * HK)
      bc_ref[hh, 0:HK, :] = eb_hi[:, ck].astype(F32).T.astype(BF16)
      bc_ref[hh, HK:2 * HK, :] = eb_lo[:, ck].astype(F32).T.astype(BF16)
      bc_ref[hh, 2 * HK:3 * HK, :] = qe[:, ck].T.astype(BF16)
      kt_ref[hh] = k[:, ck].T.astype(BF16)
      vh = p_ref[:, C_V + hh * HV:C_V + (hh + 1) * HV]
      v_ref[hh] = vh.astype(BF16)
      qk = jnp.sum(q[:, ck] * k[:, ck], axis=-1, keepdims=True)
      oa_ref[:, hh * HV:(hh + 1) * HV] = qk * vh

  row_id = lax.broadcasted_iota(jnp.int32, (rows, HV), 0)
  for nl in range(NB):
    n = i * NB + nl
    hot = row_id == n
    one_hot = jnp.where(hot, 1.0, 0.0).astype(BF16)
    for hh in range(HEADS):
      bc = _dot(bc_ref[hh], one_hot)
      decay = bc[0:HK] + bc[HK:2 * HK]
      qe_b = bc[2 * HK:3 * HK]
      kv = _dot(kt_ref[hh], jnp.where(hot, v_ref[hh], jnp.zeros((), BF16)))
      st = s_ref[nl, hh]
      o_inter = jnp.sum(qe_b * st, axis=0, keepdims=True)
      cur = oa_ref[pl.ds(n, 1), hh * HV:(hh + 1) * HV]
      oa_ref[pl.ds(n, 1), hh * HV:(hh + 1) * HV] = cur + o_inter
      s_out_ref[nl, hh] = decay * st + kv

  @pl.when(i == pl.num_programs(0) - 1)
  def _():
    ya = _dot(_head_norm_gate(oa_ref[...], p_ref, gnw_ref).astype(BF16), w_pa_ref[...])
    gvn = _gv_norm(p_ref, lnw_ref, lnb_ref)
    gvn_ref[...] = gvn
    mixed = ws0_ref[...] * gvn + bs0_ref[...]
    u = _gelu(p_ref[:, C_U:C_U + D])
    yb = _dot((u * mixed).astype(BF16), w_pb_ref[...])
    o_ref[...] = _merge_out(x_ref[...], g2_ref[...], ya, yb, p_ref, w_o_ref)


def _mix_sample_call(x, mod, nw, w_in3, w_a2, b_a, gnw, w_pa, lnw, lnb, ws0, bs0, w_pb, w_o, state):
  rows = x.shape[0]

  def mod_spec(seg):
    return pl.BlockSpec((rows, D), lambda i: (0, seg), pipeline_mode=pl.Buffered(1))

  wa, walr, wb = w_in3
  state_spec = pl.BlockSpec((None, NB, HEADS, HK, HV), lambda i: (0, i, 0, 0, 0))
  in_specs = [
      _const_spec((rows, D)), mod_spec(3), mod_spec(4), mod_spec(5),
      _const_spec((1, D)), _const_spec(wa.shape), _const_spec(walr.shape), _const_spec(wb.shape),
      _const_spec((LANE, DK)), _const_spec((1, DK)),
      _const_spec((1, DV)), _const_spec((DV, D)), _const_spec((1, D)), _const_spec((1, D)),
      _const_spec((1, D)), _const_spec((1, D)), _const_spec((D, D)), _const_spec((D, D)),
      state_spec,
  ]
  out_specs = [
      pl.BlockSpec((rows, D), lambda i: (0, 0)),
      state_spec,
      pl.BlockSpec((rows, D), lambda i: (0, 0)),
  ]
  return pl.pallas_call(
      _mix_sample_kernel,
      grid=(rows // NB,),
      in_specs=in_specs,
      out_specs=out_specs,
      out_shape=[jax.ShapeDtypeStruct((rows, D), F32),
                 jax.ShapeDtypeStruct(state.shape, F32),
                 jax.ShapeDtypeStruct((rows, D), F32)],
      scratch_shapes=[
          pltpu.VMEM((rows, D_P), F32),
          pltpu.VMEM((rows, DV), F32),
          pltpu.VMEM((HEADS, 3 * HK, rows), BF16),
          pltpu.VMEM((HEADS, HK, rows), BF16),
          pltpu.VMEM((HEADS, rows, HV), BF16),
      ],
      compiler_params=pltpu.CompilerParams(
          dimension_semantics=("arbitrary",), vmem_limit_bytes=VMEM_LIMIT),
      name="mix_sample",
  )(x, mod, mod, mod, nw, wa, walr, wb, w_a2, b_a, gnw, w_pa, lnw, lnb, ws0, bs0, w_pb, w_o, state)


def kernel(x_prompt, x_sample, state_gla, c_prompt, c_sample, w_ada, b_ada, norm1_w, ffn1_w13, ffn1_w2, norm2_w, w_in, w_a2, b_a, gla_norm_w, w_pa, gm_ln_w, gm_ln_b, gm_ws, gm_bs, w_pb, w_o, norm3_w, ffn2_w13, ffn2_w2, normf_w):
  batch, seq, _ = x_prompt.shape
  dec = x_sample.shape[0]
  depth = w_ada.shape[0]
  assert depth == 1 and x_sample.shape[1] == 1 and dec % NB == 0
  assert seq % TM_P == 0 and seq % (2 * TM_FFN) == 0

  ltri_np, masks_np = _gla_constants()
  ltri = jnp.asarray(ltri_np, BF16)
  masks = jnp.asarray(masks_np, F32)

  xp = x_prompt.reshape(batch * seq, D)
  xs = x_sample.reshape(dec, D)
  l = 0
  mod_all, w13_1, w2_1 = _ada_call(jnp.concatenate([c_sample, c_prompt], axis=0), w_ada[l],
                                   b_ada[l].reshape(1, N_MOD * D), ffn1_w13[l], ffn1_w2[l])
  mod_s = mod_all
  mod_p = mod_all[dec:].reshape(batch, 1, N_MOD * D)

  row = lambda a: a.reshape(1, -1)
  w_a2_p = jnp.pad(w_a2[l], ((0, LANE - LOWRANK), (0, 0))).astype(BF16)
  gnw = gla_norm_w[l].reshape(1, DV)
  bs_full = jnp.repeat(gm_bs[l].T, GC, axis=1)
  ws0 = jnp.repeat(gm_ws[l][:, 0, 0], GC).reshape(1, D)
  bs0 = bs_full[0:1]
  normf = row(normf_w)

  cast_weights = (jnp.transpose(w_in[l]), w_pa[l], w_pb[l], w_o[l], ffn2_w13[l], ffn2_w2[l])
  (xp, xs, wa, walr, wb, w_pa_b, w_pb_b, w_o_b, w13_2, w2_2) = _ffn_call(
      xp, xs, mod_p, mod_s, (0, 1, 2), row(norm1_w[l]), w13_1, w2_1, None, cast_weights,
      rows_per_batch=seq, tm=TM_FFN, name="ffn1")
  w_in3 = (wa, walr, wb)
  xp, s_prompt = _mix_prompt_call(xp, mod_p, row(norm2_w[l]), w_in3, w_a2_p, row(b_a[l]), ltri, masks,
                                  gnw, w_pa_b, row(gm_ln_w[l]), row(gm_ln_b[l]), gm_ws[l], bs_full,
                                  w_pb_b, w_o_b, batch=batch, seq=seq)
  xs, s_sample, gvn = _mix_sample_call(xs, mod_s, row(norm2_w[l]), w_in3, w_a2_p, row(b_a[l]), gnw,
                                       w_pa_b, row(gm_ln_w[l]), row(gm_ln_b[l]), ws0, bs0,
                                       w_pb_b, w_o_b, state_gla)
  yp, ys = _ffn_call(xp, xs, mod_p, mod_s, (6, 7, 8), row(norm3_w[l]), w13_2, w2_2, normf, None,
                     rows_per_batch=seq, tm=2 * TM_FFN, name="ffn2")

  return (yp.reshape(batch, seq, D), ys.reshape(dec, 1, D), s_prompt, s_sample,
          gvn.reshape(1, dec, 1, D))
```

```python
import functools

import numpy as np
import jax
import jax.numpy as jnp
from jax import lax
from jax.experimental import pallas as pl
from jax.experimental.pallas import tpu as pltpu

F32 = jnp.float32
BF16 = jnp.bfloat16

D = 1024
HEADS = 4
HK = 128
HV = 256
DK = HEADS * HK
DV = HEADS * HV
LOWRANK = 16
TAU = 16.0
GROUPS = 4
GC = D // GROUPS
GM_CHUNK = 128
F = 2816
N_MOD = 9
EPS = 1e-6
LOG2E = 1.4426950408889634

CH = 128
N_LEVELS = 7
LANE = 128

C_Q, C_K, C_ALR, C_U, C_GV, C_V, C_R, C_GA, C_GB = 0, 512, 1024, 1152, 2176, 3200, 4224, 5248, 6272
D_P = C_GB + D

PROJ_QK = (0, 0, 2 * DK, C_Q)
PROJ_ALR = (1, 0, LANE, C_ALR)
PROJ_UGV = (2, 0, 2 * D, C_U)
PROJ_V = (0, 2 * DK, 2 * DK + DV, C_V)
PROJ_R = (0, 2 * DK + DV, 2 * DK + 2 * DV, C_R)
PROJ_GATES = (2, 2 * D, 4 * D, C_GA)
PROJ_ALL = (PROJ_QK, PROJ_ALR, PROJ_UGV, PROJ_V, PROJ_R, PROJ_GATES)

TM_P = 512
TM_FFN = 512
NB = 8
VMEM_LIMIT = 58 * 1024 * 1024


def _gla_constants():
  t = np.arange(CH)
  masks = []
  for lvl in range(N_LEVELS):
    m = 1 << lvl
    upper = t >= (t // (2 * m)) * (2 * m) + m
    same_pair = (t[:, None] // (2 * m)) == (t[None, :] // (2 * m))
    masks.append((upper[:, None] & (~upper)[None, :] & same_pair).astype(np.float32))
  masks.append(np.eye(CH, dtype=np.float32))
  ltri = (t[None, :] <= t[:, None]).astype(np.float32)
  return ltri, np.stack(masks)


def _level_exponents(g, b):
  t = lax.broadcasted_iota(jnp.int32, g.shape, 0)
  g_prev = pltpu.roll(g, 1, 0)
  g_next = pltpu.roll(g, CH - 1, 0)
  r4 = t & 3
  levels = [
      jnp.where((t & 1) == 1, g, 0.0),
      jnp.where(r4 == 0, g_next, jnp.where(r4 == 1, 0.0, jnp.where(r4 == 2, g, g + g_prev))),
  ]
  groups = CH // 8
  row = lambda i: jnp.broadcast_to(b[i:i + 1, :], (8, b.shape[1]))
  b3 = [row(8 * j + 3) for j in range(groups)]
  b7 = [row(8 * j + 7) for j in range(groups)]
  levels.append(-jnp.abs(b - jnp.concatenate(b3, axis=0)))
  for lvl in range(3, N_LEVELS):
    m = 1 << lvl
    ref = [b7[((8 * j) // (2 * m)) * (2 * m) // 8 + m // 8 - 1] for j in range(groups)]
    levels.append(-jnp.abs(b - jnp.concatenate(ref, axis=0)))
  to_end = -jnp.abs(b - jnp.concatenate([b7[groups - 1]] * groups, axis=0))
  return levels, to_end


def _dot(a, b):
  return jnp.dot(a, b, preferred_element_type=F32)


def _rms(x, w):
  ms = jnp.mean(x * x, axis=-1, keepdims=True)
  return x * lax.rsqrt(ms + EPS) * w


def _norm_mod(x, nw, sc, sh):
  if sc.shape[0] != 1:
    return (_rms(x, nw) * (1.0 + sc) + sh).astype(BF16)
  ms = jnp.mean(x * x, axis=-1, keepdims=True)
  return (x * lax.rsqrt(ms + EPS) * (nw * (1.0 + sc)) + sh).astype(BF16)


def _gelu(x):
  return 0.5 * x * (1.0 + lax.erf(x * (2.0 ** -0.5)))


def _log_sigmoid(z):
  return jnp.minimum(z, 0.0) - jnp.log1p(jnp.exp(-jnp.abs(z)))


def _const_spec(shape):
  nd = len(shape)
  return pl.BlockSpec(shape, lambda *_: (0,) * nd, pipeline_mode=pl.Buffered(1))


ADA_STEPS = 4


def _ada_kernel(c_ref, w_ref, b_ref, w13_ref, w2_ref, o_ref, o13_ref, o2_ref):
  c = c_ref[...]
  s = (c * jax.nn.sigmoid(c)).astype(BF16)
  o_ref[...] = _dot(s, w_ref[...].astype(BF16)) + b_ref[...]
  o13_ref[...] = w13_ref[...].astype(BF16)
  o2_ref[...] = w2_ref[...].astype(BF16)


def _ada_call(c_all, w_ada, b_ada, w13, w2):
  rows = c_all.shape[0]
  tn = CAST_MOD_FROM * D // ADA_STEPS
  r13, r2 = D // ADA_STEPS, F // ADA_STEPS
  assert tn % LANE == 0 and r13 % 16 == 0 and r2 % 16 == 0
  blk13 = pl.BlockSpec((r13, 2 * F), lambda i: (i, 0))
  blk2 = pl.BlockSpec((r2, D), lambda i: (i, 0))
  return pl.pallas_call(
      _ada_kernel,
      grid=(ADA_STEPS,),
      in_specs=[
          pl.BlockSpec((rows, D), lambda i: (0, 0)),
          pl.BlockSpec((D, tn), lambda i: (0, i)),
          pl.BlockSpec((1, tn), lambda i: (0, i)),
          blk13, blk2,
      ],
      out_specs=[pl.BlockSpec((rows, tn), lambda i: (0, i)), blk13, blk2],
      out_shape=[jax.ShapeDtypeStruct((rows, CAST_MOD_FROM * D), F32),
                 jax.ShapeDtypeStruct((D, 2 * F), BF16), jax.ShapeDtypeStruct((F, D), BF16)],
      compiler_params=pltpu.CompilerParams(
          dimension_semantics=("arbitrary",), vmem_limit_bytes=VMEM_LIMIT),
      name="ada",
  )(c_all, w_ada, b_ada, w13, w2)


def _ffn_rows(x_ref, sh_ref, sc_ref, g_ref, nw_ref, w13_ref, w2_ref, nf_ref, o_ref, mod_row=None):
  rows = x_ref.shape[0]
  step = min(rows, TM_FFN)
  for r0 in range(0, rows, step):
    sl = slice(r0, r0 + step)
    pick = (lambda ref: ref[sl, :]) if mod_row is None else (lambda ref: ref[pl.ds(mod_row, 1), :])
    sh, sc, g = pick(sh_ref), pick(sc_ref), pick(g_ref)
    x = x_ref[sl, :]
    h = _norm_mod(x, nw_ref[...], sc, sh)
    a = _dot(h, w13_ref[:, :F])
    b = _dot(h, w13_ref[:, F:])
    p = (a * jax.nn.sigmoid(a) * b).astype(BF16)
    out = x + 0.5 * g * _dot(p, w2_ref[...])
    if nf_ref is not None:
      out = _rms(out, nf_ref[...])
    o_ref[sl, :] = out


def _ffn_kernel(x_ref, sh_ref, sc_ref, g_ref, xs_ref, shs_ref, scs_ref, gs_ref, nw_ref, w13_ref,
                w2_ref, *rest, final_norm, cast, tiles):
  rest = list(rest)
  nf_ref = rest.pop(0) if final_norm else None
  cast_in = [rest.pop(0) for _ in range(11)] if cast else None
  o_ref, os_ref = rest[0], rest[1]
  i = pl.program_id(0)
  n = pl.num_programs(0) - 1

  @pl.when(i < n)
  def _():
    _ffn_rows(x_ref, sh_ref, sc_ref, g_ref, nw_ref, w13_ref, w2_ref, nf_ref, o_ref,
              mod_row=jnp.minimum(i, n - 1) // tiles)
    if cast:
      (wta_ref, wtalr_ref, wtb_ref, w_pa_ref, w_pb_ref, w_o_ref, n13_ref, n2_ref,
       c_ref, wada_ref, bada_ref) = cast_in
      wa_ref, walr_ref, wb_ref, pa_ref, pb_ref, po_ref, o13_ref, o2_ref, modb_ref = rest[2:]
      c = c_ref[...]
      modb_ref[...] = _dot((c * jax.nn.sigmoid(c)).astype(BF16),
                           wada_ref[...].astype(BF16)) + bada_ref[...]
      wa_ref[...] = wta_ref[...].T.astype(BF16)
      wb_ref[...] = wtb_ref[...].T.astype(BF16)
      alr = jnp.concatenate([wtalr_ref[...], jnp.zeros((LANE - LOWRANK, D), F32)], axis=0)
      walr_ref[...] = alr.T.astype(BF16)
      pa_ref[...] = w_pa_ref[...].astype(BF16)
      pb_ref[...] = w_pb_ref[...].astype(BF16)
      po_ref[...] = w_o_ref[...].astype(BF16)
      o13_ref[...] = n13_ref[...].astype(BF16)
      o2_ref[...] = n2_ref[...].astype(BF16)

  @pl.when(i == n)
  def _():
    _ffn_rows(xs_ref, shs_ref, scs_ref, gs_ref, nw_ref, w13_ref, w2_ref, nf_ref, os_ref)


CAST_STEPS = 32
CAST_MOD_FROM = 3
CAST_MOD_COLS = 256
W2_CAST_ROWS = 2 * F // CAST_STEPS


def _mod_specs(rows_s, batch, segs):
  assert rows_s % batch == 0
  one = pl.Buffered(1)
  prompt = [pl.BlockSpec((batch, D), lambda *_, s=s: (rows_s // batch, s), pipeline_mode=one)
            for s in segs]
  sample = [pl.BlockSpec((rows_s, D), lambda *_, s=s: (0, s), pipeline_mode=one) for s in segs]
  return prompt, sample


def _ffn_call(x, xs, mod, segs, nw, w13, w2, nf, cast_weights, *, rows_per_batch, tm, name):
  rows, rows_s = x.shape[0], xs.shape[0]
  n = rows // tm
  tiles = rows_per_batch // tm
  last = lambda i: jnp.minimum(i, n - 1)
  x_spec = pl.BlockSpec((tm, D), lambda i: (last(i), 0))
  xs_spec = pl.BlockSpec((rows_s, D), lambda i: (0, 0))
  mod_prompt, mod_sample = _mod_specs(rows_s, rows // rows_per_batch, segs)
  in_specs = [x_spec] + mod_prompt + [_const_spec((rows_s, D))] + mod_sample
  in_specs += [_const_spec((1, D)), _const_spec((D, 2 * F)), _const_spec((F, D))]
  args = [x, mod, mod, mod, xs, mod, mod, mod, nw, w13, w2]
  if nf is not None:
    in_specs.append(_const_spec((1, D)))
    args.append(nf)
  out_specs = [x_spec, xs_spec]
  out_shape = [jax.ShapeDtypeStruct((rows, D), F32), jax.ShapeDtypeStruct((rows_s, D), F32)]
  if cast_weights is not None:
    assert n == CAST_STEPS and D % CAST_STEPS == 0
    w_t, w_pa, w_pb, w_o, n13, n2, c_all, w_ada, b_ada = cast_weights
    mod_cols = (N_MOD - CAST_MOD_FROM) * D
    assert mod_cols % CAST_MOD_COLS == 0 and mod_cols // CAST_MOD_COLS <= CAST_STEPS
    mod_blk = lambda i: jnp.minimum(i, mod_cols // CAST_MOD_COLS - 1)
    mod_off = CAST_MOD_FROM * D // CAST_MOD_COLS
    lo = 2 * DK + 2 * DV
    n_a, n_b = lo // LANE, (w_t.shape[0] - lo - LOWRANK) // LANE
    assert n_a <= CAST_STEPS and n_b == CAST_STEPS
    rb = D // CAST_STEPS
    step = lambda i: jnp.minimum(i, CAST_STEPS - 1)
    a_blk = lambda i: jnp.minimum(i, n_a - 1)
    elem = lambda r: (pl.Element(r), pl.Element(D))
    row_blk = lambda cols: pl.BlockSpec((rb, cols), lambda i: (step(i), 0))
    w2_blk = pl.BlockSpec((W2_CAST_ROWS, D), lambda i: (step(i) // 2, 0))
    in_specs += [
        pl.BlockSpec(elem(LANE), lambda i: (pl.multiple_of(LANE * a_blk(i), LANE), 0)),
        pl.BlockSpec(elem(LOWRANK), lambda i: (lo, 0)),
        pl.BlockSpec(elem(LANE), lambda i: (pl.multiple_of(lo + LOWRANK + LANE * step(i), LOWRANK), 0)),
        row_blk(D), row_blk(D), row_blk(D), row_blk(2 * F), w2_blk,
        _const_spec(c_all.shape),
        pl.BlockSpec((D, CAST_MOD_COLS), lambda i: (0, mod_off + mod_blk(i))),
        pl.BlockSpec((1, CAST_MOD_COLS), lambda i: (0, mod_off + mod_blk(i)))]
    args += [w_t, w_t, w_t, w_pa, w_pb, w_o, n13, n2, c_all, w_ada, b_ada]
    out_specs += [
        pl.BlockSpec((D, LANE), lambda i: (0, a_blk(i))),
        pl.BlockSpec((D, LANE), lambda i: (0, 0)),
        pl.BlockSpec((D, LANE), lambda i: (0, step(i))),
        row_blk(D), row_blk(D), row_blk(D), row_blk(2 * F), w2_blk,
        pl.BlockSpec((c_all.shape[0], CAST_MOD_COLS), lambda i: (0, mod_blk(i)))]
    bf = lambda r, c: jax.ShapeDtypeStruct((r, c), BF16)
    out_shape += [bf(D, lo), bf(D, LANE), bf(D, n_b * LANE),
                  bf(D, D), bf(D, D), bf(D, D), bf(D, 2 * F), bf(F, D),
                  jax.ShapeDtypeStruct((c_all.shape[0], mod_cols), F32)]
  return pl.pallas_call(
      functools.partial(_ffn_kernel, final_norm=nf is not None, cast=cast_weights is not None,
                        tiles=tiles),
      grid=(n + 1,),
      in_specs=in_specs,
      out_specs=out_specs,
      out_shape=out_shape,
      compiler_params=pltpu.CompilerParams(
          dimension_semantics=("arbitrary",), vmem_limit_bytes=VMEM_LIMIT),
      name=name,
  )(*args)


def _project_piece(h, w_refs, p_ref, piece):
  idx, lo, hi, dst = piece
  p_ref[:, dst:dst + hi - lo] = _dot(h, w_refs[idx][:, lo:hi])


def _log_decay(p_ref, w_a2_ref, b_a_ref):
  a_lr = p_ref[:, C_ALR:C_ALR + LANE].astype(BF16)
  z = _dot(a_lr, w_a2_ref[...]) + b_a_ref[...]
  return _log_sigmoid(z) * (1.0 / TAU)


def _head_norm_gate(o, p_ref, gnw_ref):
  parts = []
  for hh in range(HEADS):
    sl = slice(hh * HV, (hh + 1) * HV)
    parts.append(_rms(o[:, sl], gnw_ref[:, sl]))
  r = p_ref[:, C_R:C_R + DV]
  return jnp.concatenate(parts, axis=-1) * (r * jax.nn.sigmoid(r))


def _gv_norm(p_ref, lnw_ref, lnb_ref):
  gv = _gelu(p_ref[:, C_GV:C_GV + D])
  mu = jnp.mean(gv, axis=-1, keepdims=True)
  d = gv - mu
  var = jnp.mean(d * d, axis=-1, keepdims=True)
  return d * lax.rsqrt(var + EPS) * lnw_ref[...] + lnb_ref[...]


def _merge_out(x, g2, ya, yb, p_ref, w_o_ref):
  ga = p_ref[:, C_GA:C_GA + D]
  gb = p_ref[:, C_GB:C_GB + D]
  merged = (jax.nn.sigmoid(ga) * ya + jax.nn.sigmoid(gb) * yb).astype(BF16)
  return x + g2 * _dot(merged, w_o_ref[...])


def _mix_prompt_kernel(x_ref, sh_ref, sc_ref, g2_ref, nw_ref,
                       wa_ref, walr_ref, wb_ref, w_a2_ref, b_a_ref,
                       ltri_ref, mask_ref, gnw_ref, w_pa_ref, lnw_ref, lnb_ref, ws_ref, bs_ref,
                       w_pb_ref, w_o_ref, o_ref, s_out_ref,
                       st_ref, p_ref, oa_ref, mx_ref, kt_ref, s_ref):
  j = pl.program_id(1)
  tm = TM_P
  w_refs = (wa_ref, walr_ref, wb_ref)
  batch_row = pl.ds(pl.program_id(0), 1)
  sh, sc, g2 = sh_ref[batch_row, :], sc_ref[batch_row, :], g2_ref[batch_row, :]

  @pl.when(j == 0)
  def _():
    st_ref[...] = jnp.zeros_like(st_ref)

  ltri = ltri_ref[...]
  ti = lax.broadcasted_iota(jnp.int32, (GM_CHUNK, GM_CHUNK), 0)
  si = lax.broadcasted_iota(jnp.int32, (GM_CHUNK, GM_CHUNK), 1)

  def mix_stages():
    oa = oa_ref
    mx = mx_ref
    env = {}

    def gate():
      env["g"] = _log_decay(p_ref, w_a2_ref, b_a_ref) * LOG2E

    def cumsum():
      cums = []
      for c in range(tm // CH):
        gc = env["g"][c * CH:(c + 1) * CH, :]
        g_hi = gc.astype(BF16)
        g_lo = (gc - g_hi.astype(F32)).astype(BF16)
        cums.append(_dot(ltri, g_hi) + _dot(ltri, g_lo))
      env["cums"] = cums

    def prep():
      decays = []
      qk = {}
      for c in range(tm // CH):
        rows = slice(c * CH, (c + 1) * CH)
        b = env["cums"][c]
        e_lvl, e_end = _level_exponents(env["g"][rows, :], b)
        q = (p_ref[rows, C_Q:C_Q + DK] * (HK ** -0.5)).astype(BF16)
        k = p_ref[rows, C_K:C_K + DK].astype(BF16)
        for lvl in range(N_LEVELS):
          e = jnp.exp2(e_lvl[lvl]).astype(BF16)
          qk[c, 2 * lvl] = q * e
          qk[c, 2 * lvl + 1] = k * e
        qk[c, 2 * N_LEVELS] = q
        qk[c, 2 * N_LEVELS + 1] = k
        qk[c, 2 * N_LEVELS + 2] = q * jnp.exp2(b).astype(BF16)
        qk[c, 2 * N_LEVELS + 3] = k * jnp.exp2(e_end).astype(BF16)
        decays.append(jnp.exp2(b[CH - 1:CH, :]))
      env["decays"] = decays
      env["qk"] = qk
      env["gvn"] = _gv_norm(p_ref, lnw_ref, lnb_ref).astype(BF16)
      env["u"] = _gelu(p_ref[:, C_U:C_U + D])
      env["pending"] = None

    def level_scores(c, hh):
      ck = slice(hh * HK, (hh + 1) * HK)
      scores = None
      for lvl in range(N_LEVELS + 1):
        kt_ref[lvl] = env["qk"][c, 2 * lvl + 1][:, ck].T
        part = mask_ref[lvl] * _dot(env["qk"][c, 2 * lvl][:, ck], kt_ref[lvl])
        scores = part if scores is None else scores + part
      return scores.astype(BF16)

    def finish(c, hh, scores):
      rows = slice(c * CH, (c + 1) * CH)
      ck = slice(hh * HK, (hh + 1) * HK)
      v32 = p_ref[rows, C_V + hh * HV:C_V + (hh + 1) * HV]
      st = st_ref[hh]
      s_ref[hh] = st.T.astype(BF16)
      oa[rows, hh * HV:(hh + 1) * HV] = (
          _dot(scores, v32.astype(BF16))
          + _dot(env["qk"][c, 2 * N_LEVELS + 2][:, ck], s_ref[hh]))
      st_ref[hh] = (st * env["decays"][c][:, ck]
                    + _dot(v32.T.astype(BF16), env["qk"][c, 2 * N_LEVELS + 3][:, ck]))

    def gla(c):
      def run():
        for hh in range(HEADS):
          scores = level_scores(c, hh)
          if env["pending"] is not None:
            finish(*env["pending"])
          env["pending"] = (c, hh, scores)
      return run

    def gmlp():
      for gg in range(GROUPS):
        ws = jnp.where(si <= ti, ws_ref[gg], 0.0).astype(BF16)
        cols = slice(gg * GC, (gg + 1) * GC)
        for c in range(tm // GM_CHUNK):
          rows = slice(c * GM_CHUNK, (c + 1) * GM_CHUNK)
          mx[rows, cols] = _dot(ws, env["gvn"][rows, cols]) + bs_ref[:, cols]
      finish(*env["pending"])

    def branches():
      env["yb"] = _dot((env["u"] * mx[...]).astype(BF16), w_pb_ref[...])
      env["ya"] = _dot(_head_norm_gate(oa[...], p_ref, gnw_ref).astype(BF16), w_pa_ref[...])

    def out():
      ga = p_ref[:, C_GA:C_GA + D]
      gb = p_ref[:, C_GB:C_GB + D]
      merged = (jax.nn.sigmoid(ga) * env["ya"] + jax.nn.sigmoid(gb) * env["yb"]).astype(BF16)
      for r0 in range(0, tm, tm // 2):
        sl = slice(r0, r0 + tm // 2)
        o_ref[sl, :] = x_ref[sl, :] + g2 * _dot(merged[sl, :], w_o_ref[...])

    return gate, cumsum, prep, [gla(c) for c in range(tm // CH)], gmlp, branches, out

  gate, cumsum, prep, glas, gmlp, branches, out = mix_stages()
  halves = []
  for r0 in range(0, tm, tm // 2):
    sl = slice(r0, r0 + tm // 2)
    halves.append(_norm_mod(x_ref[sl, :], nw_ref[...], sc, sh))
    idx, lo, hi, dst = PROJ_QK
    p_ref[sl, dst:dst + hi - lo] = _dot(halves[-1], w_refs[idx][:, lo:hi])
  h = jnp.concatenate(halves, axis=0)
  proj = lambda piece: _project_piece(h, w_refs, p_ref, piece)
  proj(PROJ_ALR)
  proj(PROJ_UGV)
  gate()
  proj(PROJ_V)
  cumsum()
  proj(PROJ_R)
  proj(PROJ_GATES)
  prep()
  for gla_chunk in glas:
    gla_chunk()
  gmlp()
  branches()
  out()

  @pl.when(j == pl.num_programs(1) - 1)
  def _():
    for hh in range(HEADS):
      s_out_ref[hh] = st_ref[hh].T


def _mix_prompt_call(x, mod, segs, nw, w_in3, w_a2, b_a, ltri, masks, gnw, w_pa, lnw, lnb, ws,
                     bs_full, w_pb, w_o, *, batch, seq):
  tm = TM_P
  tiles = seq // tm
  row_spec = pl.BlockSpec((tm, D), lambda b, j: (b * tiles + j, 0))

  mod_prompt, _ = _mod_specs(mod.shape[0] - batch, batch, segs)
  wa, walr, wb = w_in3
  in_specs = [
      row_spec, *mod_prompt,
      _const_spec((1, D)), _const_spec(wa.shape), _const_spec(walr.shape), _const_spec(wb.shape),
      _const_spec((LANE, DK)), _const_spec((1, DK)),
      _const_spec(ltri.shape), _const_spec(masks.shape), _const_spec((1, DV)),
      _const_spec((DV, D)), _const_spec((1, D)), _const_spec((1, D)),
      _const_spec((GROUPS, GM_CHUNK, GM_CHUNK)), _const_spec((GM_CHUNK, D)),
      _const_spec((D, D)), _const_spec((D, D)),
  ]
  out_specs = [
      row_spec,
      pl.BlockSpec((None, None, HEADS, HK, HV), lambda b, j: (0, b, 0, 0, 0)),
  ]
  return pl.pallas_call(
      _mix_prompt_kernel,
      grid=(batch, tiles),
      in_specs=in_specs,
      out_specs=out_specs,
      out_shape=[jax.ShapeDtypeStruct((batch * seq, D), F32),
                 jax.ShapeDtypeStruct((1, batch, HEADS, HK, HV), F32)],
      scratch_shapes=[
          pltpu.VMEM((HEADS, HV, HK), F32),
          pltpu.VMEM((tm, D_P), F32),
          pltpu.VMEM((tm, DV), F32),
          pltpu.VMEM((tm, D), F32),
          pltpu.VMEM((N_LEVELS + 1, HK, CH), BF16),
          pltpu.VMEM((HEADS, HK, HV), BF16),
      ],
      compiler_params=pltpu.CompilerParams(
          dimension_semantics=("arbitrary", "arbitrary"), vmem_limit_bytes=VMEM_LIMIT),
      name="mix_prompt",
  )(x, mod, mod, mod, nw, wa, walr, wb, w_a2, b_a, ltri, masks, gnw, w_pa,
    lnw, lnb, ws, bs_full, w_pb, w_o)


def _mix_sample_kernel(x_ref, sh_ref, sc_ref, g2_ref, nw_ref, wa_ref, walr_ref, wb_ref,
                       w_a2_ref, b_a_ref, gnw_ref, w_pa_ref, lnw_ref, lnb_ref, ws0_ref, bs0_ref,
                       w_pb_ref, w_o_ref, s_ref, o_ref, s_out_ref, gvn_ref,
                       p_ref, oa_ref, bc_ref, kt_ref, v_ref):
  i = pl.program_id(0)
  rows = x_ref.shape[0]

  @pl.when(i == 0)
  def _():
    h = _norm_mod(x_ref[...], nw_ref[...], sc_ref[...], sh_ref[...])
    for piece in PROJ_ALL:
      _project_piece(h, (wa_ref, walr_ref, wb_ref), p_ref, piece)
    g = _log_decay(p_ref, w_a2_ref, b_a_ref)
    q = p_ref[:, C_Q:C_Q + DK] * (HK ** -0.5)
    k = p_ref[:, C_K:C_K + DK]
    eb = jnp.exp(g)
    eb_hi = eb.astype(BF16)
    eb_lo = (eb - eb_hi.astype(F32)).astype(BF16)
    qe = q * eb
    for hh in range(HEADS):
      ck = slice(hh * HK, (hh + 1) * HK)
      bc_ref[hh, 0:HK, :] = eb_hi[:, ck].astype(F32).T.astype(BF16)
      bc_ref[hh, HK:2 * HK, :] = eb_lo[:, ck].astype(F32).T.astype(BF16)
      bc_ref[hh, 2 * HK:3 * HK, :] = qe[:, ck].T.astype(BF16)
      kt_ref[hh] = k[:, ck].T.astype(BF16)
      vh = p_ref[:, C_V + hh * HV:C_V + (hh + 1) * HV]
      v_ref[hh] = vh.astype(BF16)
      qk = jnp.sum(q[:, ck] * k[:, ck], axis=-1, keepdims=True)
      oa_ref[:, hh * HV:(hh + 1) * HV] = qk * vh

  row_id = lax.broadcasted_iota(jnp.int32, (rows, HV), 0)
  for nl in range(NB):
    n = i * NB + nl
    hot = row_id == n
    one_hot = jnp.where(hot, 1.0, 0.0).astype(BF16)
    for hh in range(HEADS):
      bc = _dot(bc_ref[hh], one_hot)
      decay = bc[0:HK] + bc[HK:2 * HK]
      qe_b = bc[2 * HK:3 * HK]
      kv = _dot(kt_ref[hh], jnp.where(hot, v_ref[hh], jnp.zeros((), BF16)))
      st = s_ref[nl, hh]
      o_inter = jnp.sum(qe_b * st, axis=0, keepdims=True)
      cur = oa_ref[pl.ds(n, 1), hh * HV:(hh + 1) * HV]
      oa_ref[pl.ds(n, 1), hh * HV:(hh + 1) * HV] = cur + o_inter
      s_out_ref[nl, hh] = decay * st + kv

  @pl.when(i == pl.num_programs(0) - 1)
  def _():
    ya = _dot(_head_norm_gate(oa_ref[...], p_ref, gnw_ref).astype(BF16), w_pa_ref[...])
    gvn = _gv_norm(p_ref, lnw_ref, lnb_ref)
    gvn_ref[...] = gvn
    mixed = ws0_ref[...] * gvn + bs0_ref[...]
    u = _gelu(p_ref[:, C_U:C_U + D])
    yb = _dot((u * mixed).astype(BF16), w_pb_ref[...])
    o_ref[...] = _merge_out(x_ref[...], g2_ref[...], ya, yb, p_ref, w_o_ref)


def _mix_sample_call(x, mod, segs, nw, w_in3, w_a2, b_a, gnw, w_pa, lnw, lnb, ws0, bs0, w_pb, w_o,
                     state):
  rows = x.shape[0]
  _, mod_sample = _mod_specs(rows, mod.shape[0] - rows, segs)
  wa, walr, wb = w_in3
  state_spec = pl.BlockSpec((None, NB, HEADS, HK, HV), lambda i: (0, i, 0, 0, 0))
  in_specs = [
      _const_spec((rows, D)), *mod_sample,
      _const_spec((1, D)), _const_spec(wa.shape), _const_spec(walr.shape), _const_spec(wb.shape),
      _const_spec((LANE, DK)), _const_spec((1, DK)),
      _const_spec((1, DV)), _const_spec((DV, D)), _const_spec((1, D)), _const_spec((1, D)),
      _const_spec((1, D)), _const_spec((1, D)), _const_spec((D, D)), _const_spec((D, D)),
      state_spec,
  ]
  out_specs = [
      pl.BlockSpec((rows, D), lambda i: (0, 0)),
      state_spec,
      pl.BlockSpec((rows, D), lambda i: (0, 0)),
  ]
  return pl.pallas_call(
      _mix_sample_kernel,
      grid=(rows // NB,),
      in_specs=in_specs,
      out_specs=out_specs,
      out_shape=[jax.ShapeDtypeStruct((rows, D), F32),
                 jax.ShapeDtypeStruct(state.shape, F32),
                 jax.ShapeDtypeStruct((rows, D), F32)],
      scratch_shapes=[
          pltpu.VMEM((rows, D_P), F32),
          pltpu.VMEM((rows, DV), F32),
          pltpu.VMEM((HEADS, 3 * HK, rows), BF16),
          pltpu.VMEM((HEADS, HK, rows), BF16),
          pltpu.VMEM((HEADS, rows, HV), BF16),
      ],
      compiler_params=pltpu.CompilerParams(
          dimension_semantics=("arbitrary",), vmem_limit_bytes=VMEM_LIMIT),
      name="mix_sample",
  )(x, mod, mod, mod, nw, wa, walr, wb, w_a2, b_a, gnw, w_pa, lnw, lnb, ws0, bs0, w_pb, w_o, state)


def kernel(x_prompt, x_sample, state_gla, c_prompt, c_sample, w_ada, b_ada, norm1_w, ffn1_w13, ffn1_w2, norm2_w, w_in, w_a2, b_a, gla_norm_w, w_pa, gm_ln_w, gm_ln_b, gm_ws, gm_bs, w_pb, w_o, norm3_w, ffn2_w13, ffn2_w2, normf_w):
  batch, seq, _ = x_prompt.shape
  dec = x_sample.shape[0]
  depth = w_ada.shape[0]
  assert depth == 1 and x_sample.shape[1] == 1 and dec % NB == 0
  assert seq % TM_P == 0 and seq % (2 * TM_FFN) == 0

  ltri_np, masks_np = _gla_constants()
  ltri = jnp.asarray(ltri_np, BF16)
  masks = jnp.asarray(masks_np, F32)

  xp = x_prompt.reshape(batch * seq, D)
  xs = x_sample.reshape(dec, D)
  l = 0
  c_all = jnp.concatenate([c_sample, c_prompt], axis=0)
  b_ada_row = b_ada[l].reshape(1, N_MOD * D)
  mod_a, w13_1, w2_1 = _ada_call(c_all, w_ada[l], b_ada_row, ffn1_w13[l], ffn1_w2[l])

  row = lambda a: a.reshape(1, -1)
  w_a2_p = jnp.pad(w_a2[l], ((0, LANE - LOWRANK), (0, 0))).astype(BF16)
  gnw = gla_norm_w[l].reshape(1, DV)
  bs_full = jnp.repeat(gm_bs[l].T, GC, axis=1)
  ws0 = jnp.repeat(gm_ws[l][:, 0, 0], GC).reshape(1, D)
  bs0 = bs_full[0:1]
  normf = row(normf_w)

  cast_weights = (jnp.transpose(w_in[l]), w_pa[l], w_pb[l], w_o[l], ffn2_w13[l], ffn2_w2[l],
                  c_all, w_ada[l], b_ada_row)
  (xp, xs, wa, walr, wb, w_pa_b, w_pb_b, w_o_b, w13_2, w2_2, mod_b) = _ffn_call(
      xp, xs, mod_a, (0, 1, 2), row(norm1_w[l]), w13_1, w2_1, None, cast_weights,
      rows_per_batch=seq, tm=TM_FFN, name="ffn1")
  w_in3 = (wa, walr, wb)
  mix_segs = tuple(s - CAST_MOD_FROM for s in (3, 4, 5))
  ffn2_segs = tuple(s - CAST_MOD_FROM for s in (6, 7, 8))
  xp, s_prompt = _mix_prompt_call(xp, mod_b, mix_segs, row(norm2_w[l]), w_in3, w_a2_p, row(b_a[l]),
                                  ltri, masks, gnw, w_pa_b, row(gm_ln_w[l]), row(gm_ln_b[l]),
                                  gm_ws[l], bs_full, w_pb_b, w_o_b, batch=batch, seq=seq)
  xs, s_sample, gvn = _mix_sample_call(xs, mod_b, mix_segs, row(norm2_w[l]), w_in3, w_a2_p,
                                       row(b_a[l]), gnw, w_pa_b, row(gm_ln_w[l]), row(gm_ln_b[l]),
                                       ws0, bs0, w_pb_b, w_o_b, state_gla)
  yp, ys = _ffn_call(xp, xs, mod_b, ffn2_segs, row(norm3_w[l]), w13_2, w2_2, normf, None,
                     rows_per_batch=seq, tm=2 * TM_FFN, name="ffn2")

  return (yp.reshape(batch, seq, D), ys.reshape(dec, 1, D), s_prompt, s_sample,
          gvn.reshape(1, dec, 1, D))
```

```python
import functools

import numpy as np
import jax
import jax.numpy as jnp
from jax import lax
from jax.experimental import pallas as pl
from jax.experimental.pallas import tpu as pltpu

F32 = jnp.float32
BF16 = jnp.bfloat16

D = 1024
HEADS = 4
HK = 128
HV = 256
DK = HEADS * HK
DV = HEADS * HV
LOWRANK = 16
TAU = 16.0
GROUPS = 4
GC = D // GROUPS
GM_CHUNK = 128
F = 2816
N_MOD = 9
EPS = 1e-6
LOG2E = 1.4426950408889634

CH = 128
N_LEVELS = 7
LANE = 128

C_Q, C_K, C_ALR, C_U, C_GV, C_V, C_R, C_GA, C_GB = 0, 512, 1024, 1152, 2176, 3200, 4224, 5248, 6272
D_P = C_GB + D

PROJ_QK = (0, 0, 2 * DK, C_Q)
PROJ_ALR = (1, 0, LANE, C_ALR)
PROJ_UGV = (2, 0, 2 * D, C_U)
PROJ_V = (0, 2 * DK, 2 * DK + DV, C_V)
PROJ_R = (0, 2 * DK + DV, 2 * DK + 2 * DV, C_R)
PROJ_GATES = (2, 2 * D, 4 * D, C_GA)
PROJ_ALL = (PROJ_QK, PROJ_ALR, PROJ_UGV, PROJ_V, PROJ_R, PROJ_GATES)

TM_P = 512
TM_FFN = 512
NBS = 4
VMEM_LIMIT = 58 * 1024 * 1024


def _gla_constants():
  t = np.arange(CH)
  masks = []
  for lvl in range(N_LEVELS):
    m = 1 << lvl
    upper = t >= (t // (2 * m)) * (2 * m) + m
    same_pair = (t[:, None] // (2 * m)) == (t[None, :] // (2 * m))
    masks.append((upper[:, None] & (~upper)[None, :] & same_pair).astype(np.float32))
  masks.append(np.eye(CH, dtype=np.float32))
  ltri = (t[None, :] <= t[:, None]).astype(np.float32)
  return ltri, np.stack(masks)


def _level_exponents(g, b):
  t = lax.broadcasted_iota(jnp.int32, g.shape, 0)
  g_prev = pltpu.roll(g, 1, 0)
  g_next = pltpu.roll(g, CH - 1, 0)
  r4 = t & 3
  levels = [
      jnp.where((t & 1) == 1, g, 0.0),
      jnp.where(r4 == 0, g_next, jnp.where(r4 == 1, 0.0, jnp.where(r4 == 2, g, g + g_prev))),
  ]
  groups = CH // 8
  row = lambda i: jnp.broadcast_to(b[i:i + 1, :], (8, b.shape[1]))
  b3 = [row(8 * j + 3) for j in range(groups)]
  b7 = [row(8 * j + 7) for j in range(groups)]
  levels.append(-jnp.abs(b - jnp.concatenate(b3, axis=0)))
  for lvl in range(3, N_LEVELS):
    m = 1 << lvl
    ref = [b7[((8 * j) // (2 * m)) * (2 * m) // 8 + m // 8 - 1] for j in range(groups)]
    levels.append(-jnp.abs(b - jnp.concatenate(ref, axis=0)))
  to_end = -jnp.abs(b - jnp.concatenate([b7[groups - 1]] * groups, axis=0))
  return levels, to_end


def _dot(a, b):
  return jnp.dot(a, b, preferred_element_type=F32)


def _rms(x, w):
  ms = jnp.mean(x * x, axis=-1, keepdims=True)
  return x * lax.rsqrt(ms + EPS) * w


def _norm_mod(x, nw, sc, sh):
  if sc.shape[0] != 1:
    return (_rms(x, nw) * (1.0 + sc) + sh).astype(BF16)
  ms = jnp.mean(x * x, axis=-1, keepdims=True)
  return (x * lax.rsqrt(ms + EPS) * (nw * (1.0 + sc)) + sh).astype(BF16)


def _gelu(x):
  return 0.5 * x * (1.0 + lax.erf(x * (2.0 ** -0.5)))


def _log_sigmoid(z):
  return jnp.minimum(z, 0.0) - jnp.log1p(jnp.exp(-jnp.abs(z)))


def _const_spec(shape):
  nd = len(shape)
  return pl.BlockSpec(shape, lambda *_: (0,) * nd, pipeline_mode=pl.Buffered(1))


ADA_STEPS = 4


def _ada_kernel(c_ref, w_ref, b_ref, w13_ref, w2_ref, o_ref, o13_ref, o2_ref):
  c = c_ref[...]
  s = (c * jax.nn.sigmoid(c)).astype(BF16)
  o_ref[...] = _dot(s, w_ref[...].astype(BF16)) + b_ref[...]
  o13_ref[...] = w13_ref[...].astype(BF16)
  o2_ref[...] = w2_ref[...].astype(BF16)


def _ada_call(c_all, w_ada, b_ada, w13, w2):
  rows = c_all.shape[0]
  tn = CAST_MOD_FROM * D // ADA_STEPS
  r13, r2 = D // ADA_STEPS, F // ADA_STEPS
  assert tn % LANE == 0 and r13 % 16 == 0 and r2 % 16 == 0
  blk13 = pl.BlockSpec((r13, 2 * F), lambda i: (i, 0))
  blk2 = pl.BlockSpec((r2, D), lambda i: (i, 0))
  return pl.pallas_call(
      _ada_kernel,
      grid=(ADA_STEPS,),
      in_specs=[
          pl.BlockSpec((rows, D), lambda i: (0, 0)),
          pl.BlockSpec((D, tn), lambda i: (0, i)),
          pl.BlockSpec((1, tn), lambda i: (0, i)),
          blk13, blk2,
      ],
      out_specs=[pl.BlockSpec((rows, tn), lambda i: (0, i)), blk13, blk2],
      out_shape=[jax.ShapeDtypeStruct((rows, CAST_MOD_FROM * D), F32),
                 jax.ShapeDtypeStruct((D, 2 * F), BF16), jax.ShapeDtypeStruct((F, D), BF16)],
      compiler_params=pltpu.CompilerParams(
          dimension_semantics=("arbitrary",), vmem_limit_bytes=VMEM_LIMIT),
      name="ada",
  )(c_all, w_ada, b_ada, w13, w2)


def _ffn_rows(x_ref, sh_ref, sc_ref, g_ref, nw_ref, w13_ref, w2_ref, nf_ref, o_ref, mod_row=None):
  pick = (lambda ref: ref[...]) if mod_row is None else (lambda ref: ref[pl.ds(mod_row, 1), :])
  x = x_ref[...]
  h = _norm_mod(x, nw_ref[...], pick(sc_ref), pick(sh_ref))
  a = _dot(h, w13_ref[:, :F])
  b = _dot(h, w13_ref[:, F:])
  p = (a * jax.nn.sigmoid(a) * b).astype(BF16)
  out = x + 0.5 * pick(g_ref) * _dot(p, w2_ref[...])
  if nf_ref is not None:
    out = _rms(out, nf_ref[...])
  o_ref[...] = out


def _ffn_kernel(x_ref, sh_ref, sc_ref, g_ref, xs_ref, shs_ref, scs_ref, gs_ref, nw_ref, w13_ref,
                w2_ref, wta_ref, wtalr_ref, wtb_ref, w_pa_ref, w_pb_ref, w_o_ref, n13_ref, n2_ref,
                c_ref, wada_ref, bada_ref,
                o_ref, os_ref, wa_ref, walr_ref, wb_ref, pa_ref, pb_ref, po_ref, o13_ref, o2_ref,
                modb_ref, *, tiles):
  i = pl.program_id(0)
  n = pl.num_programs(0) - 1

  @pl.when(i < n)
  def _():
    _ffn_rows(x_ref, sh_ref, sc_ref, g_ref, nw_ref, w13_ref, w2_ref, None, o_ref,
              mod_row=i // tiles)
    c = c_ref[...]
    modb_ref[...] = _dot((c * jax.nn.sigmoid(c)).astype(BF16),
                         wada_ref[...].astype(BF16)) + bada_ref[...]
    wa_ref[...] = wta_ref[...].T.astype(BF16)
    wb_ref[...] = wtb_ref[...].T.astype(BF16)
    alr = jnp.concatenate([wtalr_ref[...], jnp.zeros((LANE - LOWRANK, D), F32)], axis=0)
    walr_ref[...] = alr.T.astype(BF16)
    pa_ref[...] = w_pa_ref[...].astype(BF16)
    pb_ref[...] = w_pb_ref[...].astype(BF16)
    po_ref[...] = w_o_ref[...].astype(BF16)
    o13_ref[...] = n13_ref[...].astype(BF16)
    o2_ref[...] = n2_ref[...].astype(BF16)

  @pl.when(i == n)
  def _():
    _ffn_rows(xs_ref, shs_ref, scs_ref, gs_ref, nw_ref, w13_ref, w2_ref, None, os_ref)


CAST_STEPS = 32
CAST_MOD_FROM = 3
CAST_MOD_COLS = 256
W2_CAST_ROWS = 2 * F // CAST_STEPS


def _mod_specs(rows_s, batch, segs):
  assert rows_s % batch == 0
  one = pl.Buffered(1)
  prompt = [pl.BlockSpec((batch, D), lambda *_, s=s: (rows_s // batch, s), pipeline_mode=one)
            for s in segs]
  sample = [pl.BlockSpec((rows_s, D), lambda *_, s=s: (0, s), pipeline_mode=one) for s in segs]
  return prompt, sample


def _ffn_call(x, xs, mod, segs, nw, w13, w2, cast_weights, *, rows_per_batch, tm):
  rows, rows_s = x.shape[0], xs.shape[0]
  n = rows // tm
  tiles = rows_per_batch // tm
  last = lambda i: jnp.minimum(i, n - 1)
  x_spec = pl.BlockSpec((tm, D), lambda i: (last(i), 0))
  xs_spec = pl.BlockSpec((rows_s, D), lambda i: (0, 0))
  mod_prompt, mod_sample = _mod_specs(rows_s, rows // rows_per_batch, segs)
  assert n == CAST_STEPS and D % CAST_STEPS == 0
  w_t, w_pa, w_pb, w_o, n13, n2, c_all, w_ada, b_ada = cast_weights
  mod_cols = (N_MOD - CAST_MOD_FROM) * D
  assert mod_cols % CAST_MOD_COLS == 0 and mod_cols // CAST_MOD_COLS <= CAST_STEPS
  mod_blk = lambda i: jnp.minimum(i, mod_cols // CAST_MOD_COLS - 1)
  mod_off = CAST_MOD_FROM * D // CAST_MOD_COLS
  lo = 2 * DK + 2 * DV
  n_a, n_b = lo // LANE, (w_t.shape[0] - lo - LOWRANK) // LANE
  assert n_a <= CAST_STEPS and n_b == CAST_STEPS
  rb = D // CAST_STEPS
  step = lambda i: jnp.minimum(i, CAST_STEPS - 1)
  a_blk = lambda i: jnp.minimum(i, n_a - 1)
  elem = lambda r: (pl.Element(r), pl.Element(D))
  row_blk = lambda cols: pl.BlockSpec((rb, cols), lambda i: (step(i), 0))
  w2_blk = pl.BlockSpec((W2_CAST_ROWS, D), lambda i: (step(i) // 2, 0))
  in_specs = [x_spec] + mod_prompt + [_const_spec((rows_s, D))] + mod_sample + [
      _const_spec((1, D)), _const_spec((D, 2 * F)), _const_spec((F, D)),
      pl.BlockSpec(elem(LANE), lambda i: (pl.multiple_of(LANE * a_blk(i), LANE), 0)),
      pl.BlockSpec(elem(LOWRANK), lambda i: (lo, 0)),
      pl.BlockSpec(elem(LANE), lambda i: (pl.multiple_of(lo + LOWRANK + LANE * step(i), LOWRANK), 0)),
      row_blk(D), row_blk(D), row_blk(D), row_blk(2 * F), w2_blk,
      _const_spec(c_all.shape),
      pl.BlockSpec((D, CAST_MOD_COLS), lambda i: (0, mod_off + mod_blk(i))),
      pl.BlockSpec((1, CAST_MOD_COLS), lambda i: (0, mod_off + mod_blk(i)))]
  out_specs = [
      x_spec, xs_spec,
      pl.BlockSpec((D, LANE), lambda i: (0, a_blk(i))),
      pl.BlockSpec((D, LANE), lambda i: (0, 0)),
      pl.BlockSpec((D, LANE), lambda i: (0, step(i))),
      row_blk(D), row_blk(D), row_blk(D), row_blk(2 * F), w2_blk,
      pl.BlockSpec((c_all.shape[0], CAST_MOD_COLS), lambda i: (0, mod_blk(i)))]
  bf = lambda r, c: jax.ShapeDtypeStruct((r, c), BF16)
  out_shape = [jax.ShapeDtypeStruct((rows, D), F32), jax.ShapeDtypeStruct((rows_s, D), F32),
               bf(D, lo), bf(D, LANE), bf(D, n_b * LANE),
               bf(D, D), bf(D, D), bf(D, D), bf(D, 2 * F), bf(F, D),
               jax.ShapeDtypeStruct((c_all.shape[0], mod_cols), F32)]
  return pl.pallas_call(
      functools.partial(_ffn_kernel, tiles=tiles),
      grid=(n + 1,),
      in_specs=in_specs,
      out_specs=out_specs,
      out_shape=out_shape,
      compiler_params=pltpu.CompilerParams(
          dimension_semantics=("arbitrary",), vmem_limit_bytes=VMEM_LIMIT),
      name="ffn1",
  )(x, mod, mod, mod, xs, mod, mod, mod, nw, w13, w2,
    w_t, w_t, w_t, w_pa, w_pb, w_o, n13, n2, c_all, w_ada, b_ada)


def _project_piece(h, w_refs, p_ref, piece):
  idx, lo, hi, dst = piece
  p_ref[:, dst:dst + hi - lo] = _dot(h, w_refs[idx][:, lo:hi])


def _log_decay(p_ref, w_a2_ref, b_a_ref):
  a_lr = p_ref[:, C_ALR:C_ALR + LANE].astype(BF16)
  z = _dot(a_lr, w_a2_ref[...]) + b_a_ref[...]
  return _log_sigmoid(z) * (1.0 / TAU)


def _head_norm_gate(o, p_ref, gnw_ref):
  parts = []
  for hh in range(HEADS):
    sl = slice(hh * HV, (hh + 1) * HV)
    parts.append(_rms(o[:, sl], gnw_ref[:, sl]))
  r = p_ref[:, C_R:C_R + DV]
  return jnp.concatenate(parts, axis=-1) * (r * jax.nn.sigmoid(r))


def _gv_norm(p_ref, lnw_ref, lnb_ref):
  gv = _gelu(p_ref[:, C_GV:C_GV + D])
  mu = jnp.mean(gv, axis=-1, keepdims=True)
  d = gv - mu
  var = jnp.mean(d * d, axis=-1, keepdims=True)
  return d * lax.rsqrt(var + EPS) * lnw_ref[...] + lnb_ref[...]


def _merge_out(x, g2, ya, yb, p_ref, w_o_ref):
  ga = p_ref[:, C_GA:C_GA + D]
  gb = p_ref[:, C_GB:C_GB + D]
  merged = (jax.nn.sigmoid(ga) * ya + jax.nn.sigmoid(gb) * yb).astype(BF16)
  return x + g2 * _dot(merged, w_o_ref[...])


def _mix_prompt_kernel(x_ref, sh_ref, sc_ref, g2_ref, nw_ref,
                       wa_ref, walr_ref, wb_ref, w_a2_ref, b_a_ref,
                       ltri_ref, mask_ref, gnw_ref, w_pa_ref, lnw_ref, lnb_ref, ws_ref, bs_ref,
                       w_pb_ref, w_o_ref, o_ref, s_out_ref,
                       st_ref, p_ref, oa_ref, mx_ref, kt_ref, s_ref):
  j = pl.program_id(1)
  tm = TM_P
  w_refs = (wa_ref, walr_ref, wb_ref)
  batch_row = pl.ds(pl.program_id(0), 1)
  sh, sc, g2 = sh_ref[batch_row, :], sc_ref[batch_row, :], g2_ref[batch_row, :]

  @pl.when(j == 0)
  def _():
    st_ref[...] = jnp.zeros_like(st_ref)

  ltri = ltri_ref[...]
  ti = lax.broadcasted_iota(jnp.int32, (GM_CHUNK, GM_CHUNK), 0)
  si = lax.broadcasted_iota(jnp.int32, (GM_CHUNK, GM_CHUNK), 1)

  def mix_stages():
    oa = oa_ref
    mx = mx_ref
    env = {}

    def gate():
      env["g"] = _log_decay(p_ref, w_a2_ref, b_a_ref) * LOG2E

    def cumsum():
      cums = []
      for c in range(tm // CH):
        gc = env["g"][c * CH:(c + 1) * CH, :]
        g_hi = gc.astype(BF16)
        g_lo = (gc - g_hi.astype(F32)).astype(BF16)
        cums.append(_dot(ltri, g_hi) + _dot(ltri, g_lo))
      env["cums"] = cums

    def prep():
      decays = []
      qk = {}
      for c in range(tm // CH):
        rows = slice(c * CH, (c + 1) * CH)
        b = env["cums"][c]
        e_lvl, e_end = _level_exponents(env["g"][rows, :], b)
        q = (p_ref[rows, C_Q:C_Q + DK] * (HK ** -0.5)).astype(BF16)
        k = p_ref[rows, C_K:C_K + DK].astype(BF16)
        for lvl in range(N_LEVELS):
          e = jnp.exp2(e_lvl[lvl]).astype(BF16)
          qk[c, 2 * lvl] = q * e
          qk[c, 2 * lvl + 1] = k * e
        qk[c, 2 * N_LEVELS] = q
        qk[c, 2 * N_LEVELS + 1] = k
        qk[c, 2 * N_LEVELS + 2] = q * jnp.exp2(b).astype(BF16)
        qk[c, 2 * N_LEVELS + 3] = k * jnp.exp2(e_end).astype(BF16)
        decays.append(jnp.exp2(b[CH - 1:CH, :]))
      env["decays"] = decays
      env["qk"] = qk
      env["gvn"] = _gv_norm(p_ref, lnw_ref, lnb_ref).astype(BF16)
      env["u"] = _gelu(p_ref[:, C_U:C_U + D])
      env["pending"] = None

    def level_scores(c, hh):
      ck = slice(hh * HK, (hh + 1) * HK)
      scores = None
      for lvl in range(N_LEVELS + 1):
        kt_ref[lvl] = env["qk"][c, 2 * lvl + 1][:, ck].T
        part = mask_ref[lvl] * _dot(env["qk"][c, 2 * lvl][:, ck], kt_ref[lvl])
        scores = part if scores is None else scores + part
      return scores.astype(BF16)

    def finish(c, hh, scores):
      rows = slice(c * CH, (c + 1) * CH)
      ck = slice(hh * HK, (hh + 1) * HK)
      v32 = p_ref[rows, C_V + hh * HV:C_V + (hh + 1) * HV]
      st = st_ref[hh]
      s_ref[hh] = st.T.astype(BF16)
      oa[rows, hh * HV:(hh + 1) * HV] = (
          _dot(scores, v32.astype(BF16))
          + _dot(env["qk"][c, 2 * N_LEVELS + 2][:, ck], s_ref[hh]))
      st_ref[hh] = (st * env["decays"][c][:, ck]
                    + _dot(v32.T.astype(BF16), env["qk"][c, 2 * N_LEVELS + 3][:, ck]))

    def gla(c):
      def run():
        for hh in range(HEADS):
          scores = level_scores(c, hh)
          if env["pending"] is not None:
            finish(*env["pending"])
          env["pending"] = (c, hh, scores)
      return run

    def gmlp():
      for gg in range(GROUPS):
        ws = jnp.where(si <= ti, ws_ref[gg], 0.0).astype(BF16)
        cols = slice(gg * GC, (gg + 1) * GC)
        for c in range(tm // GM_CHUNK):
          rows = slice(c * GM_CHUNK, (c + 1) * GM_CHUNK)
          mx[rows, cols] = _dot(ws, env["gvn"][rows, cols]) + bs_ref[:, cols]
      finish(*env["pending"])

    def branches():
      env["yb"] = _dot((env["u"] * mx[...]).astype(BF16), w_pb_ref[...])
      env["ya"] = _dot(_head_norm_gate(oa[...], p_ref, gnw_ref).astype(BF16), w_pa_ref[...])

    def out():
      ga = p_ref[:, C_GA:C_GA + D]
      gb = p_ref[:, C_GB:C_GB + D]
      merged = (jax.nn.sigmoid(ga) * env["ya"] + jax.nn.sigmoid(gb) * env["yb"]).astype(BF16)
      for r0 in range(0, tm, tm // 2):
        sl = slice(r0, r0 + tm // 2)
        o_ref[sl, :] = x_ref[sl, :] + g2 * _dot(merged[sl, :], w_o_ref[...])

    return gate, cumsum, prep, [gla(c) for c in range(tm // CH)], gmlp, branches, out

  gate, cumsum, prep, glas, gmlp, branches, out = mix_stages()
  halves = []
  for r0 in range(0, tm, tm // 2):
    sl = slice(r0, r0 + tm // 2)
    halves.append(_norm_mod(x_ref[sl, :], nw_ref[...], sc, sh))
    idx, lo, hi, dst = PROJ_QK
    p_ref[sl, dst:dst + hi - lo] = _dot(halves[-1], w_refs[idx][:, lo:hi])
  h = jnp.concatenate(halves, axis=0)
  proj = lambda piece: _project_piece(h, w_refs, p_ref, piece)
  proj(PROJ_ALR)
  proj(PROJ_UGV)
  gate()
  proj(PROJ_V)
  cumsum()
  proj(PROJ_R)
  proj(PROJ_GATES)
  prep()
  for gla_chunk in glas:
    gla_chunk()
  gmlp()
  branches()
  out()

  @pl.when(j == pl.num_programs(1) - 1)
  def _():
    for hh in range(HEADS):
      s_out_ref[hh] = st_ref[hh].T


def _mix_prompt_call(x, mod, segs, nw, w_in3, w_a2, b_a, ltri, masks, gnw, w_pa, lnw, lnb, ws,
                     bs_full, w_pb, w_o, *, batch, seq):
  tm = TM_P
  tiles = seq // tm
  row_spec = pl.BlockSpec((tm, D), lambda b, j: (b * tiles + j, 0))

  mod_prompt, _ = _mod_specs(mod.shape[0] - batch, batch, segs)
  wa, walr, wb = w_in3
  in_specs = [
      row_spec, *mod_prompt,
      _const_spec((1, D)), _const_spec(wa.shape), _const_spec(walr.shape), _const_spec(wb.shape),
      _const_spec((LANE, DK)), _const_spec((1, DK)),
      _const_spec(ltri.shape), _const_spec(masks.shape), _const_spec((1, DV)),
      _const_spec((DV, D)), _const_spec((1, D)), _const_spec((1, D)),
      _const_spec((GROUPS, GM_CHUNK, GM_CHUNK)), _const_spec((GM_CHUNK, D)),
      _const_spec((D, D)), _const_spec((D, D)),
  ]
  out_specs = [
      row_spec,
      pl.BlockSpec((None, None, HEADS, HK, HV), lambda b, j: (0, b, 0, 0, 0)),
  ]
  return pl.pallas_call(
      _mix_prompt_kernel,
      grid=(batch, tiles),
      in_specs=in_specs,
      out_specs=out_specs,
      out_shape=[jax.ShapeDtypeStruct((batch * seq, D), F32),
                 jax.ShapeDtypeStruct((1, batch, HEADS, HK, HV), F32)],
      scratch_shapes=[
          pltpu.VMEM((HEADS, HV, HK), F32),
          pltpu.VMEM((tm, D_P), F32),
          pltpu.VMEM((tm, DV), F32),
          pltpu.VMEM((tm, D), F32),
          pltpu.VMEM((N_LEVELS + 1, HK, CH), BF16),
          pltpu.VMEM((HEADS, HK, HV), BF16),
      ],
      compiler_params=pltpu.CompilerParams(
          dimension_semantics=("arbitrary", "arbitrary"), vmem_limit_bytes=VMEM_LIMIT),
      name="mix_prompt",
  )(x, mod, mod, mod, nw, wa, walr, wb, w_a2, b_a, ltri, masks, gnw, w_pa,
    lnw, lnb, ws, bs_full, w_pb, w_o)


def _sample_proj_kernel(x_ref, sh_ref, sc_ref, nw_ref, wa_ref, walr_ref, wb_ref, w_a2_ref, b_a_ref,
                        p_ref, col_ref, v_ref, oa_ref):
  h = _norm_mod(x_ref[...], nw_ref[...], sc_ref[...], sh_ref[...])
  for piece in PROJ_ALL:
    _project_piece(h, (wa_ref, walr_ref, wb_ref), p_ref, piece)
  g = _log_decay(p_ref, w_a2_ref, b_a_ref)
  q = p_ref[:, C_Q:C_Q + DK] * (HK ** -0.5)
  k = p_ref[:, C_K:C_K + DK]
  eb = jnp.exp(g)
  qe = q * eb
  v_ref[...] = p_ref[:, C_V:C_V + DV]
  for hh in range(HEADS):
    ck = slice(hh * HK, (hh + 1) * HK)
    col_ref[0, ck, :] = eb[:, ck].T
    col_ref[1, ck, :] = qe[:, ck].T
    col_ref[2, ck, :] = k[:, ck].T
    qk = jnp.sum(q[:, ck] * k[:, ck], axis=-1, keepdims=True)
    oa_ref[:, hh * HV:(hh + 1) * HV] = qk * p_ref[:, C_V + hh * HV:C_V + (hh + 1) * HV]


def _sample_state_rows(i, col_ref, v_ref, s_ref, s_out_ref, os_ref):
  rows = os_ref.shape[0]
  shift = (rows - i * NBS) % rows
  eb_t, qe_t, k_t = (pltpu.roll(col_ref[a], shift, 1) for a in range(3))
  for nl in range(NBS):
    n = i * NBS + nl
    for hh in range(HEADS):
      ck = slice(hh * HK, (hh + 1) * HK)
      cv = slice(hh * HV, (hh + 1) * HV)
      st = s_ref[nl, hh]
      o_inter = jnp.sum(qe_t[ck, nl:nl + 1] * st, axis=0, keepdims=True)
      os_ref[pl.ds(n, 1), cv] = os_ref[pl.ds(n, 1), cv] + o_inter
      s_out_ref[nl, hh] = eb_t[ck, nl:nl + 1] * st + k_t[ck, nl:nl + 1] * v_ref[pl.ds(n, 1), cv]


def _ffn_stream_kernel(x_ref, sh_ref, sc_ref, g_ref, nw_ref, w13_ref, w2_ref, nf_ref,
                       col_ref, v_ref, oa0_ref, s_ref,
                       xs_ref, p_ref, g2_ref, shs_ref, scs_ref, gs_ref, gnw_ref, w_pa_ref, lnw_ref,
                       lnb_ref, ws0_ref, bs0_ref, w_pb_ref, w_o_ref,
                       o_ref, os_ref, s_out_ref, ys_ref, gvn_ref, x2_ref, *, tiles):
  i = pl.program_id(0)
  n = pl.num_programs(0) - 1

  @pl.when(i == 0)
  def _():
    os_ref[...] = oa0_ref[...]

  @pl.when(i < n)
  def _():
    _sample_state_rows(i, col_ref, v_ref, s_ref, s_out_ref, os_ref)
    _ffn_rows(x_ref, sh_ref, sc_ref, g_ref, nw_ref, w13_ref, w2_ref, nf_ref, o_ref,
              mod_row=i // tiles)

  @pl.when(i == n)
  def _():
    ya = _dot(_head_norm_gate(os_ref[...], p_ref, gnw_ref).astype(BF16), w_pa_ref[...])
    gvn = _gv_norm(p_ref, lnw_ref, lnb_ref)
    gvn_ref[...] = gvn
    mixed = ws0_ref[...] * gvn + bs0_ref[...]
    u = _gelu(p_ref[:, C_U:C_U + D])
    yb = _dot((u * mixed).astype(BF16), w_pb_ref[...])
    x2_ref[...] = _merge_out(xs_ref[...], g2_ref[...], ya, yb, p_ref, w_o_ref)
    _ffn_rows(x2_ref, shs_ref, scs_ref, gs_ref, nw_ref, w13_ref, w2_ref, nf_ref, ys_ref)


def _sample_proj_call(x, mod, segs, nw, w_in3, w_a2, b_a):
  rows = x.shape[0]
  _, mod_sample = _mod_specs(rows, mod.shape[0] - rows, segs[:2])
  wa, walr, wb = w_in3
  shapes = [(rows, D_P), (3, DK, rows), (rows, DV), (rows, DV)]
  return pl.pallas_call(
      _sample_proj_kernel,
      grid=(1,),
      in_specs=[_const_spec((rows, D)), *mod_sample, _const_spec((1, D)), _const_spec(wa.shape),
                _const_spec(walr.shape), _const_spec(wb.shape), _const_spec((LANE, DK)),
                _const_spec((1, DK))],
      out_specs=[pl.BlockSpec(s, lambda i, nd=len(s): (0,) * nd) for s in shapes],
      out_shape=[jax.ShapeDtypeStruct(s, F32) for s in shapes],
      compiler_params=pltpu.CompilerParams(
          dimension_semantics=("arbitrary",), vmem_limit_bytes=VMEM_LIMIT),
      name="sample_proj",
  )(x, mod, mod, nw, wa, walr, wb, w_a2, b_a)


def _ffn_stream_call(x, mod, segs, nw, w13, w2, nf, cols, v_s, oa0, state,
                     xs, p_s, mix_segs, gnw, w_pa, lnw, lnb, ws0, bs0, w_pb, w_o, *, rows_per_batch, tm):
  rows, rows_s = x.shape[0], xs.shape[0]
  n = rows // tm
  assert n * NBS == rows_s
  last = lambda i: jnp.minimum(i, n - 1)
  x_spec = pl.BlockSpec((tm, D), lambda i: (last(i), 0))
  mod_prompt, mod_sample = _mod_specs(rows_s, rows // rows_per_batch, segs)
  _, mod_g2 = _mod_specs(rows_s, rows // rows_per_batch, mix_segs[2:])
  state_spec = pl.BlockSpec((None, NBS, HEADS, HK, HV), lambda i: (0, last(i), 0, 0, 0))
  row_out = pl.BlockSpec((rows_s, D), lambda i: (0, 0))
  outs = pl.pallas_call(
      functools.partial(_ffn_stream_kernel, tiles=rows_per_batch // tm),
      grid=(n + 1,),
      in_specs=[x_spec, *mod_prompt, _const_spec((1, D)), _const_spec((D, 2 * F)),
                _const_spec((F, D)), _const_spec((1, D)), _const_spec(cols.shape),
                _const_spec((rows_s, DV)), _const_spec((rows_s, DV)), state_spec,
                _const_spec((rows_s, D)), _const_spec((rows_s, D_P)), *mod_g2, *mod_sample,
                _const_spec((1, DV)), _const_spec((DV, D)), _const_spec((1, D)), _const_spec((1, D)),
                _const_spec((1, D)), _const_spec((1, D)), _const_spec((D, D)), _const_spec((D, D))],
      out_specs=[x_spec, pl.BlockSpec((rows_s, DV), lambda i: (0, 0)), state_spec, row_out, row_out],
      out_shape=[jax.ShapeDtypeStruct((rows, D), F32), jax.ShapeDtypeStruct((rows_s, DV), F32),
                 jax.ShapeDtypeStruct(state.shape, F32), jax.ShapeDtypeStruct((rows_s, D), F32),
                 jax.ShapeDtypeStruct((rows_s, D), F32)],
      scratch_shapes=[pltpu.VMEM((rows_s, D), F32)],
      compiler_params=pltpu.CompilerParams(
          dimension_semantics=("arbitrary",), vmem_limit_bytes=VMEM_LIMIT),
      name="ffn2",
  )(x, mod, mod, mod, nw, w13, w2, nf, cols, v_s, oa0, state,
    xs, p_s, mod, mod, mod, mod, gnw, w_pa, lnw, lnb, ws0, bs0, w_pb, w_o)
  y, _, s_new, ys, gvn = outs
  return y, s_new, ys, gvn


def kernel(x_prompt, x_sample, state_gla, c_prompt, c_sample, w_ada, b_ada, norm1_w, ffn1_w13, ffn1_w2, norm2_w, w_in, w_a2, b_a, gla_norm_w, w_pa, gm_ln_w, gm_ln_b, gm_ws, gm_bs, w_pb, w_o, norm3_w, ffn2_w13, ffn2_w2, normf_w):
  batch, seq, _ = x_prompt.shape
  dec = x_sample.shape[0]
  depth = w_ada.shape[0]
  assert depth == 1 and x_sample.shape[1] == 1
  assert seq % TM_P == 0 and seq % TM_FFN == 0

  ltri_np, masks_np = _gla_constants()
  ltri = jnp.asarray(ltri_np, BF16)
  masks = jnp.asarray(masks_np, F32)

  xp = x_prompt.reshape(batch * seq, D)
  xs = x_sample.reshape(dec, D)
  l = 0
  c_all = jnp.concatenate([c_sample, c_prompt], axis=0)
  b_ada_row = b_ada[l].reshape(1, N_MOD * D)
  mod_a, w13_1, w2_1 = _ada_call(c_all, w_ada[l], b_ada_row, ffn1_w13[l], ffn1_w2[l])

  row = lambda a: a.reshape(1, -1)
  w_a2_p = jnp.pad(w_a2[l], ((0, LANE - LOWRANK), (0, 0))).astype(BF16)
  gnw = gla_norm_w[l].reshape(1, DV)
  bs_full = jnp.repeat(gm_bs[l].T, GC, axis=1)
  ws0 = jnp.repeat(gm_ws[l][:, 0, 0], GC).reshape(1, D)
  bs0 = bs_full[0:1]
  normf = row(normf_w)

  cast_weights = (jnp.transpose(w_in[l]), w_pa[l], w_pb[l], w_o[l], ffn2_w13[l], ffn2_w2[l],
                  c_all, w_ada[l], b_ada_row)
  (xp, xs, wa, walr, wb, w_pa_b, w_pb_b, w_o_b, w13_2, w2_2, mod_b) = _ffn_call(
      xp, xs, mod_a, (0, 1, 2), row(norm1_w[l]), w13_1, w2_1, cast_weights,
      rows_per_batch=seq, tm=TM_FFN)
  w_in3 = (wa, walr, wb)
  mix_segs = tuple(s - CAST_MOD_FROM for s in (3, 4, 5))
  ffn2_segs = tuple(s - CAST_MOD_FROM for s in (6, 7, 8))
  xp, s_prompt = _mix_prompt_call(xp, mod_b, mix_segs, row(norm2_w[l]), w_in3, w_a2_p, row(b_a[l]),
                                  ltri, masks, gnw, w_pa_b, row(gm_ln_w[l]), row(gm_ln_b[l]),
                                  gm_ws[l], bs_full, w_pb_b, w_o_b, batch=batch, seq=seq)
  p_s, cols, v_s, oa0 = _sample_proj_call(xs, mod_b, mix_segs, row(norm2_w[l]), w_in3, w_a2_p,
                                          row(b_a[l]))
  yp, s_sample, ys, gvn = _ffn_stream_call(
      xp, mod_b, ffn2_segs, row(norm3_w[l]), w13_2, w2_2, normf, cols, v_s, oa0, state_gla,
      xs, p_s, mix_segs, gnw, w_pa_b, row(gm_ln_w[l]), row(gm_ln_b[l]), ws0, bs0, w_pb_b, w_o_b,
      rows_per_batch=seq, tm=TM_FFN)

  return (yp.reshape(batch, seq, D), ys.reshape(dec, 1, D), s_prompt, s_sample,
          gvn.reshape(1, dec, 1, D))
```

```python
import functools

import numpy as np
import jax
import jax.numpy as jnp
from jax import lax
from jax.experimental import pallas as pl
from jax.experimental.pallas import tpu as pltpu

F32 = jnp.float32
BF16 = jnp.bfloat16

D = 1024
HEADS = 4
HK = 128
HV = 256
DK = HEADS * HK
DV = HEADS * HV
LOWRANK = 16
TAU = 16.0
GROUPS = 4
GC = D // GROUPS
GM_CHUNK = 128
F = 2816
N_MOD = 9
EPS = 1e-6
LOG2E = 1.4426950408889634

CH = 128
N_LEVELS = 7
LANE = 128

C_Q, C_K, C_ALR, C_U, C_GV, C_V, C_R, C_GA, C_GB = 0, 512, 1024, 1152, 2176, 3200, 4224, 5248, 6272
D_P = C_GB + D

PROJ_QK = (0, 0, 2 * DK, C_Q)
PROJ_ALR = (1, 0, LANE, C_ALR)
PROJ_UGV = (2, 0, 2 * D, C_U)
PROJ_V = (0, 2 * DK, 2 * DK + DV, C_V)
PROJ_R = (0, 2 * DK + DV, 2 * DK + 2 * DV, C_R)
PROJ_GATES = (2, 2 * D, 4 * D, C_GA)
PROJ_ALL = (PROJ_QK, PROJ_ALR, PROJ_UGV, PROJ_V, PROJ_R, PROJ_GATES)

TM_P = 512
TM_FFN = 512
NBS = 4
VMEM_LIMIT = 58 * 1024 * 1024


def _gla_constants():
  t = np.arange(CH)
  masks = []
  for lvl in range(N_LEVELS):
    m = 1 << lvl
    upper = t >= (t // (2 * m)) * (2 * m) + m
    same_pair = (t[:, None] // (2 * m)) == (t[None, :] // (2 * m))
    masks.append((upper[:, None] & (~upper)[None, :] & same_pair).astype(np.float32))
  masks.append(np.eye(CH, dtype=np.float32))
  ltri = (t[None, :] <= t[:, None]).astype(np.float32)
  return ltri, np.stack(masks)


def _level_exponents(g, b):
  t = lax.broadcasted_iota(jnp.int32, g.shape, 0)
  g_prev = pltpu.roll(g, 1, 0)
  g_next = pltpu.roll(g, CH - 1, 0)
  r4 = t & 3
  levels = [
      jnp.where((t & 1) == 1, g, 0.0),
      jnp.where(r4 == 0, g_next, jnp.where(r4 == 1, 0.0, jnp.where(r4 == 2, g, g + g_prev))),
  ]
  groups = CH // 8
  row = lambda i: jnp.broadcast_to(b[i:i + 1, :], (8, b.shape[1]))
  b3 = [row(8 * j + 3) for j in range(groups)]
  b7 = [row(8 * j + 7) for j in range(groups)]
  levels.append(-jnp.abs(b - jnp.concatenate(b3, axis=0)))
  for lvl in range(3, N_LEVELS):
    m = 1 << lvl
    ref = [b7[((8 * j) // (2 * m)) * (2 * m) // 8 + m // 8 - 1] for j in range(groups)]
    levels.append(-jnp.abs(b - jnp.concatenate(ref, axis=0)))
  to_end = -jnp.abs(b - jnp.concatenate([b7[groups - 1]] * groups, axis=0))
  return levels, to_end


def _dot(a, b):
  return jnp.dot(a, b, preferred_element_type=F32)


def _rms(x, w):
  ms = jnp.mean(x * x, axis=-1, keepdims=True)
  return x * lax.rsqrt(ms + EPS) * w


def _norm_mod(x, nw, sc, sh):
  if sc.shape[0] != 1:
    return (_rms(x, nw) * (1.0 + sc) + sh).astype(BF16)
  ms = jnp.mean(x * x, axis=-1, keepdims=True)
  return (x * lax.rsqrt(ms + EPS) * (nw * (1.0 + sc)) + sh).astype(BF16)


def _gelu(x):
  return 0.5 * x * (1.0 + lax.erf(x * (2.0 ** -0.5)))


def _log_sigmoid(z):
  return jnp.minimum(z, 0.0) - jnp.log1p(jnp.exp(-jnp.abs(z)))


def _const_spec(shape):
  nd = len(shape)
  return pl.BlockSpec(shape, lambda *_: (0,) * nd, pipeline_mode=pl.Buffered(1))


ADA_STEPS = 4


def _ada_kernel(c_ref, w_ref, b_ref, o_ref):
  c = c_ref[...]
  s = (c * jax.nn.sigmoid(c)).astype(BF16)
  o_ref[...] = _dot(s, w_ref[...].astype(BF16)) + b_ref[...]


def _ada_call(c_all, w_ada, b_ada):
  rows = c_all.shape[0]
  tn = CAST_MOD_FROM * D // ADA_STEPS
  assert tn % LANE == 0
  return pl.pallas_call(
      _ada_kernel,
      grid=(ADA_STEPS,),
      in_specs=[
          pl.BlockSpec((rows, D), lambda i: (0, 0)),
          pl.BlockSpec((D, tn), lambda i: (0, i)),
          pl.BlockSpec((1, tn), lambda i: (0, i)),
      ],
      out_specs=pl.BlockSpec((rows, tn), lambda i: (0, i)),
      out_shape=jax.ShapeDtypeStruct((rows, CAST_MOD_FROM * D), F32),
      compiler_params=pltpu.CompilerParams(
          dimension_semantics=("arbitrary",), vmem_limit_bytes=VMEM_LIMIT),
      name="ada",
  )(c_all, w_ada, b_ada)


def _ffn_rows(x_ref, sh_ref, sc_ref, g_ref, nw_ref, w13_ref, w2_ref, nf_ref, o_ref, mod_row=None):
  pick = (lambda ref: ref[...]) if mod_row is None else (lambda ref: ref[pl.ds(mod_row, 1), :])
  x = x_ref[...]
  h = _norm_mod(x, nw_ref[...], pick(sc_ref), pick(sh_ref))
  a = _dot(h, w13_ref[:, :F])
  b = _dot(h, w13_ref[:, F:])
  p = (a * jax.nn.sigmoid(a) * b).astype(BF16)
  out = x + 0.5 * pick(g_ref) * _dot(p, w2_ref[...])
  if nf_ref is not None:
    out = _rms(out, nf_ref[...])
  o_ref[...] = out


W_LOAD_CHUNKS = 8


def _load_cast(w_hbm, w_ref, buf_ref, sem_ref):
  rows = w_hbm.shape[0] // W_LOAD_CHUNKS
  copy = lambda k: pltpu.make_async_copy(
      w_hbm.at[pl.ds(k * rows, rows), :], buf_ref.at[k % 2], sem_ref.at[k % 2])
  copy(0).start()
  for k in range(W_LOAD_CHUNKS):
    if k + 1 < W_LOAD_CHUNKS:
      copy(k + 1).start()
    copy(k).wait()
    w_ref[k * rows:(k + 1) * rows, :] = buf_ref[k % 2].astype(BF16)


def _ffn_kernel(x_ref, sh_ref, sc_ref, g_ref, xs_ref, shs_ref, scs_ref, gs_ref, nw_ref, w13_hbm,
                w2_hbm, wta_ref, wtalr_ref, wtb_ref, w_pa_ref, w_pb_ref, w_o_ref, n13_ref, n2_ref,
                c_ref, wada_ref, bada_ref,
                o_ref, os_ref, wa_ref, walr_ref, wb_ref, pa_ref, pb_ref, po_ref, o13_ref, o2_ref,
                modb_ref, w13_ref, w2_ref, buf13_ref, buf2_ref, sem13_ref, sem2_ref, *, tiles):
  i = pl.program_id(0)
  n = pl.num_programs(0) - 1

  @pl.when(i == 0)
  def _():
    _load_cast(w13_hbm, w13_ref, buf13_ref, sem13_ref)
    _load_cast(w2_hbm, w2_ref, buf2_ref, sem2_ref)

  @pl.when(i < n)
  def _():
    _ffn_rows(x_ref, sh_ref, sc_ref, g_ref, nw_ref, w13_ref, w2_ref, None, o_ref,
              mod_row=i // tiles)
    c = c_ref[...]
    modb_ref[...] = _dot((c * jax.nn.sigmoid(c)).astype(BF16),
                         wada_ref[...].astype(BF16)) + bada_ref[...]
    wa_ref[...] = wta_ref[...].T.astype(BF16)
    wb_ref[...] = wtb_ref[...].T.astype(BF16)
    alr = jnp.concatenate([wtalr_ref[...], jnp.zeros((LANE - LOWRANK, D), F32)], axis=0)
    walr_ref[...] = alr.T.astype(BF16)
    pa_ref[...] = w_pa_ref[...].astype(BF16)
    pb_ref[...] = w_pb_ref[...].astype(BF16)
    po_ref[...] = w_o_ref[...].astype(BF16)
    o13_ref[...] = n13_ref[...].astype(BF16)
    o2_ref[...] = n2_ref[...].astype(BF16)

  @pl.when(i == n)
  def _():
    _ffn_rows(xs_ref, shs_ref, scs_ref, gs_ref, nw_ref, w13_ref, w2_ref, None, os_ref)


CAST_STEPS = 32
CAST_MOD_FROM = 3
CAST_MOD_COLS = 256
W2_CAST_ROWS = 2 * F // CAST_STEPS


def _mod_specs(rows_s, batch, segs):
  assert rows_s % batch == 0
  one = pl.Buffered(1)
  prompt = [pl.BlockSpec((batch, D), lambda *_, s=s: (rows_s // batch, s), pipeline_mode=one)
            for s in segs]
  sample = [pl.BlockSpec((rows_s, D), lambda *_, s=s: (0, s), pipeline_mode=one) for s in segs]
  return prompt, sample


def _ffn_call(x, xs, mod, segs, nw, w13, w2, cast_weights, *, rows_per_batch, tm):
  rows, rows_s = x.shape[0], xs.shape[0]
  n = rows // tm
  tiles = rows_per_batch // tm
  last = lambda i: jnp.minimum(i, n - 1)
  x_spec = pl.BlockSpec((tm, D), lambda i: (last(i), 0))
  xs_spec = pl.BlockSpec((rows_s, D), lambda i: (0, 0))
  mod_prompt, mod_sample = _mod_specs(rows_s, rows // rows_per_batch, segs)
  assert n == CAST_STEPS and D % CAST_STEPS == 0
  assert D % (16 * W_LOAD_CHUNKS) == 0 and F % (16 * W_LOAD_CHUNKS) == 0
  w_t, w_pa, w_pb, w_o, n13, n2, c_all, w_ada, b_ada = cast_weights
  mod_cols = (N_MOD - CAST_MOD_FROM) * D
  assert mod_cols % CAST_MOD_COLS == 0 and mod_cols // CAST_MOD_COLS <= CAST_STEPS
  mod_blk = lambda i: jnp.minimum(i, mod_cols // CAST_MOD_COLS - 1)
  mod_off = CAST_MOD_FROM * D // CAST_MOD_COLS
  lo = 2 * DK + 2 * DV
  n_a, n_b = lo // LANE, (w_t.shape[0] - lo - LOWRANK) // LANE
  assert n_a <= CAST_STEPS and n_b == CAST_STEPS
  rb = D // CAST_STEPS
  step = lambda i: jnp.minimum(i, CAST_STEPS - 1)
  a_blk = lambda i: jnp.minimum(i, n_a - 1)
  elem = lambda r: (pl.Element(r), pl.Element(D))
  row_blk = lambda cols: pl.BlockSpec((rb, cols), lambda i: (step(i), 0))
  w2_blk = pl.BlockSpec((W2_CAST_ROWS, D), lambda i: (step(i) // 2, 0))
  in_specs = [x_spec] + mod_prompt + [_const_spec((rows_s, D))] + mod_sample + [
      _const_spec((1, D)), pl.BlockSpec(memory_space=pl.ANY), pl.BlockSpec(memory_space=pl.ANY),
      pl.BlockSpec(elem(LANE), lambda i: (pl.multiple_of(LANE * a_blk(i), LANE), 0)),
      pl.BlockSpec(elem(LOWRANK), lambda i: (lo, 0)),
      pl.BlockSpec(elem(LANE), lambda i: (pl.multiple_of(lo + LOWRANK + LANE * step(i), LOWRANK), 0)),
      row_blk(D), row_blk(D), row_blk(D), row_blk(2 * F), w2_blk,
      _const_spec(c_all.shape),
      pl.BlockSpec((D, CAST_MOD_COLS), lambda i: (0, mod_off + mod_blk(i))),
      pl.BlockSpec((1, CAST_MOD_COLS), lambda i: (0, mod_off + mod_blk(i)))]
  out_specs = [
      x_spec, xs_spec,
      pl.BlockSpec((D, LANE), lambda i: (0, a_blk(i))),
      pl.BlockSpec((D, LANE), lambda i: (0, 0)),
      pl.BlockSpec((D, LANE), lambda i: (0, step(i))),
      row_blk(D), row_blk(D), row_blk(D), row_blk(2 * F), w2_blk,
      pl.BlockSpec((c_all.shape[0], CAST_MOD_COLS), lambda i: (0, mod_blk(i)))]
  bf = lambda r, c: jax.ShapeDtypeStruct((r, c), BF16)
  out_shape = [jax.ShapeDtypeStruct((rows, D), F32), jax.ShapeDtypeStruct((rows_s, D), F32),
               bf(D, lo), bf(D, LANE), bf(D, n_b * LANE),
               bf(D, D), bf(D, D), bf(D, D), bf(D, 2 * F), bf(F, D),
               jax.ShapeDtypeStruct((c_all.shape[0], mod_cols), F32)]
  return pl.pallas_call(
      functools.partial(_ffn_kernel, tiles=tiles),
      grid=(n + 1,),
      in_specs=in_specs,
      out_specs=out_specs,
      out_shape=out_shape,
      scratch_shapes=[
          pltpu.VMEM((D, 2 * F), BF16),
          pltpu.VMEM((F, D), BF16),
          pltpu.VMEM((2, D // W_LOAD_CHUNKS, 2 * F), F32),
          pltpu.VMEM((2, F // W_LOAD_CHUNKS, D), F32),
          pltpu.SemaphoreType.DMA((2,)),
          pltpu.SemaphoreType.DMA((2,)),
      ],
      compiler_params=pltpu.CompilerParams(
          dimension_semantics=("arbitrary",), vmem_limit_bytes=VMEM_LIMIT),
      name="ffn1",
  )(x, mod, mod, mod, xs, mod, mod, mod, nw, w13, w2,
    w_t, w_t, w_t, w_pa, w_pb, w_o, n13, n2, c_all, w_ada, b_ada)


def _project_piece(h, w_refs, p_ref, piece):
  idx, lo, hi, dst = piece
  p_ref[:, dst:dst + hi - lo] = _dot(h, w_refs[idx][:, lo:hi])


def _log_decay(p_ref, w_a2_ref, b_a_ref):
  a_lr = p_ref[:, C_ALR:C_ALR + LANE].astype(BF16)
  z = _dot(a_lr, w_a2_ref[...]) + b_a_ref[...]
  return _log_sigmoid(z) * (1.0 / TAU)


def _head_norm_gate(o, p_ref, gnw_ref):
  parts = []
  for hh in range(HEADS):
    sl = slice(hh * HV, (hh + 1) * HV)
    parts.append(_rms(o[:, sl], gnw_ref[:, sl]))
  r = p_ref[:, C_R:C_R + DV]
  return jnp.concatenate(parts, axis=-1) * (r * jax.nn.sigmoid(r))


def _gv_norm(p_ref, lnw_ref, lnb_ref):
  gv = _gelu(p_ref[:, C_GV:C_GV + D])
  mu = jnp.mean(gv, axis=-1, keepdims=True)
  d = gv - mu
  var = jnp.mean(d * d, axis=-1, keepdims=True)
  return d * lax.rsqrt(var + EPS) * lnw_ref[...] + lnb_ref[...]


def _merge_out(x, g2, ya, yb, p_ref, w_o_ref):
  ga = p_ref[:, C_GA:C_GA + D]
  gb = p_ref[:, C_GB:C_GB + D]
  merged = (jax.nn.sigmoid(ga) * ya + jax.nn.sigmoid(gb) * yb).astype(BF16)
  return x + g2 * _dot(merged, w_o_ref[...])


def _mix_prompt_kernel(x_ref, sh_ref, sc_ref, g2_ref, nw_ref,
                       wa_ref, walr_ref, wb_ref, w_a2_ref, b_a_ref,
                       ltri_ref, mask_ref, gnw_ref, w_pa_ref, lnw_ref, lnb_ref, ws_ref, bs_ref,
                       w_pb_ref, w_o_ref, o_ref, s_out_ref,
                       st_ref, p_ref, oa_ref, mx_ref, kt_ref, s_ref):
  j = pl.program_id(1)
  tm = TM_P
  w_refs = (wa_ref, walr_ref, wb_ref)
  batch_row = pl.ds(pl.program_id(0), 1)
  sh, sc, g2 = sh_ref[batch_row, :], sc_ref[batch_row, :], g2_ref[batch_row, :]

  @pl.when(j == 0)
  def _():
    st_ref[...] = jnp.zeros_like(st_ref)

  ltri = ltri_ref[...]
  ti = lax.broadcasted_iota(jnp.int32, (GM_CHUNK, GM_CHUNK), 0)
  si = lax.broadcasted_iota(jnp.int32, (GM_CHUNK, GM_CHUNK), 1)

  def mix_stages():
    oa = oa_ref
    mx = mx_ref
    env = {}

    def gate():
      env["g"] = _log_decay(p_ref, w_a2_ref, b_a_ref) * LOG2E

    def cumsum():
      cums = []
      for c in range(tm // CH):
        gc = env["g"][c * CH:(c + 1) * CH, :]
        g_hi = gc.astype(BF16)
        g_lo = (gc - g_hi.astype(F32)).astype(BF16)
        cums.append(_dot(ltri, g_hi) + _dot(ltri, g_lo))
      env["cums"] = cums

    def prep():
      decays = []
      qk = {}
      for c in range(tm // CH):
        rows = slice(c * CH, (c + 1) * CH)
        b = env["cums"][c]
        e_lvl, e_end = _level_exponents(env["g"][rows, :], b)
        q = (p_ref[rows, C_Q:C_Q + DK] * (HK ** -0.5)).astype(BF16)
        k = p_ref[rows, C_K:C_K + DK].astype(BF16)
        for lvl in range(N_LEVELS):
          e = jnp.exp2(e_lvl[lvl]).astype(BF16)
          qk[c, 2 * lvl] = q * e
          qk[c, 2 * lvl + 1] = k * e
        qk[c, 2 * N_LEVELS] = q
        qk[c, 2 * N_LEVELS + 1] = k
        qk[c, 2 * N_LEVELS + 2] = q * jnp.exp2(b).astype(BF16)
        qk[c, 2 * N_LEVELS + 3] = k * jnp.exp2(e_end).astype(BF16)
        decays.append(jnp.exp2(b[CH - 1:CH, :]))
      env["decays"] = decays
      env["qk"] = qk
      env["gvn"] = _gv_norm(p_ref, lnw_ref, lnb_ref).astype(BF16)
      env["u"] = _gelu(p_ref[:, C_U:C_U + D])
      env["pending"] = None

    def level_scores(c, hh):
      ck = slice(hh * HK, (hh + 1) * HK)
      scores = None
      for lvl in range(N_LEVELS + 1):
        kt_ref[lvl] = env["qk"][c, 2 * lvl + 1][:, ck].T
        part = mask_ref[lvl] * _dot(env["qk"][c, 2 * lvl][:, ck], kt_ref[lvl])
        scores = part if scores is None else scores + part
      return scores.astype(BF16)

    def finish(c, hh, scores):
      rows = slice(c * CH, (c + 1) * CH)
      ck = slice(hh * HK, (hh + 1) * HK)
      v32 = p_ref[rows, C_V + hh * HV:C_V + (hh + 1) * HV]
      st = st_ref[hh]
      s_ref[hh] = st.T.astype(BF16)
      oa[rows, hh * HV:(hh + 1) * HV] = (
          _dot(scores, v32.astype(BF16))
          + _dot(env["qk"][c, 2 * N_LEVELS + 2][:, ck], s_ref[hh]))
      st_ref[hh] = (st * env["decays"][c][:, ck]
                    + _dot(v32.T.astype(BF16), env["qk"][c, 2 * N_LEVELS + 3][:, ck]))

    def gla(c):
      def run():
        for hh in range(HEADS):
          scores = level_scores(c, hh)
          if env["pending"] is not None:
            finish(*env["pending"])
          env["pending"] = (c, hh, scores)
      return run

    def gmlp():
      for gg in range(GROUPS):
        ws = jnp.where(si <= ti, ws_ref[gg], 0.0).astype(BF16)
        cols = slice(gg * GC, (gg + 1) * GC)
        for c in range(tm // GM_CHUNK):
          rows = slice(c * GM_CHUNK, (c + 1) * GM_CHUNK)
          mx[rows, cols] = _dot(ws, env["gvn"][rows, cols]) + bs_ref[:, cols]
      finish(*env["pending"])

    def branches():
      env["yb"] = _dot((env["u"] * mx[...]).astype(BF16), w_pb_ref[...])
      env["ya"] = _dot(_head_norm_gate(oa[...], p_ref, gnw_ref).astype(BF16), w_pa_ref[...])

    def out():
      ga = p_ref[:, C_GA:C_GA + D]
      gb = p_ref[:, C_GB:C_GB + D]
      merged = (jax.nn.sigmoid(ga) * env["ya"] + jax.nn.sigmoid(gb) * env["yb"]).astype(BF16)
      for r0 in range(0, tm, tm // 2):
        sl = slice(r0, r0 + tm // 2)
        o_ref[sl, :] = x_ref[sl, :] + g2 * _dot(merged[sl, :], w_o_ref[...])

    return gate, cumsum, prep, [gla(c) for c in range(tm // CH)], gmlp, branches, out

  gate, cumsum, prep, glas, gmlp, branches, out = mix_stages()
  halves = []
  for r0 in range(0, tm, tm // 2):
    sl = slice(r0, r0 + tm // 2)
    halves.append(_norm_mod(x_ref[sl, :], nw_ref[...], sc, sh))
    idx, lo, hi, dst = PROJ_QK
    p_ref[sl, dst:dst + hi - lo] = _dot(halves[-1], w_refs[idx][:, lo:hi])
  h = jnp.concatenate(halves, axis=0)
  proj = lambda piece: _project_piece(h, w_refs, p_ref, piece)
  proj(PROJ_ALR)
  proj(PROJ_UGV)
  gate()
  proj(PROJ_V)
  cumsum()
  proj(PROJ_R)
  proj(PROJ_GATES)
  prep()
  for gla_chunk in glas:
    gla_chunk()
  gmlp()
  branches()
  out()

  @pl.when(j == pl.num_programs(1) - 1)
  def _():
    for hh in range(HEADS):
      s_out_ref[hh] = st_ref[hh].T


def _mix_prompt_call(x, mod, segs, nw, w_in3, w_a2, b_a, ltri, masks, gnw, w_pa, lnw, lnb, ws,
                     bs_full, w_pb, w_o, *, batch, seq):
  tm = TM_P
  tiles = seq // tm
  row_spec = pl.BlockSpec((tm, D), lambda b, j: (b * tiles + j, 0))

  mod_prompt, _ = _mod_specs(mod.shape[0] - batch, batch, segs)
  wa, walr, wb = w_in3
  in_specs = [
      row_spec, *mod_prompt,
      _const_spec((1, D)), _const_spec(wa.shape), _const_spec(walr.shape), _const_spec(wb.shape),
      _const_spec((LANE, DK)), _const_spec((1, DK)),
      _const_spec(ltri.shape), _const_spec(masks.shape), _const_spec((1, DV)),
      _const_spec((DV, D)), _const_spec((1, D)), _const_spec((1, D)),
      _const_spec((GROUPS, GM_CHUNK, GM_CHUNK)), _const_spec((GM_CHUNK, D)),
      _const_spec((D, D)), _const_spec((D, D)),
  ]
  out_specs = [
      row_spec,
      pl.BlockSpec((None, None, HEADS, HK, HV), lambda b, j: (0, b, 0, 0, 0)),
  ]
  return pl.pallas_call(
      _mix_prompt_kernel,
      grid=(batch, tiles),
      in_specs=in_specs,
      out_specs=out_specs,
      out_shape=[jax.ShapeDtypeStruct((batch * seq, D), F32),
                 jax.ShapeDtypeStruct((1, batch, HEADS, HK, HV), F32)],
      scratch_shapes=[
          pltpu.VMEM((HEADS, HV, HK), F32),
          pltpu.VMEM((tm, D_P), F32),
          pltpu.VMEM((tm, DV), F32),
          pltpu.VMEM((tm, D), F32),
          pltpu.VMEM((N_LEVELS + 1, HK, CH), BF16),
          pltpu.VMEM((HEADS, HK, HV), BF16),
      ],
      compiler_params=pltpu.CompilerParams(
          dimension_semantics=("arbitrary", "arbitrary"), vmem_limit_bytes=VMEM_LIMIT),
      name="mix_prompt",
  )(x, mod, mod, mod, nw, wa, walr, wb, w_a2, b_a, ltri, masks, gnw, w_pa,
    lnw, lnb, ws, bs_full, w_pb, w_o)


def _sample_proj_kernel(x_ref, sh_ref, sc_ref, nw_ref, wa_ref, walr_ref, wb_ref, w_a2_ref, b_a_ref,
                        p_ref, col_ref, v_ref, oa_ref):
  h = _norm_mod(x_ref[...], nw_ref[...], sc_ref[...], sh_ref[...])
  for piece in PROJ_ALL:
    _project_piece(h, (wa_ref, walr_ref, wb_ref), p_ref, piece)
  g = _log_decay(p_ref, w_a2_ref, b_a_ref)
  q = p_ref[:, C_Q:C_Q + DK] * (HK ** -0.5)
  k = p_ref[:, C_K:C_K + DK]
  eb = jnp.exp(g)
  qe = q * eb
  v_ref[...] = p_ref[:, C_V:C_V + DV]
  for hh in range(HEADS):
    ck = slice(hh * HK, (hh + 1) * HK)
    col_ref[0, ck, :] = eb[:, ck].T
    col_ref[1, ck, :] = qe[:, ck].T
    col_ref[2, ck, :] = k[:, ck].T
    qk = jnp.sum(q[:, ck] * k[:, ck], axis=-1, keepdims=True)
    oa_ref[:, hh * HV:(hh + 1) * HV] = qk * p_ref[:, C_V + hh * HV:C_V + (hh + 1) * HV]


def _sample_state_rows(i, col_ref, v_ref, s_ref, s_out_ref, os_ref):
  rows = os_ref.shape[0]
  shift = (rows - i * NBS) % rows
  eb_t, qe_t, k_t = (pltpu.roll(col_ref[a], shift, 1) for a in range(3))
  for nl in range(NBS):
    n = i * NBS + nl
    for hh in range(HEADS):
      ck = slice(hh * HK, (hh + 1) * HK)
      cv = slice(hh * HV, (hh + 1) * HV)
      st = s_ref[nl, hh]
      o_inter = jnp.sum(qe_t[ck, nl:nl + 1] * st, axis=0, keepdims=True)
      os_ref[pl.ds(n, 1), cv] = os_ref[pl.ds(n, 1), cv] + o_inter
      s_out_ref[nl, hh] = eb_t[ck, nl:nl + 1] * st + k_t[ck, nl:nl + 1] * v_ref[pl.ds(n, 1), cv]


def _ffn_stream_kernel(x_ref, sh_ref, sc_ref, g_ref, nw_ref, w13_ref, w2_ref, nf_ref,
                       col_ref, v_ref, oa0_ref, s_ref,
                       xs_ref, p_ref, g2_ref, shs_ref, scs_ref, gs_ref, gnw_ref, w_pa_ref, lnw_ref,
                       lnb_ref, ws0_ref, bs0_ref, w_pb_ref, w_o_ref,
                       o_ref, os_ref, s_out_ref, ys_ref, gvn_ref, x2_ref, *, tiles):
  i = pl.program_id(0)
  n = pl.num_programs(0) - 1

  @pl.when(i == 0)
  def _():
    os_ref[...] = oa0_ref[...]

  @pl.when(i < n)
  def _():
    _sample_state_rows(i, col_ref, v_ref, s_ref, s_out_ref, os_ref)
    _ffn_rows(x_ref, sh_ref, sc_ref, g_ref, nw_ref, w13_ref, w2_ref, nf_ref, o_ref,
              mod_row=i // tiles)

  @pl.when(i == n)
  def _():
    ya = _dot(_head_norm_gate(os_ref[...], p_ref, gnw_ref).astype(BF16), w_pa_ref[...])
    gvn = _gv_norm(p_ref, lnw_ref, lnb_ref)
    gvn_ref[...] = gvn
    mixed = ws0_ref[...] * gvn + bs0_ref[...]
    u = _gelu(p_ref[:, C_U:C_U + D])
    yb = _dot((u * mixed).astype(BF16), w_pb_ref[...])
    x2_ref[...] = _merge_out(xs_ref[...], g2_ref[...], ya, yb, p_ref, w_o_ref)
    _ffn_rows(x2_ref, shs_ref, scs_ref, gs_ref, nw_ref, w13_ref, w2_ref, nf_ref, ys_ref)


def _sample_proj_call(x, mod, segs, nw, w_in3, w_a2, b_a):
  rows = x.shape[0]
  _, mod_sample = _mod_specs(rows, mod.shape[0] - rows, segs[:2])
  wa, walr, wb = w_in3
  shapes = [(rows, D_P), (3, DK, rows), (rows, DV), (rows, DV)]
  return pl.pallas_call(
      _sample_proj_kernel,
      grid=(1,),
      in_specs=[_const_spec((rows, D)), *mod_sample, _const_spec((1, D)), _const_spec(wa.shape),
                _const_spec(walr.shape), _const_spec(wb.shape), _const_spec((LANE, DK)),
                _const_spec((1, DK))],
      out_specs=[pl.BlockSpec(s, lambda i, nd=len(s): (0,) * nd) for s in shapes],
      out_shape=[jax.ShapeDtypeStruct(s, F32) for s in shapes],
      compiler_params=pltpu.CompilerParams(
          dimension_semantics=("arbitrary",), vmem_limit_bytes=VMEM_LIMIT),
      name="sample_proj",
  )(x, mod, mod, nw, wa, walr, wb, w_a2, b_a)


def _ffn_stream_call(x, mod, segs, nw, w13, w2, nf, cols, v_s, oa0, state,
                     xs, p_s, mix_segs, gnw, w_pa, lnw, lnb, ws0, bs0, w_pb, w_o, *, rows_per_batch, tm):
  rows, rows_s = x.shape[0], xs.shape[0]
  n = rows // tm
  assert n * NBS == rows_s
  last = lambda i: jnp.minimum(i, n - 1)
  x_spec = pl.BlockSpec((tm, D), lambda i: (last(i), 0))
  mod_prompt, mod_sample = _mod_specs(rows_s, rows // rows_per_batch, segs)
  _, mod_g2 = _mod_specs(rows_s, rows // rows_per_batch, mix_segs[2:])
  state_spec = pl.BlockSpec((None, NBS, HEADS, HK, HV), lambda i: (0, last(i), 0, 0, 0))
  row_out = pl.BlockSpec((rows_s, D), lambda i: (0, 0))
  outs = pl.pallas_call(
      functools.partial(_ffn_stream_kernel, tiles=rows_per_batch // tm),
      grid=(n + 1,),
      in_specs=[x_spec, *mod_prompt, _const_spec((1, D)), _const_spec((D, 2 * F)),
                _const_spec((F, D)), _const_spec((1, D)), _const_spec(cols.shape),
                _const_spec((rows_s, DV)), _const_spec((rows_s, DV)), state_spec,
                _const_spec((rows_s, D)), _const_spec((rows_s, D_P)), *mod_g2, *mod_sample,
                _const_spec((1, DV)), _const_spec((DV, D)), _const_spec((1, D)), _const_spec((1, D)),
                _const_spec((1, D)), _const_spec((1, D)), _const_spec((D, D)), _const_spec((D, D))],
      out_specs=[x_spec, pl.BlockSpec((rows_s, DV), lambda i: (0, 0)), state_spec, row_out, row_out],
      out_shape=[jax.ShapeDtypeStruct((rows, D), F32), jax.ShapeDtypeStruct((rows_s, DV), F32),
                 jax.ShapeDtypeStruct(state.shape, F32), jax.ShapeDtypeStruct((rows_s, D), F32),
                 jax.ShapeDtypeStruct((rows_s, D), F32)],
      scratch_shapes=[pltpu.VMEM((rows_s, D), F32)],
      compiler_params=pltpu.CompilerParams(
          dimension_semantics=("arbitrary",), vmem_limit_bytes=VMEM_LIMIT),
      name="ffn2",
  )(x, mod, mod, mod, nw, w13, w2, nf, cols, v_s, oa0, state,
    xs, p_s, mod, mod, mod, mod, gnw, w_pa, lnw, lnb, ws0, bs0, w_pb, w_o)
  y, _, s_new, ys, gvn = outs
  return y, s_new, ys, gvn


def kernel(x_prompt, x_sample, state_gla, c_prompt, c_sample, w_ada, b_ada, norm1_w, ffn1_w13, ffn1_w2, norm2_w, w_in, w_a2, b_a, gla_norm_w, w_pa, gm_ln_w, gm_ln_b, gm_ws, gm_bs, w_pb, w_o, norm3_w, ffn2_w13, ffn2_w2, normf_w):
  batch, seq, _ = x_prompt.shape
  dec = x_sample.shape[0]
  depth = w_ada.shape[0]
  assert depth == 1 and x_sample.shape[1] == 1
  assert seq % TM_P == 0 and seq % TM_FFN == 0

  ltri_np, masks_np = _gla_constants()
  ltri = jnp.asarray(ltri_np, BF16)
  masks = jnp.asarray(masks_np, F32)

  xp = x_prompt.reshape(batch * seq, D)
  xs = x_sample.reshape(dec, D)
  l = 0
  c_all = jnp.concatenate([c_sample, c_prompt], axis=0)
  b_ada_row = b_ada[l].reshape(1, N_MOD * D)
  mod_a = _ada_call(c_all, w_ada[l], b_ada_row)

  row = lambda a: a.reshape(1, -1)
  w_a2_p = jnp.pad(w_a2[l], ((0, LANE - LOWRANK), (0, 0))).astype(BF16)
  gnw = gla_norm_w[l].reshape(1, DV)
  bs_full = jnp.repeat(gm_bs[l].T, GC, axis=1)
  ws0 = jnp.repeat(gm_ws[l][:, 0, 0], GC).reshape(1, D)
  bs0 = bs_full[0:1]
  normf = row(normf_w)

  cast_weights = (jnp.transpose(w_in[l]), w_pa[l], w_pb[l], w_o[l], ffn2_w13[l], ffn2_w2[l],
                  c_all, w_ada[l], b_ada_row)
  (xp, xs, wa, walr, wb, w_pa_b, w_pb_b, w_o_b, w13_2, w2_2, mod_b) = _ffn_call(
      xp, xs, mod_a, (0, 1, 2), row(norm1_w[l]), ffn1_w13[l], ffn1_w2[l], cast_weights,
      rows_per_batch=seq, tm=TM_FFN)
  w_in3 = (wa, walr, wb)
  mix_segs = tuple(s - CAST_MOD_FROM for s in (3, 4, 5))
  ffn2_segs = tuple(s - CAST_MOD_FROM for s in (6, 7, 8))
  xp, s_prompt = _mix_prompt_call(xp, mod_b, mix_segs, row(norm2_w[l]), w_in3, w_a2_p, row(b_a[l]),
                                  ltri, masks, gnw, w_pa_b, row(gm_ln_w[l]), row(gm_ln_b[l]),
                                  gm_ws[l], bs_full, w_pb_b, w_o_b, batch=batch, seq=seq)
  p_s, cols, v_s, oa0 = _sample_proj_call(xs, mod_b, mix_segs, row(norm2_w[l]), w_in3, w_a2_p,
                                          row(b_a[l]))
  yp, s_sample, ys, gvn = _ffn_stream_call(
      xp, mod_b, ffn2_segs, row(norm3_w[l]), w13_2, w2_2, normf, cols, v_s, oa0, state_gla,
      xs, p_s, mix_segs, gnw, w_pa_b, row(gm_ln_w[l]), row(gm_ln_b[l]), ws0, bs0, w_pb_b, w_o_b,
      rows_per_batch=seq, tm=TM_FFN)

  return (yp.reshape(batch, seq, D), ys.reshape(dec, 1, D), s_prompt, s_sample,
          gvn.reshape(1, dec, 1, D))
```

```python
import functools

import numpy as np
import jax
import jax.numpy as jnp
from jax import lax
from jax.experimental import pallas as pl
from jax.experimental.pallas import tpu as pltpu

F32 = jnp.float32
BF16 = jnp.bfloat16

D = 1024
HEADS = 4
HK = 128
HV = 256
DK = HEADS * HK
DV = HEADS * HV
LOWRANK = 16
TAU = 16.0
GROUPS = 4
GC = D // GROUPS
GM_CHUNK = 128
F = 2816
N_MOD = 9
EPS = 1e-6
LOG2E = 1.4426950408889634

CH = 128
N_LEVELS = 7
LANE = 128

C_Q, C_K, C_ALR, C_U, C_GV, C_V, C_R, C_GA, C_GB = 0, 512, 1024, 1152, 2176, 3200, 4224, 5248, 6272
D_P = C_GB + D

PROJ_QK = (0, 0, 2 * DK, C_Q)
PROJ_ALR = (1, 0, LANE, C_ALR)
PROJ_UGV = (2, 0, 2 * D, C_U)
PROJ_V = (0, 2 * DK, 2 * DK + DV, C_V)
PROJ_R = (0, 2 * DK + DV, 2 * DK + 2 * DV, C_R)
PROJ_GATES = (2, 2 * D, 4 * D, C_GA)
PROJ_ALL = (PROJ_QK, PROJ_ALR, PROJ_UGV, PROJ_V, PROJ_R, PROJ_GATES)

TM_P = 512
TM_FFN = 512
NBS = 4
VMEM_LIMIT = 58 * 1024 * 1024


def _gla_constants():
  t = np.arange(CH)
  masks = []
  for lvl in range(N_LEVELS):
    m = 1 << lvl
    upper = t >= (t // (2 * m)) * (2 * m) + m
    same_pair = (t[:, None] // (2 * m)) == (t[None, :] // (2 * m))
    masks.append((upper[:, None] & (~upper)[None, :] & same_pair).astype(np.float32))
  masks.append(np.eye(CH, dtype=np.float32))
  ltri = (t[None, :] <= t[:, None]).astype(np.float32)
  return ltri, np.stack(masks)


def _level_exponents(g, b):
  t = lax.broadcasted_iota(jnp.int32, g.shape, 0)
  g_prev = pltpu.roll(g, 1, 0)
  g_next = pltpu.roll(g, CH - 1, 0)
  r4 = t & 3
  levels = [
      jnp.where((t & 1) == 1, g, 0.0),
      jnp.where(r4 == 0, g_next, jnp.where(r4 == 1, 0.0, jnp.where(r4 == 2, g, g + g_prev))),
  ]
  groups = CH // 8
  row = lambda i: jnp.broadcast_to(b[i:i + 1, :], (8, b.shape[1]))
  b3 = [row(8 * j + 3) for j in range(groups)]
  b7 = [row(8 * j + 7) for j in range(groups)]
  levels.append(-jnp.abs(b - jnp.concatenate(b3, axis=0)))
  for lvl in range(3, N_LEVELS):
    m = 1 << lvl
    ref = [b7[((8 * j) // (2 * m)) * (2 * m) // 8 + m // 8 - 1] for j in range(groups)]
    levels.append(-jnp.abs(b - jnp.concatenate(ref, axis=0)))
  to_end = -jnp.abs(b - jnp.concatenate([b7[groups - 1]] * groups, axis=0))
  return levels, to_end


def _dot(a, b):
  return jnp.dot(a, b, preferred_element_type=F32)


def _rms(x, w):
  ms = jnp.mean(x * x, axis=-1, keepdims=True)
  return x * lax.rsqrt(ms + EPS) * w


def _norm_mod(x, nw, sc, sh):
  if sc.shape[0] != 1:
    return (_rms(x, nw) * (1.0 + sc) + sh).astype(BF16)
  ms = jnp.mean(x * x, axis=-1, keepdims=True)
  return (x * lax.rsqrt(ms + EPS) * (nw * (1.0 + sc)) + sh).astype(BF16)


def _gelu(x):
  return 0.5 * x * (1.0 + lax.erf(x * (2.0 ** -0.5)))


def _log_sigmoid(z):
  return jnp.minimum(z, 0.0) - jnp.log1p(jnp.exp(-jnp.abs(z)))


def _const_spec(shape):
  nd = len(shape)
  return pl.BlockSpec(shape, lambda *_: (0,) * nd, pipeline_mode=pl.Buffered(1))


ADA_STEPS = 4


def _ada_kernel(c_ref, w_ref, b_ref, o_ref):
  c = c_ref[...]
  s = (c * jax.nn.sigmoid(c)).astype(BF16)
  o_ref[...] = _dot(s, w_ref[...].astype(BF16)) + b_ref[...]


def _ada_call(c_all, w_ada, b_ada):
  rows = c_all.shape[0]
  tn = CAST_MOD_FROM * D // ADA_STEPS
  assert tn % LANE == 0
  return pl.pallas_call(
      _ada_kernel,
      grid=(ADA_STEPS,),
      in_specs=[
          pl.BlockSpec((rows, D), lambda i: (0, 0)),
          pl.BlockSpec((D, tn), lambda i: (0, i)),
          pl.BlockSpec((1, tn), lambda i: (0, i)),
      ],
      out_specs=pl.BlockSpec((rows, tn), lambda i: (0, i)),
      out_shape=jax.ShapeDtypeStruct((rows, CAST_MOD_FROM * D), F32),
      compiler_params=pltpu.CompilerParams(
          dimension_semantics=("arbitrary",), vmem_limit_bytes=VMEM_LIMIT),
      name="ada",
  )(c_all, w_ada, b_ada)


def _ffn_rows(x_ref, sh_ref, sc_ref, g_ref, nw_ref, w13_ref, w2_ref, nf_ref, o_ref, mod_row=None):
  pick = (lambda ref: ref[...]) if mod_row is None else (lambda ref: ref[pl.ds(mod_row, 1), :])
  x = x_ref[...]
  h = _norm_mod(x, nw_ref[...], pick(sc_ref), pick(sh_ref))
  a = _dot(h, w13_ref[:, :F])
  b = _dot(h, w13_ref[:, F:])
  p = (a * jax.nn.sigmoid(a) * b).astype(BF16)
  out = x + 0.5 * pick(g_ref) * _dot(p, w2_ref[...])
  if nf_ref is not None:
    out = _rms(out, nf_ref[...])
  o_ref[...] = out


W_LOAD_CHUNKS = 16
W_LOAD_BUFS = 4


def _load_cast(w_hbm, w_ref, buf_ref, sem_ref):
  rows = w_hbm.shape[0] // W_LOAD_CHUNKS
  ahead = W_LOAD_BUFS - 1
  copy = lambda k: pltpu.make_async_copy(
      w_hbm.at[pl.ds(k * rows, rows), :], buf_ref.at[k % W_LOAD_BUFS], sem_ref.at[k % W_LOAD_BUFS])
  for k in range(ahead):
    copy(k).start()
  for k in range(W_LOAD_CHUNKS):
    if k + ahead < W_LOAD_CHUNKS:
      copy(k + ahead).start()
    copy(k).wait()
    w_ref[k * rows:(k + 1) * rows, :] = buf_ref[k % W_LOAD_BUFS].astype(BF16)


def _ffn_kernel(x_ref, sh_ref, sc_ref, g_ref, xs_ref, shs_ref, scs_ref, gs_ref, nw_ref, w13_hbm,
                w2_hbm, wta_ref, wtalr_ref, wtb_ref, w_pa_ref, w_pb_ref, w_o_ref, n13_ref, n2_ref,
                c_ref, wada_ref, bada_ref,
                o_ref, os_ref, wa_ref, walr_ref, wb_ref, pa_ref, pb_ref, po_ref, o13_ref, o2_ref,
                modb_ref, w13_ref, w2_ref, buf13_ref, buf2_ref, sem13_ref, sem2_ref, *, tiles):
  i = pl.program_id(0)
  n = pl.num_programs(0) - 1

  @pl.when(i == 0)
  def _():
    _load_cast(w13_hbm, w13_ref, buf13_ref, sem13_ref)
    _load_cast(w2_hbm, w2_ref, buf2_ref, sem2_ref)

  @pl.when(i < n)
  def _():
    _ffn_rows(x_ref, sh_ref, sc_ref, g_ref, nw_ref, w13_ref, w2_ref, None, o_ref,
              mod_row=i // tiles)
    c = c_ref[...]
    modb_ref[...] = _dot((c * jax.nn.sigmoid(c)).astype(BF16),
                         wada_ref[...].astype(BF16)) + bada_ref[...]
    wa_ref[...] = wta_ref[...].T.astype(BF16)
    wb_ref[...] = wtb_ref[...].T.astype(BF16)
    alr = jnp.concatenate([wtalr_ref[...], jnp.zeros((LANE - LOWRANK, D), F32)], axis=0)
    walr_ref[...] = alr.T.astype(BF16)
    pa_ref[...] = w_pa_ref[...].astype(BF16)
    pb_ref[...] = w_pb_ref[...].astype(BF16)
    po_ref[...] = w_o_ref[...].astype(BF16)
    o13_ref[...] = n13_ref[...].astype(BF16)
    o2_ref[...] = n2_ref[...].astype(BF16)

  @pl.when(i == n)
  def _():
    _ffn_rows(xs_ref, shs_ref, scs_ref, gs_ref, nw_ref, w13_ref, w2_ref, None, os_ref)


CAST_STEPS = 32
CAST_MOD_FROM = 3
CAST_MOD_COLS = 256
W2_CAST_ROWS = 2 * F // CAST_STEPS


def _mod_specs(rows_s, batch, segs):
  assert rows_s % batch == 0
  one = pl.Buffered(1)
  prompt = [pl.BlockSpec((batch, D), lambda *_, s=s: (rows_s // batch, s), pipeline_mode=one)
            for s in segs]
  sample = [pl.BlockSpec((rows_s, D), lambda *_, s=s: (0, s), pipeline_mode=one) for s in segs]
  return prompt, sample


def _ffn_call(x, xs, mod, segs, nw, w13, w2, cast_weights, *, rows_per_batch, tm):
  rows, rows_s = x.shape[0], xs.shape[0]
  n = rows // tm
  tiles = rows_per_batch // tm
  last = lambda i: jnp.minimum(i, n - 1)
  x_spec = pl.BlockSpec((tm, D), lambda i: (last(i), 0))
  xs_spec = pl.BlockSpec((rows_s, D), lambda i: (0, 0))
  mod_prompt, mod_sample = _mod_specs(rows_s, rows // rows_per_batch, segs)
  assert n == CAST_STEPS and D % CAST_STEPS == 0
  assert D % (16 * W_LOAD_CHUNKS) == 0 and F % (16 * W_LOAD_CHUNKS) == 0
  w_t, w_pa, w_pb, w_o, n13, n2, c_all, w_ada, b_ada = cast_weights
  mod_cols = (N_MOD - CAST_MOD_FROM) * D
  assert mod_cols % CAST_MOD_COLS == 0 and mod_cols // CAST_MOD_COLS <= CAST_STEPS
  mod_blk = lambda i: jnp.minimum(i, mod_cols // CAST_MOD_COLS - 1)
  mod_off = CAST_MOD_FROM * D // CAST_MOD_COLS
  lo = 2 * DK + 2 * DV
  n_a, n_b = lo // LANE, (w_t.shape[0] - lo - LOWRANK) // LANE
  assert n_a <= CAST_STEPS and n_b == CAST_STEPS
  rb = D // CAST_STEPS
  step = lambda i: jnp.minimum(i, CAST_STEPS - 1)
  a_blk = lambda i: jnp.minimum(i, n_a - 1)
  elem = lambda r: (pl.Element(r), pl.Element(D))
  row_blk = lambda cols: pl.BlockSpec((rb, cols), lambda i: (step(i), 0))
  w2_blk = pl.BlockSpec((W2_CAST_ROWS, D), lambda i: (step(i) // 2, 0))
  in_specs = [x_spec] + mod_prompt + [_const_spec((rows_s, D))] + mod_sample + [
      _const_spec((1, D)), pl.BlockSpec(memory_space=pl.ANY), pl.BlockSpec(memory_space=pl.ANY),
      pl.BlockSpec(elem(LANE), lambda i: (pl.multiple_of(LANE * a_blk(i), LANE), 0)),
      pl.BlockSpec(elem(LOWRANK), lambda i: (lo, 0)),
      pl.BlockSpec(elem(LANE), lambda i: (pl.multiple_of(lo + LOWRANK + LANE * step(i), LOWRANK), 0)),
      row_blk(D), row_blk(D), row_blk(D), row_blk(2 * F), w2_blk,
      _const_spec(c_all.shape),
      pl.BlockSpec((D, CAST_MOD_COLS), lambda i: (0, mod_off + mod_blk(i))),
      pl.BlockSpec((1, CAST_MOD_COLS), lambda i: (0, mod_off + mod_blk(i)))]
  out_specs = [
      x_spec, xs_spec,
      pl.BlockSpec((D, LANE), lambda i: (0, a_blk(i))),
      pl.BlockSpec((D, LANE), lambda i: (0, 0)),
      pl.BlockSpec((D, LANE), lambda i: (0, step(i))),
      row_blk(D), row_blk(D), row_blk(D), row_blk(2 * F), w2_blk,
      pl.BlockSpec((c_all.shape[0], CAST_MOD_COLS), lambda i: (0, mod_blk(i)))]
  bf = lambda r, c: jax.ShapeDtypeStruct((r, c), BF16)
  out_shape = [jax.ShapeDtypeStruct((rows, D), F32), jax.ShapeDtypeStruct((rows_s, D), F32),
               bf(D, lo), bf(D, LANE), bf(D, n_b * LANE),
               bf(D, D), bf(D, D), bf(D, D), bf(D, 2 * F), bf(F, D),
               jax.ShapeDtypeStruct((c_all.shape[0], mod_cols), F32)]
  return pl.pallas_call(
      functools.partial(_ffn_kernel, tiles=tiles),
      grid=(n + 1,),
      in_specs=in_specs,
      out_specs=out_specs,
      out_shape=out_shape,
      scratch_shapes=[
          pltpu.VMEM((D, 2 * F), BF16),
          pltpu.VMEM((F, D), BF16),
          pltpu.VMEM((W_LOAD_BUFS, D // W_LOAD_CHUNKS, 2 * F), F32),
          pltpu.VMEM((W_LOAD_BUFS, F // W_LOAD_CHUNKS, D), F32),
          pltpu.SemaphoreType.DMA((W_LOAD_BUFS,)),
          pltpu.SemaphoreType.DMA((W_LOAD_BUFS,)),
      ],
      compiler_params=pltpu.CompilerParams(
          dimension_semantics=("arbitrary",), vmem_limit_bytes=VMEM_LIMIT),
      name="ffn1",
  )(x, mod, mod, mod, xs, mod, mod, mod, nw, w13, w2,
    w_t, w_t, w_t, w_pa, w_pb, w_o, n13, n2, c_all, w_ada, b_ada)


def _project_piece(h, w_refs, p_ref, piece):
  idx, lo, hi, dst = piece
  p_ref[:, dst:dst + hi - lo] = _dot(h, w_refs[idx][:, lo:hi])


def _log_decay(p_ref, w_a2_ref, b_a_ref):
  a_lr = p_ref[:, C_ALR:C_ALR + LANE].astype(BF16)
  z = _dot(a_lr, w_a2_ref[...]) + b_a_ref[...]
  return _log_sigmoid(z) * (1.0 / TAU)


def _head_norm_gate(o, p_ref, gnw_ref):
  parts = []
  for hh in range(HEADS):
    sl = slice(hh * HV, (hh + 1) * HV)
    parts.append(_rms(o[:, sl], gnw_ref[:, sl]))
  r = p_ref[:, C_R:C_R + DV]
  return jnp.concatenate(parts, axis=-1) * (r * jax.nn.sigmoid(r))


def _gv_norm(p_ref, lnw_ref, lnb_ref):
  gv = _gelu(p_ref[:, C_GV:C_GV + D])
  mu = jnp.mean(gv, axis=-1, keepdims=True)
  d = gv - mu
  var = jnp.mean(d * d, axis=-1, keepdims=True)
  return d * lax.rsqrt(var + EPS) * lnw_ref[...] + lnb_ref[...]


def _merge_out(x, g2, ya, yb, p_ref, w_o_ref):
  ga = p_ref[:, C_GA:C_GA + D]
  gb = p_ref[:, C_GB:C_GB + D]
  merged = (jax.nn.sigmoid(ga) * ya + jax.nn.sigmoid(gb) * yb).astype(BF16)
  return x + g2 * _dot(merged, w_o_ref[...])


def _mix_prompt_kernel(x_ref, sh_ref, sc_ref, g2_ref, nw_ref,
                       wa_ref, walr_ref, wb_ref, w_a2_ref, b_a_ref,
                       ltri_ref, mask_ref, gnw_ref, w_pa_ref, lnw_ref, lnb_ref, ws_ref, bs_ref,
                       w_pb_ref, w_o_ref, o_ref, s_out_ref,
                       st_ref, p_ref, oa_ref, mx_ref, kt_ref, s_ref):
  j = pl.program_id(1)
  tm = TM_P
  w_refs = (wa_ref, walr_ref, wb_ref)
  batch_row = pl.ds(pl.program_id(0), 1)
  sh, sc, g2 = sh_ref[batch_row, :], sc_ref[batch_row, :], g2_ref[batch_row, :]

  @pl.when(j == 0)
  def _():
    st_ref[...] = jnp.zeros_like(st_ref)

  ltri = ltri_ref[...]
  ti = lax.broadcasted_iota(jnp.int32, (GM_CHUNK, GM_CHUNK), 0)
  si = lax.broadcasted_iota(jnp.int32, (GM_CHUNK, GM_CHUNK), 1)

  def mix_stages():
    oa = oa_ref
    mx = mx_ref
    env = {}

    def gate():
      env["g"] = _log_decay(p_ref, w_a2_ref, b_a_ref) * LOG2E

    def cumsum():
      cums = []
      for c in range(tm // CH):
        gc = env["g"][c * CH:(c + 1) * CH, :]
        g_hi = gc.astype(BF16)
        g_lo = (gc - g_hi.astype(F32)).astype(BF16)
        cums.append(_dot(ltri, g_hi) + _dot(ltri, g_lo))
      env["cums"] = cums

    def prep():
      decays = []
      qk = {}
      for c in range(tm // CH):
        rows = slice(c * CH, (c + 1) * CH)
        b = env["cums"][c]
        e_lvl, e_end = _level_exponents(env["g"][rows, :], b)
        q = (p_ref[rows, C_Q:C_Q + DK] * (HK ** -0.5)).astype(BF16)
        k = p_ref[rows, C_K:C_K + DK].astype(BF16)
        for lvl in range(N_LEVELS):
          e = jnp.exp2(e_lvl[lvl]).astype(BF16)
          qk[c, 2 * lvl] = q * e
          qk[c, 2 * lvl + 1] = k * e
        qk[c, 2 * N_LEVELS] = q
        qk[c, 2 * N_LEVELS + 1] = k
        qk[c, 2 * N_LEVELS + 2] = q * jnp.exp2(b).astype(BF16)
        qk[c, 2 * N_LEVELS + 3] = k * jnp.exp2(e_end).astype(BF16)
        decays.append(jnp.exp2(b[CH - 1:CH, :]))
      env["decays"] = decays
      env["qk"] = qk
      env["gvn"] = _gv_norm(p_ref, lnw_ref, lnb_ref).astype(BF16)
      env["u"] = _gelu(p_ref[:, C_U:C_U + D])
      env["pending"] = None

    def level_scores(c, hh):
      ck = slice(hh * HK, (hh + 1) * HK)
      scores = None
      for lvl in range(N_LEVELS + 1):
        kt_ref[lvl] = env["qk"][c, 2 * lvl + 1][:, ck].T
        part = mask_ref[lvl] * _dot(env["qk"][c, 2 * lvl][:, ck], kt_ref[lvl])
        scores = part if scores is None else scores + part
      return scores.astype(BF16)

    def finish(c, hh, scores):
      rows = slice(c * CH, (c + 1) * CH)
      ck = slice(hh * HK, (hh + 1) * HK)
      v32 = p_ref[rows, C_V + hh * HV:C_V + (hh + 1) * HV]
      st = st_ref[hh]
      s_ref[hh] = st.T.astype(BF16)
      oa[rows, hh * HV:(hh + 1) * HV] = (
          _dot(scores, v32.astype(BF16))
          + _dot(env["qk"][c, 2 * N_LEVELS + 2][:, ck], s_ref[hh]))
      st_ref[hh] = (st * env["decays"][c][:, ck]
                    + _dot(v32.T.astype(BF16), env["qk"][c, 2 * N_LEVELS + 3][:, ck]))

    def gla(c):
      def run():
        for hh in range(HEADS):
          scores = level_scores(c, hh)
          if env["pending"] is not None:
            finish(*env["pending"])
          env["pending"] = (c, hh, scores)
      return run

    def gmlp():
      for gg in range(GROUPS):
        ws = jnp.where(si <= ti, ws_ref[gg], 0.0).astype(BF16)
        cols = slice(gg * GC, (gg + 1) * GC)
        for c in range(tm // GM_CHUNK):
          rows = slice(c * GM_CHUNK, (c + 1) * GM_CHUNK)
          mx[rows, cols] = _dot(ws, env["gvn"][rows, cols]) + bs_ref[:, cols]
      finish(*env["pending"])

    def branches():
      env["yb"] = _dot((env["u"] * mx[...]).astype(BF16), w_pb_ref[...])
      env["ya"] = _dot(_head_norm_gate(oa[...], p_ref, gnw_ref).astype(BF16), w_pa_ref[...])

    def out():
      ga = p_ref[:, C_GA:C_GA + D]
      gb = p_ref[:, C_GB:C_GB + D]
      merged = (jax.nn.sigmoid(ga) * env["ya"] + jax.nn.sigmoid(gb) * env["yb"]).astype(BF16)
      for r0 in range(0, tm, tm // 2):
        sl = slice(r0, r0 + tm // 2)
        o_ref[sl, :] = x_ref[sl, :] + g2 * _dot(merged[sl, :], w_o_ref[...])

    return gate, cumsum, prep, [gla(c) for c in range(tm // CH)], gmlp, branches, out

  gate, cumsum, prep, glas, gmlp, branches, out = mix_stages()
  halves = []
  for r0 in range(0, tm, tm // 2):
    sl = slice(r0, r0 + tm // 2)
    halves.append(_norm_mod(x_ref[sl, :], nw_ref[...], sc, sh))
    idx, lo, hi, dst = PROJ_QK
    p_ref[sl, dst:dst + hi - lo] = _dot(halves[-1], w_refs[idx][:, lo:hi])
  h = jnp.concatenate(halves, axis=0)
  proj = lambda piece: _project_piece(h, w_refs, p_ref, piece)
  proj(PROJ_ALR)
  proj(PROJ_UGV)
  gate()
  proj(PROJ_V)
  cumsum()
  proj(PROJ_R)
  proj(PROJ_GATES)
  prep()
  for gla_chunk in glas:
    gla_chunk()
  gmlp()
  branches()
  out()

  @pl.when(j == pl.num_programs(1) - 1)
  def _():
    for hh in range(HEADS):
      s_out_ref[hh] = st_ref[hh].T


def _mix_prompt_call(x, mod, segs, nw, w_in3, w_a2, b_a, ltri, masks, gnw, w_pa, lnw, lnb, ws,
                     bs_full, w_pb, w_o, *, batch, seq):
  tm = TM_P
  tiles = seq // tm
  row_spec = pl.BlockSpec((tm, D), lambda b, j: (b * tiles + j, 0))

  mod_prompt, _ = _mod_specs(mod.shape[0] - batch, batch, segs)
  wa, walr, wb = w_in3
  in_specs = [
      row_spec, *mod_prompt,
      _const_spec((1, D)), _const_spec(wa.shape), _const_spec(walr.shape), _const_spec(wb.shape),
      _const_spec((LANE, DK)), _const_spec((1, DK)),
      _const_spec(ltri.shape), _const_spec(masks.shape), _const_spec((1, DV)),
      _const_spec((DV, D)), _const_spec((1, D)), _const_spec((1, D)),
      _const_spec((GROUPS, GM_CHUNK, GM_CHUNK)), _const_spec((GM_CHUNK, D)),
      _const_spec((D, D)), _const_spec((D, D)),
  ]
  out_specs = [
      row_spec,
      pl.BlockSpec((None, None, HEADS, HK, HV), lambda b, j: (0, b, 0, 0, 0)),
  ]
  return pl.pallas_call(
      _mix_prompt_kernel,
      grid=(batch, tiles),
      in_specs=in_specs,
      out_specs=out_specs,
      out_shape=[jax.ShapeDtypeStruct((batch * seq, D), F32),
                 jax.ShapeDtypeStruct((1, batch, HEADS, HK, HV), F32)],
      scratch_shapes=[
          pltpu.VMEM((HEADS, HV, HK), F32),
          pltpu.VMEM((tm, D_P), F32),
          pltpu.VMEM((tm, DV), F32),
          pltpu.VMEM((tm, D), F32),
          pltpu.VMEM((N_LEVELS + 1, HK, CH), BF16),
          pltpu.VMEM((HEADS, HK, HV), BF16),
      ],
      compiler_params=pltpu.CompilerParams(
          dimension_semantics=("arbitrary", "arbitrary"), vmem_limit_bytes=VMEM_LIMIT),
      name="mix_prompt",
  )(x, mod, mod, mod, nw, wa, walr, wb, w_a2, b_a, ltri, masks, gnw, w_pa,
    lnw, lnb, ws, bs_full, w_pb, w_o)


def _sample_proj_kernel(x_ref, sh_ref, sc_ref, nw_ref, wa_ref, walr_ref, wb_ref, w_a2_ref, b_a_ref,
                        p_ref, col_ref, v_ref, oa_ref):
  h = _norm_mod(x_ref[...], nw_ref[...], sc_ref[...], sh_ref[...])
  for piece in PROJ_ALL:
    _project_piece(h, (wa_ref, walr_ref, wb_ref), p_ref, piece)
  g = _log_decay(p_ref, w_a2_ref, b_a_ref)
  q = p_ref[:, C_Q:C_Q + DK] * (HK ** -0.5)
  k = p_ref[:, C_K:C_K + DK]
  eb = jnp.exp(g)
  qe = q * eb
  v_ref[...] = p_ref[:, C_V:C_V + DV]
  for hh in range(HEADS):
    ck = slice(hh * HK, (hh + 1) * HK)
    col_ref[0, ck, :] = eb[:, ck].T
    col_ref[1, ck, :] = qe[:, ck].T
    col_ref[2, ck, :] = k[:, ck].T
    qk = jnp.sum(q[:, ck] * k[:, ck], axis=-1, keepdims=True)
    oa_ref[:, hh * HV:(hh + 1) * HV] = qk * p_ref[:, C_V + hh * HV:C_V + (hh + 1) * HV]


def _sample_state_rows(i, col_ref, v_ref, s_ref, s_out_ref, os_ref):
  rows = os_ref.shape[0]
  shift = (rows - i * NBS) % rows
  eb_t, qe_t, k_t = (pltpu.roll(col_ref[a], shift, 1) for a in range(3))
  for nl in range(NBS):
    n = i * NBS + nl
    for hh in range(HEADS):
      ck = slice(hh * HK, (hh + 1) * HK)
      cv = slice(hh * HV, (hh + 1) * HV)
      st = s_ref[nl, hh]
      o_inter = jnp.sum(qe_t[ck, nl:nl + 1] * st, axis=0, keepdims=True)
      os_ref[pl.ds(n, 1), cv] = os_ref[pl.ds(n, 1), cv] + o_inter
      s_out_ref[nl, hh] = eb_t[ck, nl:nl + 1] * st + k_t[ck, nl:nl + 1] * v_ref[pl.ds(n, 1), cv]


def _ffn_stream_kernel(x_ref, sh_ref, sc_ref, g_ref, nw_ref, w13_ref, w2_ref, nf_ref,
                       col_ref, v_ref, oa0_ref, s_ref,
                       xs_ref, p_ref, g2_ref, shs_ref, scs_ref, gs_ref, gnw_ref, w_pa_ref, lnw_ref,
                       lnb_ref, ws0_ref, bs0_ref, w_pb_ref, w_o_ref,
                       o_ref, os_ref, s_out_ref, ys_ref, gvn_ref, x2_ref, *, tiles):
  i = pl.program_id(0)
  n = pl.num_programs(0) - 1

  @pl.when(i == 0)
  def _():
    os_ref[...] = oa0_ref[...]

  @pl.when(i < n)
  def _():
    _sample_state_rows(i, col_ref, v_ref, s_ref, s_out_ref, os_ref)
    _ffn_rows(x_ref, sh_ref, sc_ref, g_ref, nw_ref, w13_ref, w2_ref, nf_ref, o_ref,
              mod_row=i // tiles)

  @pl.when(i == n)
  def _():
    ya = _dot(_head_norm_gate(os_ref[...], p_ref, gnw_ref).astype(BF16), w_pa_ref[...])
    gvn = _gv_norm(p_ref, lnw_ref, lnb_ref)
    gvn_ref[...] = gvn
    mixed = ws0_ref[...] * gvn + bs0_ref[...]
    u = _gelu(p_ref[:, C_U:C_U + D])
    yb = _dot((u * mixed).astype(BF16), w_pb_ref[...])
    x2_ref[...] = _merge_out(xs_ref[...], g2_ref[...], ya, yb, p_ref, w_o_ref)
    _ffn_rows(x2_ref, shs_ref, scs_ref, gs_ref, nw_ref, w13_ref, w2_ref, nf_ref, ys_ref)


def _sample_proj_call(x, mod, segs, nw, w_in3, w_a2, b_a):
  rows = x.shape[0]
  _, mod_sample = _mod_specs(rows, mod.shape[0] - rows, segs[:2])
  wa, walr, wb = w_in3
  shapes = [(rows, D_P), (3, DK, rows), (rows, DV), (rows, DV)]
  return pl.pallas_call(
      _sample_proj_kernel,
      grid=(1,),
      in_specs=[_const_spec((rows, D)), *mod_sample, _const_spec((1, D)), _const_spec(wa.shape),
                _const_spec(walr.shape), _const_spec(wb.shape), _const_spec((LANE, DK)),
                _const_spec((1, DK))],
      out_specs=[pl.BlockSpec(s, lambda i, nd=len(s): (0,) * nd) for s in shapes],
      out_shape=[jax.ShapeDtypeStruct(s, F32) for s in shapes],
      compiler_params=pltpu.CompilerParams(
          dimension_semantics=("arbitrary",), vmem_limit_bytes=VMEM_LIMIT),
      name="sample_proj",
  )(x, mod, mod, nw, wa, walr, wb, w_a2, b_a)


def _ffn_stream_call(x, mod, segs, nw, w13, w2, nf, cols, v_s, oa0, state,
                     xs, p_s, mix_segs, gnw, w_pa, lnw, lnb, ws0, bs0, w_pb, w_o, *, rows_per_batch, tm):
  rows, rows_s = x.shape[0], xs.shape[0]
  n = rows // tm
  assert n * NBS == rows_s
  last = lambda i: jnp.minimum(i, n - 1)
  x_spec = pl.BlockSpec((tm, D), lambda i: (last(i), 0))
  mod_prompt, mod_sample = _mod_specs(rows_s, rows // rows_per_batch, segs)
  _, mod_g2 = _mod_specs(rows_s, rows // rows_per_batch, mix_segs[2:])
  state_spec = pl.BlockSpec((None, NBS, HEADS, HK, HV), lambda i: (0, last(i), 0, 0, 0))
  row_out = pl.BlockSpec((rows_s, D), lambda i: (0, 0))
  outs = pl.pallas_call(
      functools.partial(_ffn_stream_kernel, tiles=rows_per_batch // tm),
      grid=(n + 1,),
      in_specs=[x_spec, *mod_prompt, _const_spec((1, D)), _const_spec((D, 2 * F)),
                _const_spec((F, D)), _const_spec((1, D)), _const_spec(cols.shape),
                _const_spec((rows_s, DV)), _const_spec((rows_s, DV)), state_spec,
                _const_spec((rows_s, D)), _const_spec((rows_s, D_P)), *mod_g2, *mod_sample,
                _const_spec((1, DV)), _const_spec((DV, D)), _const_spec((1, D)), _const_spec((1, D)),
                _const_spec((1, D)), _const_spec((1, D)), _const_spec((D, D)), _const_spec((D, D))],
      out_specs=[x_spec, pl.BlockSpec((rows_s, DV), lambda i: (0, 0)), state_spec, row_out, row_out],
      out_shape=[jax.ShapeDtypeStruct((rows, D), F32), jax.ShapeDtypeStruct((rows_s, DV), F32),
                 jax.ShapeDtypeStruct(state.shape, F32), jax.ShapeDtypeStruct((rows_s, D), F32),
                 jax.ShapeDtypeStruct((rows_s, D), F32)],
      scratch_shapes=[pltpu.VMEM((rows_s, D), F32)],
      compiler_params=pltpu.CompilerParams(
          dimension_semantics=("arbitrary",), vmem_limit_bytes=VMEM_LIMIT),
      name="ffn2",
  )(x, mod, mod, mod, nw, w13, w2, nf, cols, v_s, oa0, state,
    xs, p_s, mod, mod, mod, mod, gnw, w_pa, lnw, lnb, ws0, bs0, w_pb, w_o)
  y, _, s_new, ys, gvn = outs
  return y, s_new, ys, gvn


def kernel(x_prompt, x_sample, state_gla, c_prompt, c_sample, w_ada, b_ada, norm1_w, ffn1_w13, ffn1_w2, norm2_w, w_in, w_a2, b_a, gla_norm_w, w_pa, gm_ln_w, gm_ln_b, gm_ws, gm_bs, w_pb, w_o, norm3_w, ffn2_w13, ffn2_w2, normf_w):
  batch, seq, _ = x_prompt.shape
  dec = x_sample.shape[0]
  depth = w_ada.shape[0]
  assert depth == 1 and x_sample.shape[1] == 1
  assert seq % TM_P == 0 and seq % TM_FFN == 0

  ltri_np, masks_np = _gla_constants()
  ltri = jnp.asarray(ltri_np, BF16)
  masks = jnp.asarray(masks_np, F32)

  xp = x_prompt.reshape(batch * seq, D)
  xs = x_sample.reshape(dec, D)
  l = 0
  c_all = jnp.concatenate([c_sample, c_prompt], axis=0)
  b_ada_row = b_ada[l].reshape(1, N_MOD * D)
  mod_a = _ada_call(c_all, w_ada[l], b_ada_row)

  row = lambda a: a.reshape(1, -1)
  w_a2_p = jnp.pad(w_a2[l], ((0, LANE - LOWRANK), (0, 0))).astype(BF16)
  gnw = gla_norm_w[l].reshape(1, DV)
  bs_full = jnp.repeat(gm_bs[l].T, GC, axis=1)
  ws0 = jnp.repeat(gm_ws[l][:, 0, 0], GC).reshape(1, D)
  bs0 = bs_full[0:1]
  normf = row(normf_w)

  cast_weights = (jnp.transpose(w_in[l]), w_pa[l], w_pb[l], w_o[l], ffn2_w13[l], ffn2_w2[l],
                  c_all, w_ada[l], b_ada_row)
  (xp, xs, wa, walr, wb, w_pa_b, w_pb_b, w_o_b, w13_2, w2_2, mod_b) = _ffn_call(
      xp, xs, mod_a, (0, 1, 2), row(norm1_w[l]), ffn1_w13[l], ffn1_w2[l], cast_weights,
      rows_per_batch=seq, tm=TM_FFN)
  w_in3 = (wa, walr, wb)
  mix_segs = tuple(s - CAST_MOD_FROM for s in (3, 4, 5))
  ffn2_segs = tuple(s - CAST_MOD_FROM for s in (6, 7, 8))
  xp, s_prompt = _mix_prompt_call(xp, mod_b, mix_segs, row(norm2_w[l]), w_in3, w_a2_p, row(b_a[l]),
                                  ltri, masks, gnw, w_pa_b, row(gm_ln_w[l]), row(gm_ln_b[l]),
                                  gm_ws[l], bs_full, w_pb_b, w_o_b, batch=batch, seq=seq)
  p_s, cols, v_s, oa0 = _sample_proj_call(xs, mod_b, mix_segs, row(norm2_w[l]), w_in3, w_a2_p,
                                          row(b_a[l]))
  yp, s_sample, ys, gvn = _ffn_stream_call(
      xp, mod_b, ffn2_segs, row(norm3_w[l]), w13_2, w2_2, normf, cols, v_s, oa0, state_gla,
      xs, p_s, mix_segs, gnw, w_pa_b, row(gm_ln_w[l]), row(gm_ln_b[l]), ws0, bs0, w_pb_b, w_o_b,
      rows_per_batch=seq, tm=TM_FFN)

  return (yp.reshape(batch, seq, D), ys.reshape(dec, 1, D), s_prompt, s_sample,
          gvn.reshape(1, dec, 1, D))
```

```python
import functools

import numpy as np
import jax
import jax.numpy as jnp
from jax import lax
from jax.experimental import pallas as pl
from jax.experimental.pallas import tpu as pltpu

F32 = jnp.float32
BF16 = jnp.bfloat16

D = 1024
HEADS = 4
HK = 128
HV = 256
DK = HEADS * HK
DV = HEADS * HV
LOWRANK = 16
TAU = 16.0
GROUPS = 4
GC = D // GROUPS
GM_CHUNK = 128
F = 2816
N_MOD = 9
EPS = 1e-6
LOG2E = 1.4426950408889634

CH = 128
N_LEVELS = 7
LANE = 128

C_Q, C_K, C_ALR, C_U, C_GV, C_V, C_R, C_GA, C_GB = 0, 512, 1024, 1152, 2176, 3200, 4224, 5248, 6272
D_P = C_GB + D

PROJ_QK = (0, 0, 2 * DK, C_Q)
PROJ_ALR = (1, 0, LANE, C_ALR)
PROJ_UGV = (2, 0, 2 * D, C_U)
PROJ_V = (0, 2 * DK, 2 * DK + DV, C_V)
PROJ_R = (0, 2 * DK + DV, 2 * DK + 2 * DV, C_R)
PROJ_GATES = (2, 2 * D, 4 * D, C_GA)
PROJ_ALL = (PROJ_QK, PROJ_ALR, PROJ_UGV, PROJ_V, PROJ_R, PROJ_GATES)

TM_P = 512
TM_FFN = 512
NBS = 4
VMEM_LIMIT = 58 * 1024 * 1024


def _gla_constants():
  t = np.arange(CH)
  masks = []
  for lvl in range(N_LEVELS):
    m = 1 << lvl
    upper = t >= (t // (2 * m)) * (2 * m) + m
    same_pair = (t[:, None] // (2 * m)) == (t[None, :] // (2 * m))
    masks.append((upper[:, None] & (~upper)[None, :] & same_pair).astype(np.float32))
  masks.append(np.eye(CH, dtype=np.float32))
  ltri = (t[None, :] <= t[:, None]).astype(np.float32)
  return ltri, np.stack(masks)


def _level_exponents(g, b):
  t = lax.broadcasted_iota(jnp.int32, g.shape, 0)
  g_prev = pltpu.roll(g, 1, 0)
  g_next = pltpu.roll(g, CH - 1, 0)
  r4 = t & 3
  levels = [
      jnp.where((t & 1) == 1, g, 0.0),
      jnp.where(r4 == 0, g_next, jnp.where(r4 == 1, 0.0, jnp.where(r4 == 2, g, g + g_prev))),
  ]
  groups = CH // 8
  row = lambda i: jnp.broadcast_to(b[i:i + 1, :], (8, b.shape[1]))
  b3 = [row(8 * j + 3) for j in range(groups)]
  b7 = [row(8 * j + 7) for j in range(groups)]
  levels.append(-jnp.abs(b - jnp.concatenate(b3, axis=0)))
  for lvl in range(3, N_LEVELS):
    m = 1 << lvl
    ref = [b7[((8 * j) // (2 * m)) * (2 * m) // 8 + m // 8 - 1] for j in range(groups)]
    levels.append(-jnp.abs(b - jnp.concatenate(ref, axis=0)))
  to_end = -jnp.abs(b - jnp.concatenate([b7[groups - 1]] * groups, axis=0))
  return levels, to_end


def _dot(a, b):
  return jnp.dot(a, b, preferred_element_type=F32)


def _rms(x, w):
  ms = jnp.mean(x * x, axis=-1, keepdims=True)
  return x * lax.rsqrt(ms + EPS) * w


def _norm_mod(x, nw, sc, sh):
  if sc.shape[0] != 1:
    return (_rms(x, nw) * (1.0 + sc) + sh).astype(BF16)
  ms = jnp.mean(x * x, axis=-1, keepdims=True)
  return (x * lax.rsqrt(ms + EPS) * (nw * (1.0 + sc)) + sh).astype(BF16)


def _gelu(x):
  return 0.5 * x * (1.0 + lax.erf(x * (2.0 ** -0.5)))


def _log_sigmoid(z):
  return jnp.minimum(z, 0.0) - jnp.log1p(jnp.exp(-jnp.abs(z)))


def _const_spec(shape):
  nd = len(shape)
  return pl.BlockSpec(shape, lambda *_: (0,) * nd, pipeline_mode=pl.Buffered(1))


ADA_STEPS = 4


def _ada_kernel(c_ref, w_ref, b_ref, o_ref):
  c = c_ref[...]
  s = (c * jax.nn.sigmoid(c)).astype(BF16)
  o_ref[...] = _dot(s, w_ref[...].astype(BF16)) + b_ref[...]


def _ada_call(c_all, w_ada, b_ada):
  rows = c_all.shape[0]
  tn = CAST_MOD_FROM * D // ADA_STEPS
  assert tn % LANE == 0
  return pl.pallas_call(
      _ada_kernel,
      grid=(ADA_STEPS,),
      in_specs=[
          pl.BlockSpec((rows, D), lambda i: (0, 0)),
          pl.BlockSpec((D, tn), lambda i: (0, i)),
          pl.BlockSpec((1, tn), lambda i: (0, i)),
      ],
      out_specs=pl.BlockSpec((rows, tn), lambda i: (0, i)),
      out_shape=jax.ShapeDtypeStruct((rows, CAST_MOD_FROM * D), F32),
      compiler_params=pltpu.CompilerParams(
          dimension_semantics=("arbitrary",), vmem_limit_bytes=VMEM_LIMIT),
      name="ada",
  )(c_all, w_ada, b_ada)


def _ffn_rows(x_ref, sh_ref, sc_ref, g_ref, nw_ref, w13_ref, w2_ref, nf_ref, o_ref, mod_row=None):
  pick = (lambda ref: ref[...]) if mod_row is None else (lambda ref: ref[pl.ds(mod_row, 1), :])
  x = x_ref[...]
  h = _norm_mod(x, nw_ref[...], pick(sc_ref), pick(sh_ref))
  a = _dot(h, w13_ref[:, :F])
  b = _dot(h, w13_ref[:, F:])
  p = (a * jax.nn.sigmoid(a) * b).astype(BF16)
  out = x + 0.5 * pick(g_ref) * _dot(p, w2_ref[...])
  if nf_ref is not None:
    out = _rms(out, nf_ref[...])
  o_ref[...] = out


W_LOAD_CHUNKS = 8


def _load_cast(w_hbm, w_ref, buf_ref, sem_ref):
  rows = w_hbm.shape[0] // W_LOAD_CHUNKS
  copy = lambda k: pltpu.make_async_copy(
      w_hbm.at[pl.ds(k * rows, rows), :], buf_ref.at[k % 2], sem_ref.at[k % 2])
  copy(0).start()
  for k in range(W_LOAD_CHUNKS):
    if k + 1 < W_LOAD_CHUNKS:
      copy(k + 1).start()
    copy(k).wait()
    w_ref[k * rows:(k + 1) * rows, :] = buf_ref[k % 2].astype(BF16)


def _ffn_kernel(x_ref, sh_ref, sc_ref, g_ref, xs_ref, shs_ref, scs_ref, gs_ref, nw_ref, w13_hbm,
                w2_hbm, wta_ref, wtalr_ref, wtb_ref, w_pa_ref, w_pb_ref, w_o_ref, n13_ref, n2_ref,
                c_ref, wada_ref, bada_ref,
                o_ref, os_ref, wa_ref, walr_ref, wb_ref, pa_ref, pb_ref, po_ref, o13_ref, o2_ref,
                modb_ref, w13_ref, w2_ref, buf13_ref, buf2_ref, sem13_ref, sem2_ref, *, tiles):
  i = pl.program_id(0)
  n = pl.num_programs(0) - 1

  @pl.when(i == 0)
  def _():
    _load_cast(w13_hbm, w13_ref, buf13_ref, sem13_ref)
    _load_cast(w2_hbm, w2_ref, buf2_ref, sem2_ref)

  @pl.when(i < n)
  def _():
    _ffn_rows(x_ref, sh_ref, sc_ref, g_ref, nw_ref, w13_ref, w2_ref, None, o_ref,
              mod_row=i // tiles)
    c = c_ref[...]
    modb_ref[...] = _dot((c * jax.nn.sigmoid(c)).astype(BF16),
                         wada_ref[...].astype(BF16)) + bada_ref[...]
    wa_ref[...] = wta_ref[...].T.astype(BF16)
    wb_ref[...] = wtb_ref[...].T.astype(BF16)
    alr = jnp.concatenate([wtalr_ref[...], jnp.zeros((LANE - LOWRANK, D), F32)], axis=0)
    walr_ref[...] = alr.T.astype(BF16)
    pa_ref[...] = w_pa_ref[...].astype(BF16)
    pb_ref[...] = w_pb_ref[...].astype(BF16)
    po_ref[...] = w_o_ref[...].astype(BF16)
    o13_ref[...] = n13_ref[...].astype(BF16)
    o2_ref[...] = n2_ref[...].astype(BF16)

  @pl.when(i == n)
  def _():
    _ffn_rows(xs_ref, shs_ref, scs_ref, gs_ref, nw_ref, w13_ref, w2_ref, None, os_ref)


CAST_STEPS = 32
CAST_MOD_FROM = 3
CAST_MOD_COLS = 256
W2_CAST_ROWS = 2 * F // CAST_STEPS


def _mod_specs(rows_s, batch, segs):
  assert rows_s % batch == 0
  one = pl.Buffered(1)
  prompt = [pl.BlockSpec((batch, D), lambda *_, s=s: (rows_s // batch, s), pipeline_mode=one)
            for s in segs]
  sample = [pl.BlockSpec((rows_s, D), lambda *_, s=s: (0, s), pipeline_mode=one) for s in segs]
  return prompt, sample


def _ffn_call(x, xs, mod, segs, nw, w13, w2, cast_weights, *, rows_per_batch, tm):
  rows, rows_s = x.shape[0], xs.shape[0]
  n = rows // tm
  tiles = rows_per_batch // tm
  last = lambda i: jnp.minimum(i, n - 1)
  x_spec = pl.BlockSpec((tm, D), lambda i: (last(i), 0))
  xs_spec = pl.BlockSpec((rows_s, D), lambda i: (0, 0))
  mod_prompt, mod_sample = _mod_specs(rows_s, rows // rows_per_batch, segs)
  assert n == CAST_STEPS and D % CAST_STEPS == 0
  assert D % (16 * W_LOAD_CHUNKS) == 0 and F % (16 * W_LOAD_CHUNKS) == 0
  w_t, w_pa, w_pb, w_o, n13, n2, c_all, w_ada, b_ada = cast_weights
  mod_cols = (N_MOD - CAST_MOD_FROM) * D
  assert mod_cols % CAST_MOD_COLS == 0 and mod_cols // CAST_MOD_COLS <= CAST_STEPS
  mod_blk = lambda i: jnp.minimum(i, mod_cols // CAST_MOD_COLS - 1)
  mod_off = CAST_MOD_FROM * D // CAST_MOD_COLS
  lo = 2 * DK + 2 * DV
  n_a, n_b = lo // LANE, (w_t.shape[0] - lo - LOWRANK) // LANE
  assert n_a <= CAST_STEPS and n_b == CAST_STEPS
  rb = D // CAST_STEPS
  step = lambda i: jnp.minimum(i, CAST_STEPS - 1)
  a_blk = lambda i: jnp.minimum(i, n_a - 1)
  elem = lambda r: (pl.Element(r), pl.Element(D))
  row_blk = lambda cols: pl.BlockSpec((rb, cols), lambda i: (step(i), 0))
  w2_blk = pl.BlockSpec((W2_CAST_ROWS, D), lambda i: (step(i) // 2, 0))
  in_specs = [x_spec] + mod_prompt + [_const_spec((rows_s, D))] + mod_sample + [
      _const_spec((1, D)), pl.BlockSpec(memory_space=pl.ANY), pl.BlockSpec(memory_space=pl.ANY),
      pl.BlockSpec(elem(LANE), lambda i: (pl.multiple_of(LANE * a_blk(i), LANE), 0)),
      pl.BlockSpec(elem(LOWRANK), lambda i: (lo, 0)),
      pl.BlockSpec(elem(LANE), lambda i: (pl.multiple_of(lo + LOWRANK + LANE * step(i), LOWRANK), 0)),
      row_blk(D), row_blk(D), row_blk(D), row_blk(2 * F), w2_blk,
      _const_spec(c_all.shape),
      pl.BlockSpec((D, CAST_MOD_COLS), lambda i: (0, mod_off + mod_blk(i))),
      pl.BlockSpec((1, CAST_MOD_COLS), lambda i: (0, mod_off + mod_blk(i)))]
  out_specs = [
      x_spec, xs_spec,
      pl.BlockSpec((D, LANE), lambda i: (0, a_blk(i))),
      pl.BlockSpec((D, LANE), lambda i: (0, 0)),
      pl.BlockSpec((D, LANE), lambda i: (0, step(i))),
      row_blk(D), row_blk(D), row_blk(D), row_blk(2 * F), w2_blk,
      pl.BlockSpec((c_all.shape[0], CAST_MOD_COLS), lambda i: (0, mod_blk(i)))]
  bf = lambda r, c: jax.ShapeDtypeStruct((r, c), BF16)
  out_shape = [jax.ShapeDtypeStruct((rows, D), F32), jax.ShapeDtypeStruct((rows_s, D), F32),
               bf(D, lo), bf(D, LANE), bf(D, n_b * LANE),
               bf(D, D), bf(D, D), bf(D, D), bf(D, 2 * F), bf(F, D),
               jax.ShapeDtypeStruct((c_all.shape[0], mod_cols), F32)]
  return pl.pallas_call(
      functools.partial(_ffn_kernel, tiles=tiles),
      grid=(n + 1,),
      in_specs=in_specs,
      out_specs=out_specs,
      out_shape=out_shape,
      scratch_shapes=[
          pltpu.VMEM((D, 2 * F), BF16),
          pltpu.VMEM((F, D), BF16),
          pltpu.VMEM((2, D // W_LOAD_CHUNKS, 2 * F), F32),
          pltpu.VMEM((2, F // W_LOAD_CHUNKS, D), F32),
          pltpu.SemaphoreType.DMA((2,)),
          pltpu.SemaphoreType.DMA((2,)),
      ],
      compiler_params=pltpu.CompilerParams(
          dimension_semantics=("arbitrary",), vmem_limit_bytes=VMEM_LIMIT),
      name="ffn1",
  )(x, mod, mod, mod, xs, mod, mod, mod, nw, w13, w2,
    w_t, w_t, w_t, w_pa, w_pb, w_o, n13, n2, c_all, w_ada, b_ada)


def _project_piece(h, w_refs, p_ref, piece):
  idx, lo, hi, dst = piece
  p_ref[:, dst:dst + hi - lo] = _dot(h, w_refs[idx][:, lo:hi])


def _log_decay(p_ref, w_a2_ref, b_a_ref):
  a_lr = p_ref[:, C_ALR:C_ALR + LANE].astype(BF16)
  z = _dot(a_lr, w_a2_ref[...]) + b_a_ref[...]
  return _log_sigmoid(z) * (1.0 / TAU)


def _head_norm_gate(o, p_ref, gnw_ref):
  parts = []
  for hh in range(HEADS):
    sl = slice(hh * HV, (hh + 1) * HV)
    parts.append(_rms(o[:, sl], gnw_ref[:, sl]))
  r = p_ref[:, C_R:C_R + DV]
  return jnp.concatenate(parts, axis=-1) * (r * jax.nn.sigmoid(r))


def _gv_norm(p_ref, lnw_ref, lnb_ref):
  gv = _gelu(p_ref[:, C_GV:C_GV + D])
  mu = jnp.mean(gv, axis=-1, keepdims=True)
  d = gv - mu
  var = jnp.mean(d * d, axis=-1, keepdims=True)
  return d * lax.rsqrt(var + EPS) * lnw_ref[...] + lnb_ref[...]


def _merge_out(x, g2, ya, yb, p_ref, w_o_ref):
  ga = p_ref[:, C_GA:C_GA + D]
  gb = p_ref[:, C_GB:C_GB + D]
  merged = (jax.nn.sigmoid(ga) * ya + jax.nn.sigmoid(gb) * yb).astype(BF16)
  return x + g2 * _dot(merged, w_o_ref[...])


def _mix_prompt_kernel(x_ref, sh_ref, sc_ref, g2_ref, nw_ref,
                       wa_ref, walr_ref, wb_ref, w_a2_ref, b_a_ref,
                       ltri_ref, mask_ref, gnw_ref, w_pa_ref, lnw_ref, lnb_ref, ws_ref, bs_ref,
                       w_pb_ref, w_o_ref, o_ref, s_out_ref,
                       st_ref, p_ref, oa_ref, mx_ref, kt_ref, s_ref):
  j = pl.program_id(1)
  tm = TM_P
  w_refs = (wa_ref, walr_ref, wb_ref)
  batch_row = pl.ds(pl.program_id(0), 1)
  sh, sc, g2 = sh_ref[batch_row, :], sc_ref[batch_row, :], g2_ref[batch_row, :]

  @pl.when(j == 0)
  def _():
    st_ref[...] = jnp.zeros_like(st_ref)

  ltri = ltri_ref[...]
  ti = lax.broadcasted_iota(jnp.int32, (GM_CHUNK, GM_CHUNK), 0)
  si = lax.broadcasted_iota(jnp.int32, (GM_CHUNK, GM_CHUNK), 1)

  def mix_stages():
    oa = oa_ref
    mx = mx_ref
    env = {}

    def gate():
      env["g"] = _log_decay(p_ref, w_a2_ref, b_a_ref) * LOG2E

    def cumsum():
      cums = []
      for c in range(tm // CH):
        gc = env["g"][c * CH:(c + 1) * CH, :]
        g_hi = gc.astype(BF16)
        g_lo = (gc - g_hi.astype(F32)).astype(BF16)
        cums.append(_dot(ltri, g_hi) + _dot(ltri, g_lo))
      env["cums"] = cums

    def prep():
      decays = []
      qk = {}
      for c in range(tm // CH):
        rows = slice(c * CH, (c + 1) * CH)
        b = env["cums"][c]
        e_lvl, e_end = _level_exponents(env["g"][rows, :], b)
        q = (p_ref[rows, C_Q:C_Q + DK] * (HK ** -0.5)).astype(BF16)
        k = p_ref[rows, C_K:C_K + DK].astype(BF16)
        for lvl in range(N_LEVELS):
          e = jnp.exp2(e_lvl[lvl]).astype(BF16)
          qk[c, 2 * lvl] = q * e
          qk[c, 2 * lvl + 1] = k * e
        qk[c, 2 * N_LEVELS] = q
        qk[c, 2 * N_LEVELS + 1] = k
        qk[c, 2 * N_LEVELS + 2] = q * jnp.exp2(b).astype(BF16)
        qk[c, 2 * N_LEVELS + 3] = k * jnp.exp2(e_end).astype(BF16)
        decays.append(jnp.exp2(b[CH - 1:CH, :]))
      env["decays"] = decays
      env["qk"] = qk
      env["gvn"] = _gv_norm(p_ref, lnw_ref, lnb_ref).astype(BF16)
      env["u"] = _gelu(p_ref[:, C_U:C_U + D])
      env["pending"] = None

    def level_scores(c, hh):
      ck = slice(hh * HK, (hh + 1) * HK)
      scores = None
      for lvl in range(N_LEVELS + 1):
        kt_ref[lvl] = env["qk"][c, 2 * lvl + 1][:, ck].T
        part = mask_ref[lvl] * _dot(env["qk"][c, 2 * lvl][:, ck], kt_ref[lvl])
        scores = part if scores is None else scores + part
      return scores.astype(BF16)

    def finish(c, hh, scores):
      rows = slice(c * CH, (c + 1) * CH)
      ck = slice(hh * HK, (hh + 1) * HK)
      v32 = p_ref[rows, C_V + hh * HV:C_V + (hh + 1) * HV]
      st = st_ref[hh]
      s_ref[hh] = st.T.astype(BF16)
      oa[rows, hh * HV:(hh + 1) * HV] = (
          _dot(scores, v32.astype(BF16))
          + _dot(env["qk"][c, 2 * N_LEVELS + 2][:, ck], s_ref[hh]))
      st_ref[hh] = (st * env["decays"][c][:, ck]
                    + _dot(v32.T.astype(BF16), env["qk"][c, 2 * N_LEVELS + 3][:, ck]))

    def gla(c):
      def run():
        for hh in range(HEADS):
          scores = level_scores(c, hh)
          if env["pending"] is not None:
            finish(*env["pending"])
          env["pending"] = (c, hh, scores)
      return run

    def gmlp():
      for gg in range(GROUPS):
        ws = jnp.where(si <= ti, ws_ref[gg], 0.0).astype(BF16)
        cols = slice(gg * GC, (gg + 1) * GC)
        for c in range(tm // GM_CHUNK):
          rows = slice(c * GM_CHUNK, (c + 1) * GM_CHUNK)
          mx[rows, cols] = _dot(ws, env["gvn"][rows, cols]) + bs_ref[:, cols]
      finish(*env["pending"])

    def branches():
      env["yb"] = _dot((env["u"] * mx[...]).astype(BF16), w_pb_ref[...])
      env["ya"] = _dot(_head_norm_gate(oa[...], p_ref, gnw_ref).astype(BF16), w_pa_ref[...])

    def out():
      ga = p_ref[:, C_GA:C_GA + D]
      gb = p_ref[:, C_GB:C_GB + D]
      merged = (jax.nn.sigmoid(ga) * env["ya"] + jax.nn.sigmoid(gb) * env["yb"]).astype(BF16)
      for r0 in range(0, tm, tm // 2):
        sl = slice(r0, r0 + tm // 2)
        o_ref[sl, :] = x_ref[sl, :] + g2 * _dot(merged[sl, :], w_o_ref[...])

    return gate, cumsum, prep, [gla(c) for c in range(tm // CH)], gmlp, branches, out

  gate, cumsum, prep, glas, gmlp, branches, out = mix_stages()
  halves = []
  for r0 in range(0, tm, tm // 2):
    sl = slice(r0, r0 + tm // 2)
    halves.append(_norm_mod(x_ref[sl, :], nw_ref[...], sc, sh))
    idx, lo, hi, dst = PROJ_QK
    p_ref[sl, dst:dst + hi - lo] = _dot(halves[-1], w_refs[idx][:, lo:hi])
  h = jnp.concatenate(halves, axis=0)
  proj = lambda piece: _project_piece(h, w_refs, p_ref, piece)
  proj(PROJ_ALR)
  proj(PROJ_UGV)
  gate()
  proj(PROJ_V)
  cumsum()
  proj(PROJ_R)
  proj(PROJ_GATES)
  prep()
  for gla_chunk in glas:
    gla_chunk()
  gmlp()
  branches()
  out()

  @pl.when(j == pl.num_programs(1) - 1)
  def _():
    for hh in range(HEADS):
      s_out_ref[hh] = st_ref[hh].T


def _mix_prompt_call(x, mod, segs, nw, w_in3, w_a2, b_a, ltri, masks, gnw, w_pa, lnw, lnb, ws,
                     bs_full, w_pb, w_o, *, batch, seq):
  tm = TM_P
  tiles = seq // tm
  row_spec = pl.BlockSpec((tm, D), lambda b, j: (b * tiles + j, 0))

  mod_prompt, _ = _mod_specs(mod.shape[0] - batch, batch, segs)
  wa, walr, wb = w_in3
  in_specs = [
      row_spec, *mod_prompt,
      _const_spec((1, D)), _const_spec(wa.shape), _const_spec(walr.shape), _const_spec(wb.shape),
      _const_spec((LANE, DK)), _const_spec((1, DK)),
      _const_spec(ltri.shape), _const_spec(masks.shape), _const_spec((1, DV)),
      _const_spec((DV, D)), _const_spec((1, D)), _const_spec((1, D)),
      _const_spec((GROUPS, GM_CHUNK, GM_CHUNK)), _const_spec((GM_CHUNK, D)),
      _const_spec((D, D)), _const_spec((D, D)),
  ]
  out_specs = [
      row_spec,
      pl.BlockSpec((None, None, HEADS, HK, HV), lambda b, j: (0, b, 0, 0, 0)),
  ]
  return pl.pallas_call(
      _mix_prompt_kernel,
      grid=(batch, tiles),
      in_specs=in_specs,
      out_specs=out_specs,
      out_shape=[jax.ShapeDtypeStruct((batch * seq, D), F32),
                 jax.ShapeDtypeStruct((1, batch, HEADS, HK, HV), F32)],
      scratch_shapes=[
          pltpu.VMEM((HEADS, HV, HK), F32),
          pltpu.VMEM((tm, D_P), F32),
          pltpu.VMEM((tm, DV), F32),
          pltpu.VMEM((tm, D), F32),
          pltpu.VMEM((N_LEVELS + 1, HK, CH), BF16),
          pltpu.VMEM((HEADS, HK, HV), BF16),
      ],
      compiler_params=pltpu.CompilerParams(
          dimension_semantics=("arbitrary", "arbitrary"), vmem_limit_bytes=VMEM_LIMIT),
      name="mix_prompt",
  )(x, mod, mod, mod, nw, wa, walr, wb, w_a2, b_a, ltri, masks, gnw, w_pa,
    lnw, lnb, ws, bs_full, w_pb, w_o)


def _sample_proj_kernel(x_ref, sh_ref, sc_ref, nw_ref, wa_ref, walr_ref, wb_ref, w_a2_ref, b_a_ref,
                        p_ref, col_ref, v_ref, oa_ref):
  h = _norm_mod(x_ref[...], nw_ref[...], sc_ref[...], sh_ref[...])
  for piece in PROJ_ALL:
    _project_piece(h, (wa_ref, walr_ref, wb_ref), p_ref, piece)
  g = _log_decay(p_ref, w_a2_ref, b_a_ref)
  q = p_ref[:, C_Q:C_Q + DK] * (HK ** -0.5)
  k = p_ref[:, C_K:C_K + DK]
  eb = jnp.exp(g)
  qe = q * eb
  v_ref[...] = p_ref[:, C_V:C_V + DV]
  for hh in range(HEADS):
    ck = slice(hh * HK, (hh + 1) * HK)
    col_ref[0, ck, :] = eb[:, ck].T
    col_ref[1, ck, :] = qe[:, ck].T
    col_ref[2, ck, :] = k[:, ck].T
    qk = jnp.sum(q[:, ck] * k[:, ck], axis=-1, keepdims=True)
    oa_ref[:, hh * HV:(hh + 1) * HV] = qk * p_ref[:, C_V + hh * HV:C_V + (hh + 1) * HV]


def _sample_state_rows(i, col_ref, v_ref, s_ref, s_out_ref, os_ref):
  rows = os_ref.shape[0]
  shift = (rows - i * NBS) % rows
  eb_t, qe_t, k_t = (pltpu.roll(col_ref[a], shift, 1) for a in range(3))
  for nl in range(NBS):
    n = i * NBS + nl
    for hh in range(HEADS):
      ck = slice(hh * HK, (hh + 1) * HK)
      cv = slice(hh * HV, (hh + 1) * HV)
      st = s_ref[nl, hh]
      o_inter = jnp.sum(qe_t[ck, nl:nl + 1] * st, axis=0, keepdims=True)
      os_ref[pl.ds(n, 1), cv] = os_ref[pl.ds(n, 1), cv] + o_inter
      s_out_ref[nl, hh] = eb_t[ck, nl:nl + 1] * st + k_t[ck, nl:nl + 1] * v_ref[pl.ds(n, 1), cv]


def _ffn_stream_kernel(x_ref, sh_ref, sc_ref, g_ref, nw_ref, w13_ref, w2_ref, nf_ref,
                       col_ref, v_ref, oa0_ref, s_ref,
                       xs_ref, p_ref, g2_ref, shs_ref, scs_ref, gs_ref, gnw_ref, w_pa_ref, lnw_ref,
                       lnb_ref, ws0_ref, bs0_ref, w_pb_ref, w_o_ref,
                       o_ref, os_ref, s_out_ref, ys_hbm, gvn_hbm, x2_ref, ys_ref, gvn_ref, sem_ref,
                       *, tiles):
  i = pl.program_id(0)
  n = pl.num_programs(0) - 1

  @pl.when(i == 0)
  def _():
    os_ref[...] = oa0_ref[...]

  @pl.when(i < n)
  def _():
    _sample_state_rows(i, col_ref, v_ref, s_ref, s_out_ref, os_ref)
    _ffn_rows(x_ref, sh_ref, sc_ref, g_ref, nw_ref, w13_ref, w2_ref, nf_ref, o_ref,
              mod_row=i // tiles)

  @pl.when(i == n)
  def _():
    ya = _dot(_head_norm_gate(os_ref[...], p_ref, gnw_ref).astype(BF16), w_pa_ref[...])
    gvn_copy = pltpu.make_async_copy(gvn_ref, gvn_hbm.at[0, :, 0, :], sem_ref.at[0])
    ys_copy = pltpu.make_async_copy(ys_ref, ys_hbm.at[:, 0, :], sem_ref.at[1])
    gvn = _gv_norm(p_ref, lnw_ref, lnb_ref)
    gvn_ref[...] = gvn
    gvn_copy.start()
    mixed = ws0_ref[...] * gvn + bs0_ref[...]
    u = _gelu(p_ref[:, C_U:C_U + D])
    yb = _dot((u * mixed).astype(BF16), w_pb_ref[...])
    x2_ref[...] = _merge_out(xs_ref[...], g2_ref[...], ya, yb, p_ref, w_o_ref)
    _ffn_rows(x2_ref, shs_ref, scs_ref, gs_ref, nw_ref, w13_ref, w2_ref, nf_ref, ys_ref)
    ys_copy.start()
    gvn_copy.wait()
    ys_copy.wait()


def _sample_proj_call(x, mod, segs, nw, w_in3, w_a2, b_a):
  rows = x.shape[0]
  _, mod_sample = _mod_specs(rows, mod.shape[0] - rows, segs[:2])
  wa, walr, wb = w_in3
  shapes = [(rows, D_P), (3, DK, rows), (rows, DV), (rows, DV)]
  return pl.pallas_call(
      _sample_proj_kernel,
      grid=(1,),
      in_specs=[_const_spec((rows, D)), *mod_sample, _const_spec((1, D)), _const_spec(wa.shape),
                _const_spec(walr.shape), _const_spec(wb.shape), _const_spec((LANE, DK)),
                _const_spec((1, DK))],
      out_specs=[pl.BlockSpec(s, lambda i, nd=len(s): (0,) * nd) for s in shapes],
      out_shape=[jax.ShapeDtypeStruct(s, F32) for s in shapes],
      compiler_params=pltpu.CompilerParams(
          dimension_semantics=("arbitrary",), vmem_limit_bytes=VMEM_LIMIT),
      name="sample_proj",
  )(x, mod, mod, nw, wa, walr, wb, w_a2, b_a)


def _ffn_stream_call(x, mod, segs, nw, w13, w2, nf, cols, v_s, oa0, state,
                     xs, p_s, mix_segs, gnw, w_pa, lnw, lnb, ws0, bs0, w_pb, w_o, *, rows_per_batch, tm):
  rows, rows_s = x.shape[0], xs.shape[0]
  n = rows // tm
  assert n * NBS == rows_s
  last = lambda i: jnp.minimum(i, n - 1)
  x_spec = pl.BlockSpec((tm, D), lambda i: (last(i), 0))
  mod_prompt, mod_sample = _mod_specs(rows_s, rows // rows_per_batch, segs)
  _, mod_g2 = _mod_specs(rows_s, rows // rows_per_batch, mix_segs[2:])
  state_spec = pl.BlockSpec((None, NBS, HEADS, HK, HV), lambda i: (0, last(i), 0, 0, 0))
  hbm_out = pl.BlockSpec(memory_space=pl.ANY)
  outs = pl.pallas_call(
      functools.partial(_ffn_stream_kernel, tiles=rows_per_batch // tm),
      grid=(n + 1,),
      in_specs=[x_spec, *mod_prompt, _const_spec((1, D)), _const_spec((D, 2 * F)),
                _const_spec((F, D)), _const_spec((1, D)), _const_spec(cols.shape),
                _const_spec((rows_s, DV)), _const_spec((rows_s, DV)), state_spec,
                _const_spec((rows_s, D)), _const_spec((rows_s, D_P)), *mod_g2, *mod_sample,
                _const_spec((1, DV)), _const_spec((DV, D)), _const_spec((1, D)), _const_spec((1, D)),
                _const_spec((1, D)), _const_spec((1, D)), _const_spec((D, D)), _const_spec((D, D))],
      out_specs=[x_spec, pl.BlockSpec((rows_s, DV), lambda i: (0, 0)), state_spec, hbm_out, hbm_out],
      out_shape=[jax.ShapeDtypeStruct((rows, D), F32), jax.ShapeDtypeStruct((rows_s, DV), F32),
                 jax.ShapeDtypeStruct(state.shape, F32), jax.ShapeDtypeStruct((rows_s, 1, D), F32),
                 jax.ShapeDtypeStruct((1, rows_s, 1, D), F32)],
      scratch_shapes=[pltpu.VMEM((rows_s, D), F32), pltpu.VMEM((rows_s, D), F32),
                      pltpu.VMEM((rows_s, D), F32), pltpu.SemaphoreType.DMA((2,))],
      compiler_params=pltpu.CompilerParams(
          dimension_semantics=("arbitrary",), vmem_limit_bytes=VMEM_LIMIT),
      name="ffn2",
  )(x, mod, mod, mod, nw, w13, w2, nf, cols, v_s, oa0, state,
    xs, p_s, mod, mod, mod, mod, gnw, w_pa, lnw, lnb, ws0, bs0, w_pb, w_o)
  y, _, s_new, ys, gvn = outs
  return y, s_new, ys, gvn


def kernel(x_prompt, x_sample, state_gla, c_prompt, c_sample, w_ada, b_ada, norm1_w, ffn1_w13, ffn1_w2, norm2_w, w_in, w_a2, b_a, gla_norm_w, w_pa, gm_ln_w, gm_ln_b, gm_ws, gm_bs, w_pb, w_o, norm3_w, ffn2_w13, ffn2_w2, normf_w):
  batch, seq, _ = x_prompt.shape
  dec = x_sample.shape[0]
  depth = w_ada.shape[0]
  assert depth == 1 and x_sample.shape[1] == 1
  assert seq % TM_P == 0 and seq % TM_FFN == 0

  ltri_np, masks_np = _gla_constants()
  ltri = jnp.asarray(ltri_np, BF16)
  masks = jnp.asarray(masks_np, F32)

  xp = x_prompt.reshape(batch * seq, D)
  xs = x_sample.reshape(dec, D)
  l = 0
  c_all = jnp.concatenate([c_sample, c_prompt], axis=0)
  b_ada_row = b_ada[l].reshape(1, N_MOD * D)
  mod_a = _ada_call(c_all, w_ada[l], b_ada_row)

  row = lambda a: a.reshape(1, -1)
  w_a2_p = jnp.pad(w_a2[l], ((0, LANE - LOWRANK), (0, 0))).astype(BF16)
  gnw = gla_norm_w[l].reshape(1, DV)
  bs_full = jnp.repeat(gm_bs[l].T, GC, axis=1)
  ws0 = jnp.repeat(gm_ws[l][:, 0, 0], GC).reshape(1, D)
  bs0 = bs_full[0:1]
  normf = row(normf_w)

  cast_weights = (jnp.transpose(w_in[l]), w_pa[l], w_pb[l], w_o[l], ffn2_w13[l], ffn2_w2[l],
                  c_all, w_ada[l], b_ada_row)
  (xp, xs, wa, walr, wb, w_pa_b, w_pb_b, w_o_b, w13_2, w2_2, mod_b) = _ffn_call(
      xp, xs, mod_a, (0, 1, 2), row(norm1_w[l]), ffn1_w13[l], ffn1_w2[l], cast_weights,
      rows_per_batch=seq, tm=TM_FFN)
  w_in3 = (wa, walr, wb)
  mix_segs = tuple(s - CAST_MOD_FROM for s in (3, 4, 5))
  ffn2_segs = tuple(s - CAST_MOD_FROM for s in (6, 7, 8))
  xp, s_prompt = _mix_prompt_call(xp, mod_b, mix_segs, row(norm2_w[l]), w_in3, w_a2_p, row(b_a[l]),
                                  ltri, masks, gnw, w_pa_b, row(gm_ln_w[l]), row(gm_ln_b[l]),
                                  gm_ws[l], bs_full, w_pb_b, w_o_b, batch=batch, seq=seq)
  p_s, cols, v_s, oa0 = _sample_proj_call(xs, mod_b, mix_segs, row(norm2_w[l]), w_in3, w_a2_p,
                                          row(b_a[l]))
  yp, s_sample, ys, gvn = _ffn_stream_call(
      xp, mod_b, ffn2_segs, row(norm3_w[l]), w13_2, w2_2, normf, cols, v_s, oa0, state_gla,
      xs, p_s, mix_segs, gnw, w_pa_b, row(gm_ln_w[l]), row(gm_ln_b[l]), ws0, bs0, w_pb_b, w_o_b,
      rows_per_batch=seq, tm=TM_FFN)

  return yp.reshape(batch, seq, D), ys, s_prompt, s_sample, gvn
```

```python
import functools

import numpy as np
import jax
import jax.numpy as jnp
from jax import lax
from jax.experimental import pallas as pl
from jax.experimental.pallas import tpu as pltpu

F32 = jnp.float32
BF16 = jnp.bfloat16

D = 1024
HEADS = 4
HK = 128
HV = 256
DK = HEADS * HK
DV = HEADS * HV
LOWRANK = 16
TAU = 16.0
GROUPS = 4
GC = D // GROUPS
GM_CHUNK = 128
F = 2816
N_MOD = 9
EPS = 1e-6
LOG2E = 1.4426950408889634

CH = 128
N_LEVELS = 7
LANE = 128

C_Q, C_K, C_ALR, C_U, C_GV, C_V, C_R, C_GA, C_GB = 0, 512, 1024, 1152, 2176, 3200, 4224, 5248, 6272
D_P = C_GB + D

PROJ_QK = (0, 0, 2 * DK, C_Q)
PROJ_ALR = (1, 0, LANE, C_ALR)
PROJ_UGV = (2, 0, 2 * D, C_U)
PROJ_V = (0, 2 * DK, 2 * DK + DV, C_V)
PROJ_R = (0, 2 * DK + DV, 2 * DK + 2 * DV, C_R)
PROJ_GATES = (2, 2 * D, 4 * D, C_GA)
PROJ_ALL = (PROJ_QK, PROJ_ALR, PROJ_UGV, PROJ_V, PROJ_R, PROJ_GATES)

TM_P = 512
TM_FFN = 512
NBS = 4
VMEM_LIMIT = 58 * 1024 * 1024


def _gla_constants():
  t = np.arange(CH)
  masks = []
  for lvl in range(N_LEVELS):
    m = 1 << lvl
    upper = t >= (t // (2 * m)) * (2 * m) + m
    same_pair = (t[:, None] // (2 * m)) == (t[None, :] // (2 * m))
    masks.append((upper[:, None] & (~upper)[None, :] & same_pair).astype(np.float32))
  masks.append(np.eye(CH, dtype=np.float32))
  ltri = (t[None, :] <= t[:, None]).astype(np.float32)
  return ltri, np.stack(masks)


def _level_exponents(g, b):
  t = lax.broadcasted_iota(jnp.int32, g.shape, 0)
  g_prev = pltpu.roll(g, 1, 0)
  g_next = pltpu.roll(g, CH - 1, 0)
  r4 = t & 3
  levels = [
      jnp.where((t & 1) == 1, g, 0.0),
      jnp.where(r4 == 0, g_next, jnp.where(r4 == 1, 0.0, jnp.where(r4 == 2, g, g + g_prev))),
  ]
  groups = CH // 8
  row = lambda i: jnp.broadcast_to(b[i:i + 1, :], (8, b.shape[1]))
  b3 = [row(8 * j + 3) for j in range(groups)]
  b7 = [row(8 * j + 7) for j in range(groups)]
  levels.append(-jnp.abs(b - jnp.concatenate(b3, axis=0)))
  for lvl in range(3, N_LEVELS):
    m = 1 << lvl
    ref = [b7[((8 * j) // (2 * m)) * (2 * m) // 8 + m // 8 - 1] for j in range(groups)]
    levels.append(-jnp.abs(b - jnp.concatenate(ref, axis=0)))
  to_end = -jnp.abs(b - jnp.concatenate([b7[groups - 1]] * groups, axis=0))
  return levels, to_end


def _dot(a, b):
  return jnp.dot(a, b, preferred_element_type=F32)


def _rms(x, w):
  ms = jnp.mean(x * x, axis=-1, keepdims=True)
  return x * lax.rsqrt(ms + EPS) * w


def _norm_mod(x, nw, sc, sh):
  if sc.shape[0] != 1:
    return (_rms(x, nw) * (1.0 + sc) + sh).astype(BF16)
  ms = jnp.mean(x * x, axis=-1, keepdims=True)
  return (x * lax.rsqrt(ms + EPS) * (nw * (1.0 + sc)) + sh).astype(BF16)


def _gelu(x):
  return 0.5 * x * (1.0 + lax.erf(x * (2.0 ** -0.5)))


def _log_sigmoid(z):
  return jnp.minimum(z, 0.0) - jnp.log1p(jnp.exp(-jnp.abs(z)))


def _const_spec(shape):
  nd = len(shape)
  return pl.BlockSpec(shape, lambda *_: (0,) * nd, pipeline_mode=pl.Buffered(1))


ADA_STEPS = 4


def _ada_kernel(c_ref, w_ref, b_ref, o_ref):
  c = c_ref[...]
  s = (c * jax.nn.sigmoid(c)).astype(BF16)
  o_ref[...] = _dot(s, w_ref[...].astype(BF16)) + b_ref[...]


def _ada_call(c_all, w_ada, b_ada):
  rows = c_all.shape[0]
  tn = CAST_MOD_FROM * D // ADA_STEPS
  assert tn % LANE == 0
  return pl.pallas_call(
      _ada_kernel,
      grid=(ADA_STEPS,),
      in_specs=[
          pl.BlockSpec((rows, D), lambda i: (0, 0)),
          pl.BlockSpec((D, tn), lambda i: (0, i)),
          pl.BlockSpec((1, tn), lambda i: (0, i)),
      ],
      out_specs=pl.BlockSpec((rows, tn), lambda i: (0, i)),
      out_shape=jax.ShapeDtypeStruct((rows, CAST_MOD_FROM * D), F32),
      compiler_params=pltpu.CompilerParams(
          dimension_semantics=("arbitrary",), vmem_limit_bytes=VMEM_LIMIT),
      name="ada",
  )(c_all, w_ada, b_ada)


def _ffn_rows(x_ref, sh_ref, sc_ref, g_ref, nw_ref, w13_ref, w2_ref, nf_ref, o_ref, mod_row=None):
  pick = (lambda ref: ref[...]) if mod_row is None else (lambda ref: ref[pl.ds(mod_row, 1), :])
  x = x_ref[...]
  h = _norm_mod(x, nw_ref[...], pick(sc_ref), pick(sh_ref))
  a = _dot(h, w13_ref[:, :F])
  b = _dot(h, w13_ref[:, F:])
  p = (a * jax.nn.sigmoid(a) * b).astype(BF16)
  out = x + 0.5 * pick(g_ref) * _dot(p, w2_ref[...])
  if nf_ref is not None:
    out = _rms(out, nf_ref[...])
  o_ref[...] = out


W_LOAD_CHUNKS = 8


def _load_cast(w_hbm, w_ref, buf_ref, sem_ref):
  rows = w_hbm.shape[0] // W_LOAD_CHUNKS
  copy = lambda k: pltpu.make_async_copy(
      w_hbm.at[pl.ds(k * rows, rows), :], buf_ref.at[k % 2], sem_ref.at[k % 2])
  copy(0).start()
  for k in range(W_LOAD_CHUNKS):
    if k + 1 < W_LOAD_CHUNKS:
      copy(k + 1).start()
    copy(k).wait()
    w_ref[k * rows:(k + 1) * rows, :] = buf_ref[k % 2].astype(BF16)


def _ffn_kernel(x_ref, sh_ref, sc_ref, g_ref, xs_hbm, shs_ref, scs_ref, gs_ref, nw_ref, w13_hbm,
                w2_hbm, wta_ref, wtalr_ref, wtb_ref, w_pa_ref, w_pb_ref, w_o_ref, n13_ref, n2_ref,
                c_ref, wada_ref, bada_ref,
                o_ref, os_ref, wa_ref, walr_ref, wb_ref, pa_ref, pb_ref, po_ref, o13_ref, o2_ref,
                modb_ref, w13_ref, w2_ref, buf13_ref, buf2_ref, sem13_ref, sem2_ref, xs_ref,
                semx_ref, *, tiles):
  i = pl.program_id(0)
  n = pl.num_programs(0) - 1

  xs_copy = pltpu.make_async_copy(xs_hbm.at[:, 0, :], xs_ref, semx_ref.at[0])

  @pl.when(i == 0)
  def _():
    xs_copy.start()
    _load_cast(w13_hbm, w13_ref, buf13_ref, sem13_ref)
    _load_cast(w2_hbm, w2_ref, buf2_ref, sem2_ref)

  @pl.when(i < n)
  def _():
    _ffn_rows(x_ref, sh_ref, sc_ref, g_ref, nw_ref, w13_ref, w2_ref, None, o_ref,
              mod_row=i // tiles)
    c = c_ref[...]
    modb_ref[...] = _dot((c * jax.nn.sigmoid(c)).astype(BF16),
                         wada_ref[...].astype(BF16)) + bada_ref[...]
    wa_ref[...] = wta_ref[...].T.astype(BF16)
    wb_ref[...] = wtb_ref[...].T.astype(BF16)
    alr = jnp.concatenate([wtalr_ref[...], jnp.zeros((LANE - LOWRANK, D), F32)], axis=0)
    walr_ref[...] = alr.T.astype(BF16)
    pa_ref[...] = w_pa_ref[...].astype(BF16)
    pb_ref[...] = w_pb_ref[...].astype(BF16)
    po_ref[...] = w_o_ref[...].astype(BF16)
    o13_ref[...] = n13_ref[...].astype(BF16)
    o2_ref[...] = n2_ref[...].astype(BF16)

  @pl.when(i == n)
  def _():
    xs_copy.wait()
    _ffn_rows(xs_ref, shs_ref, scs_ref, gs_ref, nw_ref, w13_ref, w2_ref, None, os_ref)


CAST_STEPS = 32
CAST_MOD_FROM = 3
CAST_MOD_COLS = 256
W2_CAST_ROWS = 2 * F // CAST_STEPS


def _mod_specs(rows_s, batch, segs):
  assert rows_s % batch == 0
  one = pl.Buffered(1)
  prompt = [pl.BlockSpec((batch, D), lambda *_, s=s: (rows_s // batch, s), pipeline_mode=one)
            for s in segs]
  sample = [pl.BlockSpec((rows_s, D), lambda *_, s=s: (0, s), pipeline_mode=one) for s in segs]
  return prompt, sample


def _ffn_call(x, xs, mod, segs, nw, w13, w2, cast_weights, *, rows_per_batch, tm):
  rows, rows_s = x.shape[0], xs.shape[0]
  n = rows // tm
  tiles = rows_per_batch // tm
  last = lambda i: jnp.minimum(i, n - 1)
  x_spec = pl.BlockSpec((tm, D), lambda i: (last(i), 0))
  xs_spec = pl.BlockSpec((rows_s, D), lambda i: (0, 0))
  mod_prompt, mod_sample = _mod_specs(rows_s, rows // rows_per_batch, segs)
  assert n == CAST_STEPS and D % CAST_STEPS == 0
  assert D % (16 * W_LOAD_CHUNKS) == 0 and F % (16 * W_LOAD_CHUNKS) == 0
  w_t, w_pa, w_pb, w_o, n13, n2, c_all, w_ada, b_ada = cast_weights
  mod_cols = (N_MOD - CAST_MOD_FROM) * D
  assert mod_cols % CAST_MOD_COLS == 0 and mod_cols // CAST_MOD_COLS <= CAST_STEPS
  mod_blk = lambda i: jnp.minimum(i, mod_cols // CAST_MOD_COLS - 1)
  mod_off = CAST_MOD_FROM * D // CAST_MOD_COLS
  lo = 2 * DK + 2 * DV
  n_a, n_b = lo // LANE, (w_t.shape[0] - lo - LOWRANK) // LANE
  assert n_a <= CAST_STEPS and n_b == CAST_STEPS
  rb = D // CAST_STEPS
  step = lambda i: jnp.minimum(i, CAST_STEPS - 1)
  a_blk = lambda i: jnp.minimum(i, n_a - 1)
  elem = lambda r: (pl.Element(r), pl.Element(D))
  row_blk = lambda cols: pl.BlockSpec((rb, cols), lambda i: (step(i), 0))
  w2_blk = pl.BlockSpec((W2_CAST_ROWS, D), lambda i: (step(i) // 2, 0))
  in_specs = [x_spec] + mod_prompt + [pl.BlockSpec(memory_space=pl.ANY)] + mod_sample + [
      _const_spec((1, D)), pl.BlockSpec(memory_space=pl.ANY), pl.BlockSpec(memory_space=pl.ANY),
      pl.BlockSpec(elem(LANE), lambda i: (pl.multiple_of(LANE * a_blk(i), LANE), 0)),
      pl.BlockSpec(elem(LOWRANK), lambda i: (lo, 0)),
      pl.BlockSpec(elem(LANE), lambda i: (pl.multiple_of(lo + LOWRANK + LANE * step(i), LOWRANK), 0)),
      row_blk(D), row_blk(D), row_blk(D), row_blk(2 * F), w2_blk,
      _const_spec(c_all.shape),
      pl.BlockSpec((D, CAST_MOD_COLS), lambda i: (0, mod_off + mod_blk(i))),
      pl.BlockSpec((1, CAST_MOD_COLS), lambda i: (0, mod_off + mod_blk(i)))]
  out_specs = [
      x_spec, xs_spec,
      pl.BlockSpec((D, LANE), lambda i: (0, a_blk(i))),
      pl.BlockSpec((D, LANE), lambda i: (0, 0)),
      pl.BlockSpec((D, LANE), lambda i: (0, step(i))),
      row_blk(D), row_blk(D), row_blk(D), row_blk(2 * F), w2_blk,
      pl.BlockSpec((c_all.shape[0], CAST_MOD_COLS), lambda i: (0, mod_blk(i)))]
  bf = lambda r, c: jax.ShapeDtypeStruct((r, c), BF16)
  out_shape = [jax.ShapeDtypeStruct((rows, D), F32), jax.ShapeDtypeStruct((rows_s, D), F32),
               bf(D, lo), bf(D, LANE), bf(D, n_b * LANE),
               bf(D, D), bf(D, D), bf(D, D), bf(D, 2 * F), bf(F, D),
               jax.ShapeDtypeStruct((c_all.shape[0], mod_cols), F32)]
  return pl.pallas_call(
      functools.partial(_ffn_kernel, tiles=tiles),
      grid=(n + 1,),
      in_specs=in_specs,
      out_specs=out_specs,
      out_shape=out_shape,
      scratch_shapes=[
          pltpu.VMEM((D, 2 * F), BF16),
          pltpu.VMEM((F, D), BF16),
          pltpu.VMEM((2, D // W_LOAD_CHUNKS, 2 * F), F32),
          pltpu.VMEM((2, F // W_LOAD_CHUNKS, D), F32),
          pltpu.SemaphoreType.DMA((2,)),
          pltpu.SemaphoreType.DMA((2,)),
          pltpu.VMEM((rows_s, D), F32),
          pltpu.SemaphoreType.DMA((1,)),
      ],
      compiler_params=pltpu.CompilerParams(
          dimension_semantics=("arbitrary",), vmem_limit_bytes=VMEM_LIMIT),
      name="ffn1",
  )(x, mod, mod, mod, xs, mod, mod, mod, nw, w13, w2,
    w_t, w_t, w_t, w_pa, w_pb, w_o, n13, n2, c_all, w_ada, b_ada)


def _project_piece(h, w_refs, p_ref, piece):
  idx, lo, hi, dst = piece
  p_ref[:, dst:dst + hi - lo] = _dot(h, w_refs[idx][:, lo:hi])


def _log_decay(p_ref, w_a2_ref, b_a_ref):
  a_lr = p_ref[:, C_ALR:C_ALR + LANE].astype(BF16)
  z = _dot(a_lr, w_a2_ref[...]) + b_a_ref[...]
  return _log_sigmoid(z) * (1.0 / TAU)


def _head_norm_gate(o, p_ref, gnw_ref):
  parts = []
  for hh in range(HEADS):
    sl = slice(hh * HV, (hh + 1) * HV)
    parts.append(_rms(o[:, sl], gnw_ref[:, sl]))
  r = p_ref[:, C_R:C_R + DV]
  return jnp.concatenate(parts, axis=-1) * (r * jax.nn.sigmoid(r))


def _gv_norm(p_ref, lnw_ref, lnb_ref):
  gv = _gelu(p_ref[:, C_GV:C_GV + D])
  mu = jnp.mean(gv, axis=-1, keepdims=True)
  d = gv - mu
  var = jnp.mean(d * d, axis=-1, keepdims=True)
  return d * lax.rsqrt(var + EPS) * lnw_ref[...] + lnb_ref[...]


def _merge_out(x, g2, ya, yb, p_ref, w_o_ref):
  ga = p_ref[:, C_GA:C_GA + D]
  gb = p_ref[:, C_GB:C_GB + D]
  merged = (jax.nn.sigmoid(ga) * ya + jax.nn.sigmoid(gb) * yb).astype(BF16)
  return x + g2 * _dot(merged, w_o_ref[...])


def _mix_prompt_kernel(x_ref, sh_ref, sc_ref, g2_ref, nw_ref,
                       wa_ref, walr_ref, wb_ref, w_a2_ref, b_a_ref,
                       ltri_ref, mask_ref, gnw_ref, w_pa_ref, lnw_ref, lnb_ref, ws_ref, bs_ref,
                       w_pb_ref, w_o_ref, o_ref, s_out_ref,
                       st_ref, p_ref, oa_ref, mx_ref, kt_ref, s_ref):
  j = pl.program_id(1)
  tm = TM_P
  w_refs = (wa_ref, walr_ref, wb_ref)
  batch_row = pl.ds(pl.program_id(0), 1)
  sh, sc, g2 = sh_ref[batch_row, :], sc_ref[batch_row, :], g2_ref[batch_row, :]

  @pl.when(j == 0)
  def _():
    st_ref[...] = jnp.zeros_like(st_ref)

  ltri = ltri_ref[...]
  ti = lax.broadcasted_iota(jnp.int32, (GM_CHUNK, GM_CHUNK), 0)
  si = lax.broadcasted_iota(jnp.int32, (GM_CHUNK, GM_CHUNK), 1)

  def mix_stages():
    oa = oa_ref
    mx = mx_ref
    env = {}

    def gate():
      env["g"] = _log_decay(p_ref, w_a2_ref, b_a_ref) * LOG2E

    def cumsum():
      cums = []
      for c in range(tm // CH):
        gc = env["g"][c * CH:(c + 1) * CH, :]
        g_hi = gc.astype(BF16)
        g_lo = (gc - g_hi.astype(F32)).astype(BF16)
        cums.append(_dot(ltri, g_hi) + _dot(ltri, g_lo))
      env["cums"] = cums

    def prep():
      decays = []
      qk = {}
      for c in range(tm // CH):
        rows = slice(c * CH, (c + 1) * CH)
        b = env["cums"][c]
        e_lvl, e_end = _level_exponents(env["g"][rows, :], b)
        q = (p_ref[rows, C_Q:C_Q + DK] * (HK ** -0.5)).astype(BF16)
        k = p_ref[rows, C_K:C_K + DK].astype(BF16)
        for lvl in range(N_LEVELS):
          e = jnp.exp2(e_lvl[lvl]).astype(BF16)
          qk[c, 2 * lvl] = q * e
          qk[c, 2 * lvl + 1] = k * e
        qk[c, 2 * N_LEVELS] = q
        qk[c, 2 * N_LEVELS + 1] = k
        qk[c, 2 * N_LEVELS + 2] = q * jnp.exp2(b).astype(BF16)
        qk[c, 2 * N_LEVELS + 3] = k * jnp.exp2(e_end).astype(BF16)
        decays.append(jnp.exp2(b[CH - 1:CH, :]))
      env["decays"] = decays
      env["qk"] = qk
      env["gvn"] = _gv_norm(p_ref, lnw_ref, lnb_ref).astype(BF16)
      env["u"] = _gelu(p_ref[:, C_U:C_U + D])
      env["pending"] = None

    def level_scores(c, hh):
      ck = slice(hh * HK, (hh + 1) * HK)
      scores = None
      for lvl in range(N_LEVELS + 1):
        kt_ref[lvl] = env["qk"][c, 2 * lvl + 1][:, ck].T
        part = mask_ref[lvl] * _dot(env["qk"][c, 2 * lvl][:, ck], kt_ref[lvl])
        scores = part if scores is None else scores + part
      return scores.astype(BF16)

    def finish(c, hh, scores):
      rows = slice(c * CH, (c + 1) * CH)
      ck = slice(hh * HK, (hh + 1) * HK)
      v32 = p_ref[rows, C_V + hh * HV:C_V + (hh + 1) * HV]
      st = st_ref[hh]
      s_ref[hh] = st.T.astype(BF16)
      oa[rows, hh * HV:(hh + 1) * HV] = (
          _dot(scores, v32.astype(BF16))
          + _dot(env["qk"][c, 2 * N_LEVELS + 2][:, ck], s_ref[hh]))
      st_ref[hh] = (st * env["decays"][c][:, ck]
                    + _dot(v32.T.astype(BF16), env["qk"][c, 2 * N_LEVELS + 3][:, ck]))

    def gla(c):
      def run():
        for hh in range(HEADS):
          scores = level_scores(c, hh)
          if env["pending"] is not None:
            finish(*env["pending"])
          env["pending"] = (c, hh, scores)
      return run

    def gmlp():
      for gg in range(GROUPS):
        ws = jnp.where(si <= ti, ws_ref[gg], 0.0).astype(BF16)
        cols = slice(gg * GC, (gg + 1) * GC)
        for c in range(tm // GM_CHUNK):
          rows = slice(c * GM_CHUNK, (c + 1) * GM_CHUNK)
          mx[rows, cols] = _dot(ws, env["gvn"][rows, cols]) + bs_ref[:, cols]
      finish(*env["pending"])

    def branches():
      env["yb"] = _dot((env["u"] * mx[...]).astype(BF16), w_pb_ref[...])
      env["ya"] = _dot(_head_norm_gate(oa[...], p_ref, gnw_ref).astype(BF16), w_pa_ref[...])

    def out():
      ga = p_ref[:, C_GA:C_GA + D]
      gb = p_ref[:, C_GB:C_GB + D]
      merged = (jax.nn.sigmoid(ga) * env["ya"] + jax.nn.sigmoid(gb) * env["yb"]).astype(BF16)
      for r0 in range(0, tm, tm // 2):
        sl = slice(r0, r0 + tm // 2)
        o_ref[sl, :] = x_ref[sl, :] + g2 * _dot(merged[sl, :], w_o_ref[...])

    return gate, cumsum, prep, [gla(c) for c in range(tm // CH)], gmlp, branches, out

  gate, cumsum, prep, glas, gmlp, branches, out = mix_stages()
  halves = []
  for r0 in range(0, tm, tm // 2):
    sl = slice(r0, r0 + tm // 2)
    halves.append(_norm_mod(x_ref[sl, :], nw_ref[...], sc, sh))
    idx, lo, hi, dst = PROJ_QK
    p_ref[sl, dst:dst + hi - lo] = _dot(halves[-1], w_refs[idx][:, lo:hi])
  h = jnp.concatenate(halves, axis=0)
  proj = lambda piece: _project_piece(h, w_refs, p_ref, piece)
  proj(PROJ_ALR)
  proj(PROJ_UGV)
  gate()
  proj(PROJ_V)
  cumsum()
  proj(PROJ_R)
  proj(PROJ_GATES)
  prep()
  for gla_chunk in glas:
    gla_chunk()
  gmlp()
  branches()
  out()

  @pl.when(j == pl.num_programs(1) - 1)
  def _():
    for hh in range(HEADS):
      s_out_ref[hh] = st_ref[hh].T


def _mix_prompt_call(x, mod, segs, nw, w_in3, w_a2, b_a, ltri, masks, gnw, w_pa, lnw, lnb, ws,
                     bs_full, w_pb, w_o, *, batch, seq):
  tm = TM_P
  tiles = seq // tm
  row_spec = pl.BlockSpec((tm, D), lambda b, j: (b * tiles + j, 0))

  mod_prompt, _ = _mod_specs(mod.shape[0] - batch, batch, segs)
  wa, walr, wb = w_in3
  in_specs = [
      row_spec, *mod_prompt,
      _const_spec((1, D)), _const_spec(wa.shape), _const_spec(walr.shape), _const_spec(wb.shape),
      _const_spec((LANE, DK)), _const_spec((1, DK)),
      _const_spec(ltri.shape), _const_spec(masks.shape), _const_spec((1, DV)),
      _const_spec((DV, D)), _const_spec((1, D)), _const_spec((1, D)),
      _const_spec((GROUPS, GM_CHUNK, GM_CHUNK)), _const_spec((GM_CHUNK, D)),
      _const_spec((D, D)), _const_spec((D, D)),
  ]
  out_specs = [
      row_spec,
      pl.BlockSpec((None, None, HEADS, HK, HV), lambda b, j: (0, b, 0, 0, 0)),
  ]
  return pl.pallas_call(
      _mix_prompt_kernel,
      grid=(batch, tiles),
      in_specs=in_specs,
      out_specs=out_specs,
      out_shape=[jax.ShapeDtypeStruct((batch * seq, D), F32),
                 jax.ShapeDtypeStruct((1, batch, HEADS, HK, HV), F32)],
      scratch_shapes=[
          pltpu.VMEM((HEADS, HV, HK), F32),
          pltpu.VMEM((tm, D_P), F32),
          pltpu.VMEM((tm, DV), F32),
          pltpu.VMEM((tm, D), F32),
          pltpu.VMEM((N_LEVELS + 1, HK, CH), BF16),
          pltpu.VMEM((HEADS, HK, HV), BF16),
      ],
      compiler_params=pltpu.CompilerParams(
          dimension_semantics=("arbitrary", "arbitrary"), vmem_limit_bytes=VMEM_LIMIT),
      name="mix_prompt",
  )(x, mod, mod, mod, nw, wa, walr, wb, w_a2, b_a, ltri, masks, gnw, w_pa,
    lnw, lnb, ws, bs_full, w_pb, w_o)


def _sample_proj_kernel(x_ref, sh_ref, sc_ref, nw_ref, wa_ref, walr_ref, wb_ref, w_a2_ref, b_a_ref,
                        p_ref, col_ref, v_ref, oa_ref):
  h = _norm_mod(x_ref[...], nw_ref[...], sc_ref[...], sh_ref[...])
  for piece in PROJ_ALL:
    _project_piece(h, (wa_ref, walr_ref, wb_ref), p_ref, piece)
  g = _log_decay(p_ref, w_a2_ref, b_a_ref)
  q = p_ref[:, C_Q:C_Q + DK] * (HK ** -0.5)
  k = p_ref[:, C_K:C_K + DK]
  eb = jnp.exp(g)
  qe = q * eb
  v_ref[...] = p_ref[:, C_V:C_V + DV]
  for hh in range(HEADS):
    ck = slice(hh * HK, (hh + 1) * HK)
    col_ref[0, ck, :] = eb[:, ck].T
    col_ref[1, ck, :] = qe[:, ck].T
    col_ref[2, ck, :] = k[:, ck].T
    qk = jnp.sum(q[:, ck] * k[:, ck], axis=-1, keepdims=True)
    oa_ref[:, hh * HV:(hh + 1) * HV] = qk * p_ref[:, C_V + hh * HV:C_V + (hh + 1) * HV]


def _sample_state_rows(i, col_ref, v_ref, s_ref, s_out_ref, os_ref):
  rows = os_ref.shape[0]
  shift = (rows - i * NBS) % rows
  eb_t, qe_t, k_t = (pltpu.roll(col_ref[a], shift, 1) for a in range(3))
  for nl in range(NBS):
    n = i * NBS + nl
    for hh in range(HEADS):
      ck = slice(hh * HK, (hh + 1) * HK)
      cv = slice(hh * HV, (hh + 1) * HV)
      st = s_ref[nl, hh]
      o_inter = jnp.sum(qe_t[ck, nl:nl + 1] * st, axis=0, keepdims=True)
      os_ref[pl.ds(n, 1), cv] = os_ref[pl.ds(n, 1), cv] + o_inter
      s_out_ref[nl, hh] = eb_t[ck, nl:nl + 1] * st + k_t[ck, nl:nl + 1] * v_ref[pl.ds(n, 1), cv]


def _ffn_stream_kernel(x_ref, sh_ref, sc_ref, g_ref, nw_ref, w13_ref, w2_ref, nf_ref,
                       col_ref, v_ref, oa0_ref, s_ref,
                       xs_ref, p_ref, g2_ref, shs_ref, scs_ref, gs_ref, gnw_ref, w_pa_ref, lnw_ref,
                       lnb_ref, ws0_ref, bs0_ref, w_pb_ref, w_o_ref,
                       o_ref, os_ref, s_out_ref, ys_hbm, gvn_hbm, x2_ref, ys_ref, gvn_ref, sem_ref,
                       *, tiles):
  i = pl.program_id(0)
  n = pl.num_programs(0) - 1

  @pl.when(i == 0)
  def _():
    os_ref[...] = oa0_ref[...]

  @pl.when(i < n)
  def _():
    _sample_state_rows(i, col_ref, v_ref, s_ref, s_out_ref, os_ref)
    _ffn_rows(x_ref, sh_ref, sc_ref, g_ref, nw_ref, w13_ref, w2_ref, nf_ref, o_ref,
              mod_row=i // tiles)

  @pl.when(i == n)
  def _():
    ya = _dot(_head_norm_gate(os_ref[...], p_ref, gnw_ref).astype(BF16), w_pa_ref[...])
    gvn_copy = pltpu.make_async_copy(gvn_ref, gvn_hbm.at[0, :, 0, :], sem_ref.at[0])
    ys_copy = pltpu.make_async_copy(ys_ref, ys_hbm.at[:, 0, :], sem_ref.at[1])
    gvn = _gv_norm(p_ref, lnw_ref, lnb_ref)
    gvn_ref[...] = gvn
    gvn_copy.start()
    mixed = ws0_ref[...] * gvn + bs0_ref[...]
    u = _gelu(p_ref[:, C_U:C_U + D])
    yb = _dot((u * mixed).astype(BF16), w_pb_ref[...])
    x2_ref[...] = _merge_out(xs_ref[...], g2_ref[...], ya, yb, p_ref, w_o_ref)
    _ffn_rows(x2_ref, shs_ref, scs_ref, gs_ref, nw_ref, w13_ref, w2_ref, nf_ref, ys_ref)
    ys_copy.start()
    gvn_copy.wait()
    ys_copy.wait()


def _sample_proj_call(x, mod, segs, nw, w_in3, w_a2, b_a):
  rows = x.shape[0]
  _, mod_sample = _mod_specs(rows, mod.shape[0] - rows, segs[:2])
  wa, walr, wb = w_in3
  shapes = [(rows, D_P), (3, DK, rows), (rows, DV), (rows, DV)]
  return pl.pallas_call(
      _sample_proj_kernel,
      grid=(1,),
      in_specs=[_const_spec((rows, D)), *mod_sample, _const_spec((1, D)), _const_spec(wa.shape),
                _const_spec(walr.shape), _const_spec(wb.shape), _const_spec((LANE, DK)),
                _const_spec((1, DK))],
      out_specs=[pl.BlockSpec(s, lambda i, nd=len(s): (0,) * nd) for s in shapes],
      out_shape=[jax.ShapeDtypeStruct(s, F32) for s in shapes],
      compiler_params=pltpu.CompilerParams(
          dimension_semantics=("arbitrary",), vmem_limit_bytes=VMEM_LIMIT),
      name="sample_proj",
  )(x, mod, mod, nw, wa, walr, wb, w_a2, b_a)


def _ffn_stream_call(x, mod, segs, nw, w13, w2, nf, cols, v_s, oa0, state,
                     xs, p_s, mix_segs, gnw, w_pa, lnw, lnb, ws0, bs0, w_pb, w_o, *, rows_per_batch, tm):
  rows, rows_s = x.shape[0], xs.shape[0]
  n = rows // tm
  assert n * NBS == rows_s
  last = lambda i: jnp.minimum(i, n - 1)
  x_spec = pl.BlockSpec((tm, D), lambda i: (last(i), 0))
  mod_prompt, mod_sample = _mod_specs(rows_s, rows // rows_per_batch, segs)
  _, mod_g2 = _mod_specs(rows_s, rows // rows_per_batch, mix_segs[2:])
  state_spec = pl.BlockSpec((None, NBS, HEADS, HK, HV), lambda i: (0, last(i), 0, 0, 0))
  hbm_out = pl.BlockSpec(memory_space=pl.ANY)
  outs = pl.pallas_call(
      functools.partial(_ffn_stream_kernel, tiles=rows_per_batch // tm),
      grid=(n + 1,),
      in_specs=[x_spec, *mod_prompt, _const_spec((1, D)), _const_spec((D, 2 * F)),
                _const_spec((F, D)), _const_spec((1, D)), _const_spec(cols.shape),
                _const_spec((rows_s, DV)), _const_spec((rows_s, DV)), state_spec,
                _const_spec((rows_s, D)), _const_spec((rows_s, D_P)), *mod_g2, *mod_sample,
                _const_spec((1, DV)), _const_spec((DV, D)), _const_spec((1, D)), _const_spec((1, D)),
                _const_spec((1, D)), _const_spec((1, D)), _const_spec((D, D)), _const_spec((D, D))],
      out_specs=[x_spec, pl.BlockSpec((rows_s, DV), lambda i: (0, 0)), state_spec, hbm_out, hbm_out],
      out_shape=[jax.ShapeDtypeStruct((rows, D), F32), jax.ShapeDtypeStruct((rows_s, DV), F32),
                 jax.ShapeDtypeStruct(state.shape, F32), jax.ShapeDtypeStruct((rows_s, 1, D), F32),
                 jax.ShapeDtypeStruct((1, rows_s, 1, D), F32)],
      scratch_shapes=[pltpu.VMEM((rows_s, D), F32), pltpu.VMEM((rows_s, D), F32),
                      pltpu.VMEM((rows_s, D), F32), pltpu.SemaphoreType.DMA((2,))],
      compiler_params=pltpu.CompilerParams(
          dimension_semantics=("arbitrary",), vmem_limit_bytes=VMEM_LIMIT),
      name="ffn2",
  )(x, mod, mod, mod, nw, w13, w2, nf, cols, v_s, oa0, state,
    xs, p_s, mod, mod, mod, mod, gnw, w_pa, lnw, lnb, ws0, bs0, w_pb, w_o)
  y, _, s_new, ys, gvn = outs
  return y, s_new, ys, gvn


def kernel(x_prompt, x_sample, state_gla, c_prompt, c_sample, w_ada, b_ada, norm1_w, ffn1_w13, ffn1_w2, norm2_w, w_in, w_a2, b_a, gla_norm_w, w_pa, gm_ln_w, gm_ln_b, gm_ws, gm_bs, w_pb, w_o, norm3_w, ffn2_w13, ffn2_w2, normf_w):
  batch, seq, _ = x_prompt.shape
  dec = x_sample.shape[0]
  depth = w_ada.shape[0]
  assert depth == 1 and x_sample.shape[1] == 1
  assert seq % TM_P == 0 and seq % TM_FFN == 0

  ltri_np, masks_np = _gla_constants()
  ltri = jnp.asarray(ltri_np, BF16)
  masks = jnp.asarray(masks_np, F32)

  xp = x_prompt.reshape(batch * seq, D)
  xs = x_sample
  l = 0
  c_all = jnp.concatenate([c_sample, c_prompt], axis=0)
  b_ada_row = b_ada[l].reshape(1, N_MOD * D)
  mod_a = _ada_call(c_all, w_ada[l], b_ada_row)

  row = lambda a: a.reshape(1, -1)
  w_a2_p = jnp.pad(w_a2[l], ((0, LANE - LOWRANK), (0, 0))).astype(BF16)
  gnw = gla_norm_w[l].reshape(1, DV)
  bs_full = jnp.repeat(gm_bs[l].T, GC, axis=1)
  ws0 = jnp.repeat(gm_ws[l][:, 0, 0], GC).reshape(1, D)
  bs0 = bs_full[0:1]
  normf = row(normf_w)

  cast_weights = (jnp.transpose(w_in[l]), w_pa[l], w_pb[l], w_o[l], ffn2_w13[l], ffn2_w2[l],
                  c_all, w_ada[l], b_ada_row)
  (xp, xs, wa, walr, wb, w_pa_b, w_pb_b, w_o_b, w13_2, w2_2, mod_b) = _ffn_call(
      xp, xs, mod_a, (0, 1, 2), row(norm1_w[l]), ffn1_w13[l], ffn1_w2[l], cast_weights,
      rows_per_batch=seq, tm=TM_FFN)
  w_in3 = (wa, walr, wb)
  mix_segs = tuple(s - CAST_MOD_FROM for s in (3, 4, 5))
  ffn2_segs = tuple(s - CAST_MOD_FROM for s in (6, 7, 8))
  xp, s_prompt = _mix_prompt_call(xp, mod_b, mix_segs, row(norm2_w[l]), w_in3, w_a2_p, row(b_a[l]),
                                  ltri, masks, gnw, w_pa_b, row(gm_ln_w[l]), row(gm_ln_b[l]),
                                  gm_ws[l], bs_full, w_pb_b, w_o_b, batch=batch, seq=seq)
  p_s, cols, v_s, oa0 = _sample_proj_call(xs, mod_b, mix_segs, row(norm2_w[l]), w_in3, w_a2_p,
                                          row(b_a[l]))
  yp, s_sample, ys, gvn = _ffn_stream_call(
      xp, mod_b, ffn2_segs, row(norm3_w[l]), w13_2, w2_2, normf, cols, v_s, oa0, state_gla,
      xs, p_s, mix_segs, gnw, w_pa_b, row(gm_ln_w[l]), row(gm_ln_b[l]), ws0, bs0, w_pb_b, w_o_b,
      rows_per_batch=seq, tm=TM_FFN)

  return yp.reshape(batch, seq, D), ys, s_prompt, s_sample, gvn
```
